```python
import jax, jax.numpy as jnp
from jax import lax
import numpy as np

D_MODEL = 2048
BATCH = 2
SEQ = 8192
DEPTH = 1
DEC_BATCH = 8
DEC_SEQ = 64
PAST_LEN = 4096

CHUNK = 64
Q_BLOCK = 128
D_POOL = 1024
POOL_WINDOWS = (2, 4, 8, 16)
POOL_GROUP = D_POOL // len(POOL_WINDOWS)
POOL_STATE = max(POOL_WINDOWS) - 1
N_HEADS = 16
Q_LORA = 512
KV_LORA = 512
NOPE_DIM = 128
ROPE_DIM = 64
V_DIM = 128
QK_DIM = NOPE_DIM + ROPE_DIM
ATTN_SCALE = QK_DIM ** -0.5
ROPE_BASE = 10000.0
D_FF = 6144
CONV_W = 3
N_BRANCH = 2
D_IN = D_POOL + Q_LORA + KV_LORA + ROPE_DIM + N_BRANCH * D_MODEL
SPLITS = [D_POOL, D_POOL + Q_LORA, D_POOL + Q_LORA + KV_LORA, D_POOL + Q_LORA + KV_LORA + ROPE_DIM]
EPS = 1e-6

kernel_name = "hybrid_pool_mla_convffn_stream_step"


def rms_norm(x, g):
    xf = x.astype(jnp.float32)
    y = xf * lax.rsqrt(jnp.mean(xf * xf, axis=-1, keepdims=True) + EPS)
    return (y * g.astype(jnp.float32)).astype(x.dtype)


def rope_tables(pos0, length):
    pos = (pos0 + jnp.arange(length)).astype(jnp.float32)
    inv = ROPE_BASE ** (-(jnp.arange(ROPE_DIM // 2, dtype=jnp.float32) * 2.0 / ROPE_DIM))
    ang = pos[:, None] * inv[None, :]
    return jnp.cos(ang), jnp.sin(ang)


def apply_rope(x, cos, sin):
    xf = x.astype(jnp.float32)
    x1, x2 = xf[..., :ROPE_DIM // 2], xf[..., ROPE_DIM // 2:]
    return jnp.concatenate([x1 * cos - x2 * sin, x2 * cos + x1 * sin], axis=-1).astype(x.dtype)


def pool_mixer(z, prefix, pos0, pool_w, pool_scale):
    L = z.shape[1]
    ext = jnp.concatenate([prefix.astype(z.dtype), z], axis=1)
    zf = ext.astype(jnp.float32)
    csum = jnp.concatenate([jnp.zeros_like(zf[:, :1]), jnp.cumsum(zf, axis=1)], axis=1)
    end = csum[:, POOL_STATE + 1:]
    pos = pos0 + jnp.arange(L)
    outs = []
    for g, w in enumerate(POOL_WINDOWS):
        sl = slice(g * POOL_GROUP, (g + 1) * POOL_GROUP)
        start = csum[:, POOL_STATE + 1 - w: POOL_STATE + 1 - w + L, sl]
        cnt = jnp.minimum(pos + 1, w).astype(jnp.float32)[None, :, None]
        diff = (end[..., sl] - start) / cnt - zf[:, POOL_STATE:, sl]
        outs.append(jnp.einsum('blc,cd->bld', diff.astype(z.dtype), pool_w[g]))
    y = jnp.concatenate(outs, axis=-1) * pool_scale
    return y, ext[:, -POOL_STATE:]


def causal_dwconv(a, prefix, conv_w, conv_b):
    L = a.shape[1]
    ext = jnp.concatenate([prefix.astype(a.dtype), a], axis=1)
    y = conv_b + ext[:, 0:L] * conv_w[0]
    for k in range(1, CONV_W):
        y = y + ext[:, k:k + L] * conv_w[k]
    return y, ext[:, -(CONV_W - 1):]


def mla_block_causal(q_nope, q_rope, ckv, kr, w_uk, w_uv):
    B, L = ckv.shape[0], ckv.shape[1]
    k_nope = jnp.einsum('blc,chn->blhn', ckv, w_uk)
    v = jnp.einsum('blc,chv->blhv', ckv, w_uv)
    nb = L // Q_BLOCK
    key_chunk = jnp.arange(L) // CHUNK
    qn = q_nope.reshape(B, nb, Q_BLOCK, N_HEADS, NOPE_DIM).swapaxes(0, 1)
    qr = q_rope.reshape(B, nb, Q_BLOCK, N_HEADS, ROPE_DIM).swapaxes(0, 1)

    def one_block(args):
        qn_b, qr_b, i = args
        s = (jnp.einsum('bqhn,bkhn->bhqk', qn_b, k_nope, preferred_element_type=jnp.float32)
             + jnp.einsum('bqhr,bkr->bhqk', qr_b, kr, preferred_element_type=jnp.float32)) * ATTN_SCALE
        q_chunk = (i * Q_BLOCK + jnp.arange(Q_BLOCK)) // CHUNK
        mask = key_chunk[None, :] <= q_chunk[:, None]
        s = jnp.where(mask[None, None], s, -jnp.inf)
        p = jax.nn.softmax(s, axis=-1).astype(v.dtype)
        return jnp.einsum('bhqk,bkhv->bqhv', p, v)

    o = lax.map(one_block, (qn, qr, jnp.arange(nb)))
    return o.swapaxes(0, 1).reshape(B, L, N_HEADS, V_DIM)


def mla_with_cache(q_nope, q_rope, ckv_new, kr_new, ckv_cache, kr_cache, w_uk, w_uv):
    ckv = jnp.concatenate([ckv_cache.astype(ckv_new.dtype), ckv_new], axis=1)
    kr = jnp.concatenate([kr_cache.astype(kr_new.dtype), kr_new], axis=1)
    q_lat = jnp.einsum('bshn,chn->bshc', q_nope, w_uk)
    s = (jnp.einsum('bshc,btc->bhst', q_lat, ckv, preferred_element_type=jnp.float32)
         + jnp.einsum('bshr,btr->bhst', q_rope, kr, preferred_element_type=jnp.float32)) * ATTN_SCALE
    p = jax.nn.softmax(s, axis=-1).astype(ckv.dtype)
    o_lat = jnp.einsum('bhst,btc->bshc', p, ckv)
    return jnp.einsum('bshc,chv->bshv', o_lat, w_uv)


def trunk_layer(x, pool_prefix, conv_prefix, ckv_cache, kr_cache, pos0,
                norm1_g, w_in, pool_w, pool_scale, w_pool_out, q_norm_g, w_uq, kv_norm_g,
                w_uk, w_uv, w_mla_out, w_out, norm2_g, w_up, conv_w, conv_b, w_down):
    B, L, _ = x.shape
    u = rms_norm(x, norm1_g)
    proj = u @ w_in
    z_pool, c_q, c_kv, k_r, gate_logits = jnp.split(proj, SPLITS, axis=-1)
    y_pool, new_pool = pool_mixer(z_pool, pool_prefix, pos0, pool_w, pool_scale)
    br_a = y_pool @ w_pool_out
    c_q = rms_norm(c_q, q_norm_g)
    q = (c_q @ w_uq).reshape(B, L, N_HEADS, QK_DIM)
    cos, sin = rope_tables(pos0, L)
    q_nope = q[..., :NOPE_DIM]
    q_rope = apply_rope(q[..., NOPE_DIM:], cos[:, None, :], sin[:, None, :])
    c_kv = rms_norm(c_kv, kv_norm_g)
    k_r = apply_rope(k_r, cos, sin)
    if ckv_cache is None:
        o = mla_block_causal(q_nope, q_rope, c_kv, k_r, w_uk, w_uv)
    else:
        o = mla_with_cache(q_nope, q_rope, c_kv, k_r, ckv_cache, kr_cache, w_uk, w_uv)
    br_b = o.reshape(B, L, N_HEADS * V_DIM) @ w_mla_out
    gates = jax.nn.sigmoid(gate_logits.astype(jnp.float32)).astype(x.dtype).reshape(B, L, N_BRANCH, D_MODEL)
    h = x + (gates[:, :, 0] * br_a + gates[:, :, 1] * br_b) @ w_out
    up = rms_norm(h, norm2_g) @ w_up
    a, b = up[..., :D_FF], up[..., D_FF:]
    a_c, new_conv = causal_dwconv(a, conv_prefix, conv_w, conv_b)
    h = h + (jax.nn.gelu(a_c, approximate=False) * b) @ w_down
    return h, new_pool, c_kv, k_r, new_conv


def setup_inputs(seed: int = 0) -> dict:
    key = jax.random.key(seed)
    ks = jax.random.split(key, 32)
    f32 = jnp.float32
    nrm = lambda k, shape, scale: jax.random.normal(k, shape, f32) * scale
    gain = lambda k, shape: 1.0 + 0.02 * jax.random.normal(k, shape, f32)
    return {
        "x_prompt": nrm(ks[0], (BATCH, SEQ, D_MODEL), 1.0),
        "x_sample": nrm(ks[1], (DEC_BATCH, DEC_SEQ, D_MODEL), 1.0),
        "cache_ckv": nrm(ks[2], (DEPTH, DEC_BATCH, PAST_LEN, KV_LORA), 1.0),
        "cache_krope": nrm(ks[3], (DEPTH, DEC_BATCH, PAST_LEN, ROPE_DIM), 1.0),
        "state_pool": nrm(ks[4], (DEPTH, DEC_BATCH, POOL_STATE, D_POOL), 1.0),
        "state_conv": nrm(ks[5], (DEPTH, DEC_BATCH, CONV_W - 1, D_FF), 1.0),
        "norm1_g": gain(ks[6], (DEPTH, D_MODEL)),
        "w_in": nrm(ks[7], (DEPTH, D_MODEL, D_IN), D_MODEL ** -0.5),
        "pool_w": nrm(ks[8], (DEPTH, len(POOL_WINDOWS), POOL_GROUP, POOL_GROUP), POOL_GROUP ** -0.5),
        "pool_scale": gain(ks[9], (DEPTH, D_POOL)),
        "w_pool_out": nrm(ks[10], (DEPTH, D_POOL, D_MODEL), D_POOL ** -0.5),
        "q_norm_g": gain(ks[11], (DEPTH, Q_LORA)),
        "w_uq": nrm(ks[12], (DEPTH, Q_LORA, N_HEADS * QK_DIM), Q_LORA ** -0.5),
        "kv_norm_g": gain(ks[13], (DEPTH, KV_LORA)),
        "w_uk": nrm(ks[14], (DEPTH, KV_LORA, N_HEADS, NOPE_DIM), KV_LORA ** -0.5),
        "w_uv": nrm(ks[15], (DEPTH, KV_LORA, N_HEADS, V_DIM), KV_LORA ** -0.5),
        "w_mla_out": nrm(ks[16], (DEPTH, N_HEADS * V_DIM, D_MODEL), (N_HEADS * V_DIM) ** -0.5),
        "w_out": nrm(ks[17], (DEPTH, D_MODEL, D_MODEL), D_MODEL ** -0.5),
        "norm2_g": gain(ks[18], (DEPTH, D_MODEL)),
        "w_up": nrm(ks[19], (DEPTH, D_MODEL, 2 * D_FF), D_MODEL ** -0.5),
        "conv_w": nrm(ks[20], (DEPTH, CONV_W, D_FF), CONV_W ** -0.5),
        "conv_b": nrm(ks[21], (DEPTH, D_FF), 0.01),
        "w_down": nrm(ks[22], (DEPTH, D_FF, D_MODEL), D_FF ** -0.5),
        "final_g": gain(ks[23], (D_MODEL,)),
    }


def reference(x_prompt, x_sample, cache_ckv, cache_krope, state_pool, state_conv,
              norm1_g, w_in, pool_w, pool_scale, w_pool_out, q_norm_g, w_uq, kv_norm_g,
              w_uk, w_uv, w_mla_out, w_out, norm2_g, w_up, conv_w, conv_b, w_down, final_g):
    past_len = cache_ckv.shape[2]
    bp = x_prompt.shape[0]
    hp, hs = x_prompt, x_sample
    p_ckv, p_kr, p_pool, p_conv = [], [], [], []
    s_ckv, s_kr, s_pool, s_conv = [], [], [], []
    for l in range(DEPTH):
        lw = (norm1_g[l], w_in[l], pool_w[l], pool_scale[l], w_pool_out[l], q_norm_g[l], w_uq[l],
              kv_norm_g[l], w_uk[l], w_uv[l], w_mla_out[l], w_out[l], norm2_g[l], w_up[l],
              conv_w[l], conv_b[l], w_down[l])
        pool0 = jnp.zeros((bp, POOL_STATE, D_POOL), hp.dtype)
        conv0 = jnp.zeros((bp, CONV_W - 1, D_FF), hp.dtype)
        hp, pp, pc, pk, pcv = trunk_layer(hp, pool0, conv0, None, None, 0, *lw)
        hs, sp, sc, sk, scv = trunk_layer(hs, state_pool[l], state_conv[l], cache_ckv[l], cache_krope[l],
                                          past_len, *lw)
        p_ckv.append(pc); p_kr.append(pk); p_pool.append(pp); p_conv.append(pcv)
        s_ckv.append(sc); s_kr.append(sk); s_pool.append(sp); s_conv.append(scv)
    y_prompt = rms_norm(hp, final_g)
    y_sample = rms_norm(hs, final_g)
    return (y_prompt, y_sample,
            jnp.stack(p_ckv), jnp.stack(p_kr), jnp.stack(p_pool), jnp.stack(p_conv),
            jnp.stack(s_ckv), jnp.stack(s_kr), jnp.stack(s_pool), jnp.stack(s_conv))
```

```python
import functools

import jax
import jax.numpy as jnp
from jax import lax
from jax.experimental import pallas as pl
from jax.experimental.pallas import tpu as pltpu

D_MODEL = 2048
CHUNK = 64
D_POOL = 1024
POOL_WINDOWS = (2, 4, 8, 16)
POOL_GROUP = D_POOL // len(POOL_WINDOWS)
POOL_STATE = max(POOL_WINDOWS) - 1
POOL_HALO = 16
N_HEADS = 16
Q_LORA = 512
KV_LORA = 512
NOPE_DIM = 128
ROPE_DIM = 64
V_DIM = 128
QK_DIM = NOPE_DIM + ROPE_DIM
ATTN_SCALE = QK_DIM ** -0.5
ROPE_BASE = 10000.0
D_FF = 6144
CONV_W = 3
CONV_HALO = 8
EPS = 1e-6
MASK_VALUE = -1e30

COL_Q = D_POOL
COL_KV = COL_Q + Q_LORA
COL_GATE = COL_KV + KV_LORA
COL_ROPE = COL_GATE + 2 * D_MODEL
D_IN_R = COL_ROPE + 2 * ROPE_DIM

V7X_VMEM_LIMIT = 56 * 1024 * 1024

F32 = jnp.float32
BF16 = jnp.bfloat16


def _rms(x, g):
    return x * lax.rsqrt(jnp.mean(x * x, axis=-1, keepdims=True) + EPS) * g


def _dot(a, b):
    return jnp.dot(a, b, preferred_element_type=F32)


def _dot_nt(a, b):
    return lax.dot_general(a, b, (((1,), (1,)), ((), ())), preferred_element_type=F32)


def _resident(shape):
    zeros = (0,) * len(shape)
    return pl.BlockSpec(shape, lambda *_: zeros, pipeline_mode=pl.Buffered(1))


def _params(semantics):
    return pltpu.CompilerParams(dimension_semantics=semantics, vmem_limit_bytes=V7X_VMEM_LIMIT)


def _in_proj_kernel(x_ref, g1_ref, w_ref, qg_ref, kvg_ref, cos_ref, sin_ref,
                    z_ref, cq_ref, ckv_ref, kr_ref, gate_ref):
    u = _rms(x_ref[...], g1_ref[...]).astype(BF16)
    z_ref[...] = _dot(u, w_ref[:, 0:COL_Q])
    cq_ref[...] = _rms(_dot(u, w_ref[:, COL_Q:COL_KV]), qg_ref[...]).astype(BF16)
    ckv_ref[...] = _rms(_dot(u, w_ref[:, COL_KV:COL_GATE]), kvg_ref[...])
    step = 1024
    for lo in range(0, 2 * D_MODEL, step):
        logits = _dot(u, w_ref[:, COL_GATE + lo:COL_GATE + lo + step])
        gate_ref[:, lo:lo + step] = 1.0 / (1.0 + jnp.exp(-logits))
    r = _dot(u, w_ref[:, COL_ROPE:D_IN_R])
    kr_ref[...] = r[:, :ROPE_DIM] * cos_ref[...] + r[:, ROPE_DIM:] * sin_ref[...]


def _in_proj(x, g1, w_in_r, qg, kvg, cos_t, sin_t, tm):
    n = x.shape[0]
    table_blocks = cos_t.shape[0] // tm
    row = lambda i: (i, 0)
    tab = lambda i: (i % table_blocks, 0)
    return pl.pallas_call(
        _in_proj_kernel,
        grid=(n // tm,),
        in_specs=[
            pl.BlockSpec((tm, D_MODEL), row),
            _resident((1, D_MODEL)),
            _resident((D_MODEL, D_IN_R)),
            _resident((1, Q_LORA)),
            _resident((1, KV_LORA)),
            pl.BlockSpec((tm, ROPE_DIM), tab),
            pl.BlockSpec((tm, ROPE_DIM), tab),
        ],
        out_specs=[
            pl.BlockSpec((tm, D_POOL), row),
            pl.BlockSpec((tm, Q_LORA), row),
            pl.BlockSpec((tm, KV_LORA), row),
            pl.BlockSpec((tm, ROPE_DIM), row),
            pl.BlockSpec((tm, 2 * D_MODEL), row),
        ],
        out_shape=[
            jax.ShapeDtypeStruct((n, D_POOL), F32),
            jax.ShapeDtypeStruct((n, Q_LORA), BF16),
            jax.ShapeDtypeStruct((n, KV_LORA), F32),
            jax.ShapeDtypeStruct((n, ROPE_DIM), F32),
            jax.ShapeDtypeStruct((n, 2 * D_MODEL), F32),
        ],
        compiler_params=_params(("arbitrary",)),
        name="in_proj",
    )(x, g1, w_in_r, qg, kvg, cos_t, sin_t)


def _head_proj_kernel(*refs, n_seg, seg_len, with_kv):
    if with_kv:
        cq_ref, cos_ref, sin_ref, wq_ref, ckv_ref, kr_ref, wuk_ref, wuvt_ref, q_ref, k_ref, vt_ref = refs
    else:
        cq_ref, cos_ref, sin_ref, wq_ref, q_ref = refs
    cq = cq_ref[...]
    cos = cos_ref[...]
    sin = sin_ref[...]
    for hh in range(2):
        res = _dot(cq, wq_ref[hh])
        qn = (res[:, :NOPE_DIM] * ATTN_SCALE).astype(BF16)
        qr = ((res[:, NOPE_DIM:QK_DIM] * cos + res[:, QK_DIM:] * sin) * ATTN_SCALE).astype(BF16)
        for s in range(n_seg):
            rows = slice(s * seg_len, (s + 1) * seg_len)
            q_ref[s, hh, :, 0:NOPE_DIM] = qn[rows]
            q_ref[s, hh, :, NOPE_DIM:QK_DIM] = qr[rows]
    if with_kv:
        ckv = ckv_ref[...].astype(BF16)
        krb = kr_ref[...].astype(BF16)
        kn = _dot(ckv, wuk_ref[0]).astype(BF16)
        vt = _dot_nt(wuvt_ref[0], ckv).astype(BF16)
        for hh in range(2):
            k_ref[0, hh, 0, :, 0:NOPE_DIM] = kn[:, hh * NOPE_DIM:(hh + 1) * NOPE_DIM]
            k_ref[0, hh, 0, :, NOPE_DIM:QK_DIM] = krb
            vt_ref[0, hh, 0] = vt[hh * V_DIM:(hh + 1) * V_DIM]


def _head_proj(cq, cos_t, sin_t, wq, batch, seq, tm, kv=None):
    n = cq.shape[0]
    with_kv = kv is not None
    tiles_per_seq = max(seq // tm, 1)
    n_seg = max(tm // seq, 1)
    seg_len = tm // n_seg
    table_blocks = cos_t.shape[0] // tm
    row = lambda i, hp: (i, 0)
    tab = lambda i, hp: (i % table_blocks, 0)
    in_specs = [
        pl.BlockSpec((tm, Q_LORA), row),
        pl.BlockSpec((tm, ROPE_DIM), tab),
        pl.BlockSpec((tm, ROPE_DIM), tab),
        pl.BlockSpec((2, Q_LORA, 2 * NOPE_DIM), lambda i, hp: (hp, 0, 0)),
    ]
    args = [cq, cos_t, sin_t, wq]
    out_specs = [pl.BlockSpec((n_seg, 2, seg_len, QK_DIM),
                              lambda i, hp: (i // tiles_per_seq, hp, i % tiles_per_seq, 0))]
    out_shape = [jax.ShapeDtypeStruct((batch, N_HEADS, seq, QK_DIM), BF16)]
    if with_kv:
        assert n_seg == 1
        ckv, kr, wuk2, wuvt2 = kv
        in_specs += [
            pl.BlockSpec((tm, KV_LORA), row),
            pl.BlockSpec((tm, ROPE_DIM), row),
            pl.BlockSpec((1, KV_LORA, 2 * NOPE_DIM), lambda i, hp: (hp, 0, 0)),
            pl.BlockSpec((1, 2 * V_DIM, KV_LORA), lambda i, hp: (hp, 0, 0)),
        ]
        args += [ckv, kr, wuk2, wuvt2]
        tile = lambda i, hp: (i // tiles_per_seq, hp, i % tiles_per_seq, 0, 0)
        out_specs += [pl.BlockSpec((1, 2, 1, tm, QK_DIM), tile),
                      pl.BlockSpec((1, 2, 1, V_DIM, tm), tile)]
        out_shape += [jax.ShapeDtypeStruct((batch, N_HEADS, tiles_per_seq, tm, QK_DIM), BF16),
                      jax.ShapeDtypeStruct((batch, N_HEADS, tiles_per_seq, V_DIM, tm), BF16)]
    return pl.pallas_call(
        functools.partial(_head_proj_kernel, n_seg=n_seg, seg_len=seg_len, with_kv=with_kv),
        grid=(n // tm, N_HEADS // 2),
        in_specs=in_specs,
        out_specs=out_specs,
        out_shape=out_shape,
        compiler_params=_params(("arbitrary", "arbitrary")),
        name="head_proj_kv" if with_kv else "head_proj_q",
    )(*args)


def _flash_kernel(q_ref, k_ref, vt_ref, o_ref, *, heads, tile):
    qi = pl.program_id(2)
    qs = [q_ref[0, g] for g in range(heads)]
    key_chunk = lax.broadcasted_iota(jnp.int32, (tile, tile), 0) // CHUNK
    query_chunk = lax.broadcasted_iota(jnp.int32, (tile, tile), 1) // CHUNK
    visible = key_chunk <= query_chunk

    def step(j, carry, diagonal):
        new = []
        for g in range(heads):
            m, l, acc = carry[g]
            st = _dot_nt(k_ref[0, g, j], qs[g])
            if diagonal:
                st = jnp.where(visible, st, MASK_VALUE)
            m_new = jnp.maximum(m, jnp.max(st, axis=0, keepdims=True))
            alpha = jnp.exp(m - m_new)
            p = jnp.exp(st - m_new)
            l = alpha * l + jnp.sum(p, axis=0, keepdims=True)
            acc = alpha * acc + _dot(vt_ref[0, g, j], p.astype(BF16))
            new.append((m_new, l, acc))
        return tuple(new)

    init = tuple((jnp.full((1, tile), MASK_VALUE, F32), jnp.zeros((1, tile), F32),
                  jnp.zeros((V_DIM, tile), F32)) for _ in range(heads))
    carry = lax.fori_loop(0, qi, lambda j, c: step(j, c, False), init)
    carry = step(qi, carry, True)
    for g in range(heads):
        _, l, acc = carry[g]
        o_ref[:, g * V_DIM:(g + 1) * V_DIM] = (acc / l).T.astype(BF16)


def _flash(q, k, vt, heads=2):
    batch, _, seq, _ = q.shape
    n_tiles, tile = k.shape[2], k.shape[3]
    return pl.pallas_call(
        functools.partial(_flash_kernel, heads=heads, tile=tile),
        grid=(batch, N_HEADS // heads, n_tiles),
        in_specs=[
            pl.BlockSpec((1, heads, tile, QK_DIM), lambda b, hg, qi: (b, hg, qi, 0)),
            pl.BlockSpec((1, heads, n_tiles, tile, QK_DIM), lambda b, hg, qi: (b, hg, 0, 0, 0)),
            pl.BlockSpec((1, heads, n_tiles, V_DIM, tile), lambda b, hg, qi: (b, hg, 0, 0, 0)),
        ],
        out_specs=pl.BlockSpec((tile, heads * V_DIM), lambda b, hg, qi: (b * n_tiles + qi, hg)),
        out_shape=jax.ShapeDtypeStruct((batch * seq, N_HEADS * V_DIM), BF16),
        compiler_params=_params(("arbitrary", "arbitrary", "arbitrary")),
        name="flash",
    )(q, k, vt)


def _decode_kernel(q_ref, wuk_ref, wuv_ref, cc_ref, ckr_ref, cn_ref, krn_ref, o_ref,
                   qlat, qrope, m_s, l_s, acc_s, *, seq):
    kt = pl.program_id(1)

    @pl.when(kt == 0)
    def _():
        for h in range(N_HEADS):
            rows = slice(h * seq, (h + 1) * seq)
            qlat[rows, :] = _dot_nt(q_ref[0, h, :, 0:NOPE_DIM], wuk_ref[h]).astype(BF16)
            qrope[rows, :] = q_ref[0, h, :, NOPE_DIM:QK_DIM]
        m_s[...] = jnp.full(m_s.shape, MASK_VALUE, F32)
        l_s[...] = jnp.zeros(l_s.shape, F32)
        acc_s[...] = jnp.zeros(acc_s.shape, F32)

    def update(kc, kr):
        s = _dot_nt(qlat[...], kc) + _dot_nt(qrope[...], kr)
        m_prev = m_s[...]
        m_new = jnp.maximum(m_prev, jnp.max(s, axis=-1, keepdims=True))
        alpha = jnp.exp(m_prev - m_new)
        p = jnp.exp(s - m_new)
        l_s[...] = alpha * l_s[...] + jnp.sum(p, axis=-1, keepdims=True)
        acc_s[...] = alpha * acc_s[...] + _dot(p.astype(BF16), kc)
        m_s[...] = m_new

    update(cc_ref[0].astype(BF16), ckr_ref[0].astype(BF16))

    @pl.when(kt == pl.num_programs(1) - 1)
    def _():
        update(cn_ref[...].astype(BF16), krn_ref[...].astype(BF16))
        o_lat = (acc_s[...] / l_s[...]).astype(BF16)
        for h in range(N_HEADS):
            o_h = _dot(o_lat[h * seq:(h + 1) * seq], wuv_ref[h])
            o_ref[:, h * V_DIM:(h + 1) * V_DIM] = o_h.astype(BF16)


def _decode_attn(q, wuk, wuv, cache_ckv, cache_kr, ckv_new, kr_new, key_tile=1024):
    batch, _, seq, _ = q.shape
    past = cache_ckv.shape[1]
    rows = N_HEADS * seq
    return pl.pallas_call(
        functools.partial(_decode_kernel, seq=seq),
        grid=(batch, past // key_tile),
        in_specs=[
            pl.BlockSpec((1, N_HEADS, seq, QK_DIM), lambda b, kt: (b, 0, 0, 0)),
            _resident((N_HEADS, KV_LORA, NOPE_DIM)),
            _resident((N_HEADS, KV_LORA, V_DIM)),
            pl.BlockSpec((1, key_tile, KV_LORA), lambda b, kt: (b, kt, 0)),
            pl.BlockSpec((1, key_tile, ROPE_DIM), lambda b, kt: (b, kt, 0)),
            pl.BlockSpec((seq, KV_LORA), lambda b, kt: (b, 0)),
            pl.BlockSpec((seq, ROPE_DIM), lambda b, kt: (b, 0)),
        ],
        out_specs=pl.BlockSpec((seq, N_HEADS * V_DIM), lambda b, kt: (b, 0)),
        out_shape=jax.ShapeDtypeStruct((batch * seq, N_HEADS * V_DIM), BF16),
        scratch_shapes=[
            pltpu.VMEM((rows, KV_LORA), BF16),
            pltpu.VMEM((rows, ROPE_DIM), BF16),
            pltpu.VMEM((rows, 1), F32),
            pltpu.VMEM((rows, 1), F32),
            pltpu.VMEM((rows, KV_LORA), F32),
        ],
        compiler_params=_params(("arbitrary", "arbitrary")),
        name="decode_attn",
    )(q, wuk, wuv, cache_ckv, cache_kr, ckv_new, kr_new)


def _merge_kernel(z_ref, zprev_ref, pre_ref, att_ref, gate_ref, x_ref, pw_ref, ps_ref,
                  wpo_ref, wmo_ref, wo_ref, h_ref, ext, ypool, *, n_seg, seg_len, tiles_per_seq, pos0):
    t = pl.program_id(0) % tiles_per_seq
    row = lax.broadcasted_iota(jnp.int32, (seg_len, 1), 0)
    pos = pos0 + t * seg_len + row
    for s in range(n_seg):
        if tiles_per_seq == 1:
            halo = pre_ref[s]
        else:
            halo = jnp.where(t == 0, pre_ref[0], zprev_ref[...])
        ext[0:POOL_HALO, :] = halo
        ext[POOL_HALO:, :] = z_ref[s * seg_len:(s + 1) * seg_len, :]
        for g, w in enumerate(POOL_WINDOWS):
            cols = slice(g * POOL_GROUP, (g + 1) * POOL_GROUP)
            cur = ext[POOL_HALO:, cols]
            total = cur
            for d in range(1, w):
                total = total + ext[POOL_HALO - d:POOL_HALO - d + seg_len, cols]
            count = jnp.minimum(pos + 1, w).astype(F32)
            diff = total / count - cur
            y = _dot(diff.astype(BF16), pw_ref[g]) * ps_ref[:, cols]
            ypool[s * seg_len:(s + 1) * seg_len, cols] = y.astype(BF16)
    br_a = _dot(ypool[...], wpo_ref[...])
    br_b = _dot(att_ref[...], wmo_ref[...])
    merged = gate_ref[:, 0:D_MODEL] * br_a + gate_ref[:, D_MODEL:] * br_b
    h_ref[...] = x_ref[...] + _dot(merged.astype(BF16), wo_ref[...])


def _merge(z, prefix, att, gates, x, pool_w, pool_scale, w_pool_out, w_mla_out, w_out, seq, pos0, tm):
    n = x.shape[0]
    tiles_per_seq = max(seq // tm, 1)
    n_seg = max(tm // seq, 1)
    seg_len = tm // n_seg
    halo_blocks = tm // POOL_HALO
    row = lambda i: (i, 0)
    return pl.pallas_call(
        functools.partial(_merge_kernel, n_seg=n_seg, seg_len=seg_len,
                          tiles_per_seq=tiles_per_seq, pos0=pos0),
        grid=(n // tm,),
        in_specs=[
            pl.BlockSpec((tm, D_POOL), row),
            pl.BlockSpec((POOL_HALO, D_POOL), lambda i: (jnp.maximum(i * halo_blocks - 1, 0), 0)),
            pl.BlockSpec((n_seg, POOL_HALO, D_POOL), lambda i: (i // tiles_per_seq, 0, 0)),
            pl.BlockSpec((tm, N_HEADS * V_DIM), row),
            pl.BlockSpec((tm, 2 * D_MODEL), row),
            pl.BlockSpec((tm, D_MODEL), row),
            _resident((len(POOL_WINDOWS), POOL_GROUP, POOL_GROUP)),
            _resident((1, D_POOL)),
            _resident((D_POOL, D_MODEL)),
            _resident((N_HEADS * V_DIM, D_MODEL)),
            _resident((D_MODEL, D_MODEL)),
        ],
        out_specs=pl.BlockSpec((tm, D_MODEL), row),
        out_shape=jax.ShapeDtypeStruct((n, D_MODEL), F32),
        scratch_shapes=[
            pltpu.VMEM((POOL_HALO + seg_len, D_POOL), F32),
            pltpu.VMEM((tm, D_POOL), BF16),
        ],
        compiler_params=_params(("arbitrary",)),
        name="merge",
    )(z, z, prefix, att, gates, x, pool_w, pool_scale, w_pool_out, w_mla_out, w_out)


def _ffn_kernel(h_ref, g2_ref, wa_ref, wb_ref, cw_ref, cb_ref, pre_ref, wd_ref, fg_ref,
                y_ref, tail_ref, hn, ext, gated, carry, *, n_seg, seg_len, tiles_per_seq):
    i = pl.program_id(0)
    j = pl.program_id(1)
    t = i % tiles_per_seq

    @pl.when(j == 0)
    def _():
        hn[...] = _rms(h_ref[...], g2_ref[...]).astype(BF16)

    a = _dot(hn[...], wa_ref[...])
    b = _dot(hn[...], wb_ref[...])
    for s in range(n_seg):
        rows = slice(s * seg_len, (s + 1) * seg_len)
        a_s = a[rows]
        last = a_s[seg_len - (CONV_W - 1):]
        if tiles_per_seq == 1:
            halo = pre_ref[s]
        else:
            @pl.when(t == 0)
            def _():
                carry[j] = pre_ref[0]

            halo = carry[j]
            carry[j] = last
        tail_ref[0, s] = last
        ext[CONV_HALO - (CONV_W - 1):CONV_HALO, :] = halo
        ext[CONV_HALO:, :] = a_s
        c = cb_ref[...] + ext[CONV_HALO - 2:CONV_HALO - 2 + seg_len, :] * cw_ref[0:1, :]
        c = c + ext[CONV_HALO - 1:CONV_HALO - 1 + seg_len, :] * cw_ref[1:2, :]
        c = c + a_s * cw_ref[2:3, :]
        gelu = 0.5 * c * (1.0 + lax.erf(c * (2.0 ** -0.5)))
        gated[rows, :] = (gelu * b[rows]).astype(BF16)
    down = _dot(gated[...], wd_ref[...])

    @pl.when(j == 0)
    def _():
        y_ref[...] = down

    @pl.when(j > 0)
    def _():
        y_ref[...] += down

    @pl.when(j == pl.num_programs(1) - 1)
    def _():
        y_ref[...] = _rms(h_ref[...] + y_ref[...], fg_ref[...])


def _ffn(h, norm2_g, w_up, conv_w, conv_b, prefix, w_down, final_g, seq, tm, tf=512):
    n = h.shape[0]
    tiles_per_seq = max(seq // tm, 1)
    n_seg = max(tm // seq, 1)
    seg_len = tm // n_seg
    nff = D_FF // tf
    return pl.pallas_call(
        functools.partial(_ffn_kernel, n_seg=n_seg, seg_len=seg_len, tiles_per_seq=tiles_per_seq),
        grid=(n // tm, nff),
        in_specs=[
            pl.BlockSpec((tm, D_MODEL), lambda i, j: (i, 0)),
            _resident((1, D_MODEL)),
            pl.BlockSpec((D_MODEL, tf), lambda i, j: (0, j)),
            pl.BlockSpec((D_MODEL, tf), lambda i, j: (0, nff + j)),
            pl.BlockSpec((CONV_W, tf), lambda i, j: (0, j)),
            pl.BlockSpec((1, tf), lambda i, j: (0, j)),
            pl.BlockSpec((n_seg, CONV_W - 1, tf), lambda i, j: (i // tiles_per_seq, 0, j)),
            pl.BlockSpec((tf, D_MODEL), lambda i, j: (j, 0)),
            _resident((1, D_MODEL)),
        ],
        out_specs=[
            pl.BlockSpec((tm, D_MODEL), lambda i, j: (i, 0)),
            pl.BlockSpec((1, n_seg, CONV_W - 1, tf), lambda i, j: (i, 0, 0, j)),
        ],
        out_shape=[
            jax.ShapeDtypeStruct((n, D_MODEL), F32),
            jax.ShapeDtypeStruct((n // tm, n_seg, CONV_W - 1, D_FF), F32),
        ],
        scratch_shapes=[
            pltpu.VMEM((tm, D_MODEL), BF16),
            pltpu.VMEM((CONV_HALO + seg_len, tf), F32),
            pltpu.VMEM((tm, tf), BF16),
            pltpu.VMEM((nff, CONV_W - 1, tf), F32),
        ],
        compiler_params=_params(("arbitrary", "arbitrary")),
        name="ffn",
    )(h, norm2_g, w_up, w_up, conv_w, conv_b, prefix, w_down, final_g)


def _rope_tables(pos0, length, rows):
    pos = (pos0 + jnp.arange(length)).astype(F32)
    inv = ROPE_BASE ** (-(jnp.arange(ROPE_DIM // 2, dtype=F32) * 2.0 / ROPE_DIM))
    ang = pos[:, None] * inv[None, :]
    cos, sin = jnp.cos(ang), jnp.sin(ang)
    cos_t = jnp.concatenate([cos, cos], axis=-1)
    sin_t = jnp.concatenate([-sin, sin], axis=-1)
    reps = max(rows // length, 1)
    return jnp.tile(cos_t, (reps, 1)), jnp.tile(sin_t, (reps, 1))


def _swap_halves(w):
    return jnp.concatenate([w[..., ROPE_DIM // 2:], w[..., :ROPE_DIM // 2]], axis=-1)


def kernel(x_prompt, x_sample, cache_ckv, cache_krope, state_pool, state_conv, norm1_g, w_in, pool_w, pool_scale, w_pool_out, q_norm_g, w_uq, kv_norm_g, w_uk, w_uv, w_mla_out, w_out, norm2_g, w_up, conv_w, conv_b, w_down, final_g):
    l = 0
    bp, sp, _ = x_prompt.shape
    bs, ss, _ = x_sample.shape
    past = cache_ckv.shape[2]

    w = w_in[l]
    w_rope = w[:, COL_GATE:COL_GATE + ROPE_DIM]
    w_in_r = jnp.concatenate(
        [w[:, :COL_GATE], w[:, COL_GATE + ROPE_DIM:], w_rope, _swap_halves(w_rope)], axis=-1).astype(BF16)
    wq = w_uq[l].reshape(Q_LORA, N_HEADS, QK_DIM)
    wq = jnp.concatenate([wq, _swap_halves(wq[..., NOPE_DIM:])], axis=-1)
    wq = wq.transpose(1, 0, 2).astype(BF16)
    wuk = w_uk[l].astype(BF16)
    wuv = w_uv[l].astype(BF16)
    wuk2 = wuk.reshape(KV_LORA, N_HEADS // 2, 2 * NOPE_DIM).transpose(1, 0, 2)
    wuvt2 = wuv.reshape(KV_LORA, N_HEADS // 2, 2 * V_DIM).transpose(1, 2, 0)
    wuk_h = wuk.transpose(1, 0, 2)
    wuv_h = wuv.transpose(1, 0, 2)
    pw = pool_w[l].astype(BF16)
    wpo = w_pool_out[l].astype(BF16)
    wmo = w_mla_out[l].astype(BF16)
    wo = w_out[l].astype(BF16)
    wup = w_up[l].astype(BF16)
    wdn = w_down[l].astype(BF16)
    g1 = norm1_g[l][None]
    qg = q_norm_g[l][None]
    kvg = kv_norm_g[l][None]
    g2 = norm2_g[l][None]
    fg = final_g[None]
    ps = pool_scale[l][None]
    cb = conv_b[l][None]
    cw = conv_w[l]

    tm_in, tm_head, tm_merge, tm_ffn = 256, 512, 256, 512
    xp = x_prompt.reshape(bp * sp, D_MODEL)
    cos_p, sin_p = _rope_tables(0, sp, sp)
    z_p, cq_p, ckv_p, kr_p, gates_p = _in_proj(xp, g1, w_in_r, qg, kvg, cos_p, sin_p, tm_in)
    q_p, k_p, vt_p = _head_proj(cq_p, cos_p, sin_p, wq, bp, sp, tm_head, kv=(ckv_p, kr_p, wuk2, wuvt2))
    att_p = _flash(q_p, k_p, vt_p)
    pool0 = jnp.zeros((bp, POOL_HALO, D_POOL), F32)
    h_p = _merge(z_p, pool0, att_p, gates_p, xp, pw, ps, wpo, wmo, wo, sp, 0, tm_merge)
    conv0 = jnp.zeros((bp, CONV_W - 1, D_FF), F32)
    y_p, tail_p = _ffn(h_p, g2, wup, cw, cb, conv0, wdn, fg, sp, tm_ffn)

    ns = bs * ss
    xs = x_sample.reshape(ns, D_MODEL)
    cos_s, sin_s = _rope_tables(past, ss, ns)
    z_s, cq_s, ckv_s, kr_s, gates_s = _in_proj(xs, g1, w_in_r, qg, kvg, cos_s, sin_s, 256)
    (q_s,) = _head_proj(cq_s, cos_s, sin_s, wq, bs, ss, ns)
    att_s = _decode_attn(q_s, wuk_h, wuv_h, cache_ckv[l], cache_krope[l], ckv_s, kr_s)
    pool_pre = jnp.pad(state_pool[l], ((0, 0), (POOL_HALO - POOL_STATE, 0), (0, 0)))
    h_s = _merge(z_s, pool_pre, att_s, gates_s, xs, pw, ps, wpo, wmo, wo, ss, past, 256)
    y_s, tail_s = _ffn(h_s, g2, wup, cw, cb, state_conv[l], wdn, fg, ss, ns)

    tiles_per_seq = sp // tm_ffn
    p_conv = tail_p.reshape(bp, tiles_per_seq, CONV_W - 1, D_FF)[:, -1]
    s_conv = tail_s.reshape(bs, CONV_W - 1, D_FF)
    return (
        y_p.reshape(bp, sp, D_MODEL),
        y_s.reshape(bs, ss, D_MODEL),
        ckv_p.reshape(1, bp, sp, KV_LORA),
        kr_p.reshape(1, bp, sp, ROPE_DIM),
        z_p.reshape(bp, sp, D_POOL)[:, sp - POOL_STATE:][None],
        p_conv[None],
        ckv_s.reshape(1, bs, ss, KV_LORA),
        kr_s.reshape(1, bs, ss, ROPE_DIM),
        z_s.reshape(bs, ss, D_POOL)[:, ss - POOL_STATE:][None],
        s_conv[None],
    )
```

```python
import functools

import jax
import jax.numpy as jnp
from jax import lax
from jax.experimental import pallas as pl
from jax.experimental.pallas import tpu as pltpu

D_MODEL = 2048
CHUNK = 64
D_POOL = 1024
POOL_WINDOWS = (2, 4, 8, 16)
POOL_GROUP = D_POOL // len(POOL_WINDOWS)
POOL_STATE = max(POOL_WINDOWS) - 1
POOL_HALO = 16
N_HEADS = 16
Q_LORA = 512
KV_LORA = 512
NOPE_DIM = 128
ROPE_DIM = 64
V_DIM = 128
QK_DIM = NOPE_DIM + ROPE_DIM
V_AUG = V_DIM + 16
ATTN_SCALE = QK_DIM ** -0.5
LOG2E = 1.4426950408889634
Q_SCALE = ATTN_SCALE * LOG2E
ROPE_BASE = 10000.0
D_FF = 6144
CONV_W = 3
CONV_HALO = 8
FFN_ROW_CHUNKS = 2
EPS = 1e-6
MASK_VALUE = -1e30

COL_Q = D_POOL
COL_KV = COL_Q + Q_LORA
COL_GATE = COL_KV + KV_LORA

V7X_VMEM_LIMIT = 56 * 1024 * 1024

F32 = jnp.float32
BF16 = jnp.bfloat16


def _rms(x, g):
    return x * lax.rsqrt(jnp.mean(x * x, axis=-1, keepdims=True) + EPS) * g


def _dot(a, b):
    return jnp.dot(a, b, preferred_element_type=F32)


def _dot_nt(a, b):
    return lax.dot_general(a, b, (((1,), (1,)), ((), ())), preferred_element_type=F32)


def _resident(shape):
    zeros = (0,) * len(shape)
    return pl.BlockSpec(shape, lambda *_: zeros, pipeline_mode=pl.Buffered(1))


def _params(semantics):
    return pltpu.CompilerParams(dimension_semantics=semantics, vmem_limit_bytes=V7X_VMEM_LIMIT)


def _in_proj_kernel(x_ref, g1_ref, w_ref, wg_ref, wr_ref, qg_ref, kvg_ref, cos_ref, sin_ref,
                    z_ref, cq_ref, ckv_ref, kr_ref, gate_ref):
    u = _rms(x_ref[...], g1_ref[...]).astype(BF16)
    z_ref[...] = _dot(u, w_ref[:, 0:COL_Q])
    cq_ref[...] = _rms(_dot(u, w_ref[:, COL_Q:COL_KV]), qg_ref[...]).astype(BF16)
    ckv_ref[...] = _rms(_dot(u, w_ref[:, COL_KV:COL_GATE]), kvg_ref[...])
    r = _dot(u, wr_ref[...])
    kr_ref[...] = r[:, :ROPE_DIM] * cos_ref[...] + r[:, ROPE_DIM:] * sin_ref[...]
    step = 1024
    for lo in range(0, 2 * D_MODEL, step):
        logits = _dot(u, wg_ref[:, lo:lo + step])
        gate_ref[:, lo:lo + step] = 1.0 / (1.0 + jnp.exp(-logits))


def _in_proj(x, g1, w_main, w_gate, w_rope, qg, kvg, cos_t, sin_t, tm):
    n = x.shape[0]
    table_blocks = cos_t.shape[0] // tm
    row = lambda i: (i, 0)
    tab = lambda i: (i % table_blocks, 0)
    return pl.pallas_call(
        _in_proj_kernel,
        grid=(n // tm,),
        in_specs=[
            pl.BlockSpec((tm, D_MODEL), row),
            _resident((1, D_MODEL)),
            _resident((D_MODEL, COL_GATE)),
            _resident((D_MODEL, 2 * D_MODEL)),
            _resident((D_MODEL, 2 * ROPE_DIM)),
            _resident((1, Q_LORA)),
            _resident((1, KV_LORA)),
            pl.BlockSpec((tm, ROPE_DIM), tab),
            pl.BlockSpec((tm, ROPE_DIM), tab),
        ],
        out_specs=[
            pl.BlockSpec((tm, D_POOL), row),
            pl.BlockSpec((tm, Q_LORA), row),
            pl.BlockSpec((tm, KV_LORA), row),
            pl.BlockSpec((tm, ROPE_DIM), row),
            pl.BlockSpec((tm, 2 * D_MODEL), row),
        ],
        out_shape=[
            jax.ShapeDtypeStruct((n, D_POOL), F32),
            jax.ShapeDtypeStruct((n, Q_LORA), BF16),
            jax.ShapeDtypeStruct((n, KV_LORA), F32),
            jax.ShapeDtypeStruct((n, ROPE_DIM), F32),
            jax.ShapeDtypeStruct((n, 2 * D_MODEL), F32),
        ],
        compiler_params=_params(("arbitrary",)),
        name="in_proj",
    )(x, g1, w_main, w_gate, w_rope, qg, kvg, cos_t, sin_t)


def _head_proj_kernel(*refs, n_seg, seg_len, with_kv):
    if with_kv:
        cq_ref, cos_ref, sin_ref, wq_ref, ckv_ref, kr_ref, wuk_ref, wuvt_ref, q_ref, k_ref, vt_ref = refs
    else:
        cq_ref, cos_ref, sin_ref, wq_ref, q_ref = refs
    cq = cq_ref[...]
    cos = cos_ref[...] * Q_SCALE
    sin = sin_ref[...] * Q_SCALE
    if with_kv:
        ckv = ckv_ref[...].astype(BF16)
        krb = kr_ref[...].astype(BF16)
        ones = jnp.ones((V_AUG - V_DIM, ckv.shape[0]), BF16)

    def head_pair(hp, carry):
        for hh in range(2):
            h = 2 * hp + hh
            res = _dot(cq, wq_ref[h])
            qn = (res[:, :NOPE_DIM] * Q_SCALE).astype(BF16)
            qr = (res[:, NOPE_DIM:QK_DIM] * cos + res[:, QK_DIM:] * sin).astype(BF16)
            for s in range(n_seg):
                rows = slice(s * seg_len, (s + 1) * seg_len)
                q_ref[s, h, :, 0:NOPE_DIM] = qn[rows]
                q_ref[s, h, :, NOPE_DIM:QK_DIM] = qr[rows]
        if with_kv:
            kn = _dot(ckv, wuk_ref[hp]).astype(BF16)
            vt = _dot_nt(wuvt_ref[hp], ckv).astype(BF16)
            for hh in range(2):
                h = 2 * hp + hh
                k_ref[0, h, 0, :, 0:NOPE_DIM] = kn[:, hh * NOPE_DIM:(hh + 1) * NOPE_DIM]
                k_ref[0, h, 0, :, NOPE_DIM:QK_DIM] = krb
                vt_ref[0, h, 0, 0:V_DIM, :] = vt[hh * V_DIM:(hh + 1) * V_DIM]
                vt_ref[0, h, 0, V_DIM:V_AUG, :] = ones
        return carry

    lax.fori_loop(0, N_HEADS // 2, head_pair, 0)


def _head_proj(cq, cos_t, sin_t, wq, batch, seq, tm, kv=None):
    n = cq.shape[0]
    with_kv = kv is not None
    tiles_per_seq = max(seq // tm, 1)
    n_seg = max(tm // seq, 1)
    seg_len = tm // n_seg
    table_blocks = cos_t.shape[0] // tm
    row = lambda i: (i, 0)
    tab = lambda i: (i % table_blocks, 0)
    in_specs = [
        pl.BlockSpec((tm, Q_LORA), row),
        pl.BlockSpec((tm, ROPE_DIM), tab),
        pl.BlockSpec((tm, ROPE_DIM), tab),
        _resident((N_HEADS, Q_LORA, 2 * NOPE_DIM)),
    ]
    args = [cq, cos_t, sin_t, wq]
    out_specs = [pl.BlockSpec((n_seg, N_HEADS, seg_len, QK_DIM),
                              lambda i: (i // tiles_per_seq, 0, i % tiles_per_seq, 0))]
    out_shape = [jax.ShapeDtypeStruct((batch, N_HEADS, seq, QK_DIM), BF16)]
    if with_kv:
        assert n_seg == 1
        ckv, kr, wuk2, wuvt2 = kv
        in_specs += [
            pl.BlockSpec((tm, KV_LORA), row),
            pl.BlockSpec((tm, ROPE_DIM), row),
            _resident((N_HEADS // 2, KV_LORA, 2 * NOPE_DIM)),
            _resident((N_HEADS // 2, 2 * V_DIM, KV_LORA)),
        ]
        args += [ckv, kr, wuk2, wuvt2]
        tile = lambda i: (i // tiles_per_seq, 0, i % tiles_per_seq, 0, 0)
        out_specs += [pl.BlockSpec((1, N_HEADS, 1, tm, QK_DIM), tile),
                      pl.BlockSpec((1, N_HEADS, 1, V_AUG, tm), tile)]
        out_shape += [jax.ShapeDtypeStruct((batch, N_HEADS, tiles_per_seq, tm, QK_DIM), BF16),
                      jax.ShapeDtypeStruct((batch, N_HEADS, tiles_per_seq, V_AUG, tm), BF16)]
    return pl.pallas_call(
        functools.partial(_head_proj_kernel, n_seg=n_seg, seg_len=seg_len, with_kv=with_kv),
        grid=(n // tm,),
        in_specs=in_specs,
        out_specs=out_specs,
        out_shape=out_shape,
        compiler_params=_params(("arbitrary",)),
        name="head_proj_kv" if with_kv else "head_proj_q",
    )(*args)


def _flash_kernel(q_ref, k_ref, vt_ref, o_ref, s_ref, acc_ref, m_ref, cmax_ref, *, heads, tile):
    qi = pl.program_id(2)

    def scores(g, j):
        st = _dot_nt(k_ref[0, g, j], q_ref[0, g])
        s_ref[g] = st
        cmax_ref[g] = jnp.max(st, axis=0, keepdims=True)

    def softmax_update(g, diagonal):
        st = s_ref[g]
        if diagonal:
            key_chunk = lax.broadcasted_iota(jnp.int32, (tile, tile), 0) // CHUNK
            query_chunk = lax.broadcasted_iota(jnp.int32, (tile, tile), 1) // CHUNK
            st = jnp.where(key_chunk <= query_chunk, st, MASK_VALUE)
            tile_max = jnp.max(st, axis=0, keepdims=True)
        else:
            tile_max = cmax_ref[g]
        m_old = m_ref[g]
        m_new = jnp.maximum(m_old, tile_max)
        m_ref[g] = m_new
        return jnp.exp2(st - m_new).astype(BF16), jnp.exp2(m_old - m_new)

    for g in range(heads):
        scores(g, 0)
        m_ref[g] = jnp.full((1, tile), MASK_VALUE, F32)
        acc_ref[g] = jnp.zeros((V_AUG, tile), F32)

    def body(j, carry):
        for g in range(heads):
            p, alpha = softmax_update(g, False)
            scores(g, j + 1)
            acc_ref[g] = alpha * acc_ref[g] + _dot(vt_ref[0, g, j], p)
        return carry

    lax.fori_loop(0, qi, body, 0)
    for g in range(heads):
        p, alpha = softmax_update(g, True)
        acc = alpha * acc_ref[g] + _dot(vt_ref[0, g, qi], p)
        o = acc[0:V_DIM] / acc[V_DIM:V_DIM + 1]
        o_ref[:, g * V_DIM:(g + 1) * V_DIM] = o.T.astype(BF16)


def _flash(q, k, vt, heads=4):
    batch, _, seq, _ = q.shape
    n_tiles, tile = k.shape[2], k.shape[3]
    once = pl.Buffered(1)
    return pl.pallas_call(
        functools.partial(_flash_kernel, heads=heads, tile=tile),
        grid=(batch, N_HEADS // heads, n_tiles),
        in_specs=[
            pl.BlockSpec((1, heads, tile, QK_DIM), lambda b, hg, qi: (b, hg, qi, 0)),
            pl.BlockSpec((1, heads, n_tiles, tile, QK_DIM), lambda b, hg, qi: (b, hg, 0, 0, 0),
                         pipeline_mode=once),
            pl.BlockSpec((1, heads, n_tiles, V_AUG, tile), lambda b, hg, qi: (b, hg, 0, 0, 0),
                         pipeline_mode=once),
        ],
        out_specs=pl.BlockSpec((tile, heads * V_DIM), lambda b, hg, qi: (b * n_tiles + qi, hg)),
        out_shape=jax.ShapeDtypeStruct((batch * seq, N_HEADS * V_DIM), BF16),
        scratch_shapes=[
            pltpu.VMEM((heads, tile, tile), F32),
            pltpu.VMEM((heads, V_AUG, tile), F32),
            pltpu.VMEM((heads, 1, tile), F32),
            pltpu.VMEM((heads, 1, tile), F32),
        ],
        compiler_params=_params(("arbitrary", "arbitrary", "arbitrary")),
        name="flash",
    )(q, k, vt)


def _decode_kernel(q_ref, wuk_ref, wuv_ref, cc_ref, ckr_ref, cn_ref, krn_ref, o_ref,
                   qlat, qrope, m_s, l_s, acc_s, *, seq):
    kt = pl.program_id(1)

    @pl.when(kt == 0)
    def _():
        for h in range(N_HEADS):
            rows = slice(h * seq, (h + 1) * seq)
            qlat[rows, :] = _dot_nt(q_ref[0, h, :, 0:NOPE_DIM], wuk_ref[h]).astype(BF16)
            qrope[rows, :] = q_ref[0, h, :, NOPE_DIM:QK_DIM]
        m_s[...] = jnp.full(m_s.shape, MASK_VALUE, F32)
        l_s[...] = jnp.zeros(l_s.shape, F32)
        acc_s[...] = jnp.zeros(acc_s.shape, F32)

    def update(kc, kr):
        s = _dot_nt(qlat[...], kc) + _dot_nt(qrope[...], kr)
        m_prev = m_s[...]
        m_new = jnp.maximum(m_prev, jnp.max(s, axis=-1, keepdims=True))
        alpha = jnp.exp2(m_prev - m_new)
        p = jnp.exp2(s - m_new)
        l_s[...] = alpha * l_s[...] + jnp.sum(p, axis=-1, keepdims=True)
        acc_s[...] = alpha * acc_s[...] + _dot(p.astype(BF16), kc)
        m_s[...] = m_new

    update(cc_ref[0].astype(BF16), ckr_ref[0].astype(BF16))

    @pl.when(kt == pl.num_programs(1) - 1)
    def _():
        update(cn_ref[...].astype(BF16), krn_ref[...].astype(BF16))
        o_lat = (acc_s[...] / l_s[...]).astype(BF16)
        for h in range(N_HEADS):
            o_h = _dot(o_lat[h * seq:(h + 1) * seq], wuv_ref[h])
            o_ref[:, h * V_DIM:(h + 1) * V_DIM] = o_h.astype(BF16)


def _decode_attn(q, wuk, wuv, cache_ckv, cache_kr, ckv_new, kr_new, key_tile=1024):
    batch, _, seq, _ = q.shape
    past = cache_ckv.shape[1]
    rows = N_HEADS * seq
    return pl.pallas_call(
        functools.partial(_decode_kernel, seq=seq),
        grid=(batch, past // key_tile),
        in_specs=[
            pl.BlockSpec((1, N_HEADS, seq, QK_DIM), lambda b, kt: (b, 0, 0, 0)),
            _resident((N_HEADS, KV_LORA, NOPE_DIM)),
            _resident((N_HEADS, KV_LORA, V_DIM)),
            pl.BlockSpec((1, key_tile, KV_LORA), lambda b, kt: (b, kt, 0)),
            pl.BlockSpec((1, key_tile, ROPE_DIM), lambda b, kt: (b, kt, 0)),
            pl.BlockSpec((seq, KV_LORA), lambda b, kt: (b, 0)),
            pl.BlockSpec((seq, ROPE_DIM), lambda b, kt: (b, 0)),
        ],
        out_specs=pl.BlockSpec((seq, N_HEADS * V_DIM), lambda b, kt: (b, 0)),
        out_shape=jax.ShapeDtypeStruct((batch * seq, N_HEADS * V_DIM), BF16),
        scratch_shapes=[
            pltpu.VMEM((rows, KV_LORA), BF16),
            pltpu.VMEM((rows, ROPE_DIM), BF16),
            pltpu.VMEM((rows, 1), F32),
            pltpu.VMEM((rows, 1), F32),
            pltpu.VMEM((rows, KV_LORA), F32),
        ],
        compiler_params=_params(("arbitrary", "arbitrary")),
        name="decode_attn",
    )(q, wuk, wuv, cache_ckv, cache_kr, ckv_new, kr_new)


def _merge_kernel(z_ref, zprev_ref, pre_ref, att_ref, gate_ref, x_ref, pw_ref, ps_ref,
                  wpo_ref, wmo_ref, wo_ref, h_ref, ext, ypool, *, n_seg, seg_len, tiles_per_seq, pos0):
    t = pl.program_id(0) % tiles_per_seq
    row = lax.broadcasted_iota(jnp.int32, (seg_len, 1), 0)
    pos = pos0 + t * seg_len + row
    for s in range(n_seg):
        if tiles_per_seq == 1:
            halo = pre_ref[s]
        else:
            halo = jnp.where(t == 0, pre_ref[0], zprev_ref[...])
        ext[0:POOL_HALO, :] = halo
        ext[POOL_HALO:, :] = z_ref[s * seg_len:(s + 1) * seg_len, :]
        for g, w in enumerate(POOL_WINDOWS):
            cols = slice(g * POOL_GROUP, (g + 1) * POOL_GROUP)
            cur = ext[POOL_HALO:, cols]
            total = cur
            for d in range(1, w):
                total = total + ext[POOL_HALO - d:POOL_HALO - d + seg_len, cols]
            count = jnp.minimum(pos + 1, w).astype(F32)
            diff = total / count - cur
            y = _dot(diff.astype(BF16), pw_ref[g]) * ps_ref[:, cols]
            ypool[s * seg_len:(s + 1) * seg_len, cols] = y.astype(BF16)
    br_a = _dot(ypool[...], wpo_ref[...])
    br_b = _dot(att_ref[...], wmo_ref[...])
    merged = gate_ref[:, 0:D_MODEL] * br_a + gate_ref[:, D_MODEL:] * br_b
    h_ref[...] = x_ref[...] + _dot(merged.astype(BF16), wo_ref[...])


def _merge(z, prefix, att, gates, x, pool_w, pool_scale, w_pool_out, w_mla_out, w_out, seq, pos0, tm):
    n = x.shape[0]
    tiles_per_seq = max(seq // tm, 1)
    n_seg = max(tm // seq, 1)
    seg_len = tm // n_seg
    halo_blocks = tm // POOL_HALO
    row = lambda i: (i, 0)
    return pl.pallas_call(
        functools.partial(_merge_kernel, n_seg=n_seg, seg_len=seg_len,
                          tiles_per_seq=tiles_per_seq, pos0=pos0),
        grid=(n // tm,),
        in_specs=[
            pl.BlockSpec((tm, D_POOL), row),
            pl.BlockSpec((POOL_HALO, D_POOL), lambda i: (jnp.maximum(i * halo_blocks - 1, 0), 0)),
            pl.BlockSpec((n_seg, POOL_HALO, D_POOL), lambda i: (i // tiles_per_seq, 0, 0)),
            pl.BlockSpec((tm, N_HEADS * V_DIM), row),
            pl.BlockSpec((tm, 2 * D_MODEL), row),
            pl.BlockSpec((tm, D_MODEL), row),
            _resident((len(POOL_WINDOWS), POOL_GROUP, POOL_GROUP)),
            _resident((1, D_POOL)),
            _resident((D_POOL, D_MODEL)),
            _resident((N_HEADS * V_DIM, D_MODEL)),
            _resident((D_MODEL, D_MODEL)),
        ],
        out_specs=pl.BlockSpec((tm, D_MODEL), row),
        out_shape=jax.ShapeDtypeStruct((n, D_MODEL), F32),
        scratch_shapes=[
            pltpu.VMEM((POOL_HALO + seg_len, D_POOL), F32),
            pltpu.VMEM((tm, D_POOL), BF16),
        ],
        compiler_params=_params(("arbitrary",)),
        name="merge",
    )(z, z, prefix, att, gates, x, pool_w, pool_scale, w_pool_out, w_mla_out, w_out)


def _ffn_kernel(h_ref, g2_ref, wa_ref, wb_ref, cw_ref, cb_ref, pre_ref, wd_ref, fg_ref,
                y_ref, tail_ref, hn, ext, carry, *, n_seg, seg_len, tiles_per_seq):
    i = pl.program_id(0)
    j = pl.program_id(1)
    t = i % tiles_per_seq

    @pl.when(j == 0)
    def _():
        hn[...] = _rms(h_ref[...], g2_ref[...]).astype(BF16)
        y_ref[...] = jnp.zeros(y_ref.shape, F32)

    if tiles_per_seq > 1:
        @pl.when(t == 0)
        def _():
            carry[j] = pre_ref[0]

    tm = hn.shape[0]
    rc = tm // FFN_ROW_CHUNKS
    up = []
    for c in range(FFN_ROW_CHUNKS):
        hc = hn[c * rc:(c + 1) * rc, :]
        up.append((_dot(hc, wa_ref[...]), _dot(hc, wb_ref[...])))

    def conv_gelu(seg, lo, a_rows, b_rows):
        n = a_rows.shape[0]
        base = CONV_HALO + lo
        ext[seg, base:base + n, :] = a_rows
        c = cb_ref[...] + ext[seg, base - 2:base - 2 + n, :] * cw_ref[0:1, :]
        c = c + ext[seg, base - 1:base - 1 + n, :] * cw_ref[1:2, :]
        c = c + a_rows * cw_ref[2:3, :]
        gelu = 0.5 * c * (1.0 + lax.erf(c * (2.0 ** -0.5)))
        return (gelu * b_rows).astype(BF16)

    for c in range(FFN_ROW_CHUNKS):
        a_c, b_c = up[c]
        if n_seg == 1:
            if c == 0:
                ext[0, CONV_HALO - (CONV_W - 1):CONV_HALO, :] = carry[j] if tiles_per_seq > 1 else pre_ref[0]
            gated_c = conv_gelu(0, c * rc, a_c, b_c)
            if c == FFN_ROW_CHUNKS - 1:
                last = a_c[rc - (CONV_W - 1):]
                tail_ref[0, 0] = last
                if tiles_per_seq > 1:
                    carry[j] = last
        else:
            per_chunk = n_seg // FFN_ROW_CHUNKS
            parts = []
            for k in range(per_chunk):
                s = c * per_chunk + k
                rows = slice(k * seg_len, (k + 1) * seg_len)
                ext[s, CONV_HALO - (CONV_W - 1):CONV_HALO, :] = pre_ref[s]
                parts.append(conv_gelu(s, 0, a_c[rows], b_c[rows]))
                tail_ref[0, s] = a_c[rows][seg_len - (CONV_W - 1):]
            gated_c = jnp.concatenate(parts, axis=0)
        y_ref[c * rc:(c + 1) * rc, :] += _dot(gated_c, wd_ref[...])

    @pl.when(j == pl.num_programs(1) - 1)
    def _():
        y_ref[...] = _rms(h_ref[...] + y_ref[...], fg_ref[...])


def _ffn(h, norm2_g, w_up, conv_w, conv_b, prefix, w_down, final_g, seq, tm, tf=512):
    n = h.shape[0]
    tiles_per_seq = max(seq // tm, 1)
    n_seg = max(tm // seq, 1)
    seg_len = tm // n_seg
    nff = D_FF // tf
    return pl.pallas_call(
        functools.partial(_ffn_kernel, n_seg=n_seg, seg_len=seg_len, tiles_per_seq=tiles_per_seq),
        grid=(n // tm, nff),
        in_specs=[
            pl.BlockSpec((tm, D_MODEL), lambda i, j: (i, 0)),
            _resident((1, D_MODEL)),
            pl.BlockSpec((D_MODEL, tf), lambda i, j: (0, j)),
            pl.BlockSpec((D_MODEL, tf), lambda i, j: (0, nff + j)),
            pl.BlockSpec((CONV_W, tf), lambda i, j: (0, j)),
            pl.BlockSpec((1, tf), lambda i, j: (0, j)),
            pl.BlockSpec((n_seg, CONV_W - 1, tf), lambda i, j: (i // tiles_per_seq, 0, j)),
            pl.BlockSpec((tf, D_MODEL), lambda i, j: (j, 0)),
            _resident((1, D_MODEL)),
        ],
        out_specs=[
            pl.BlockSpec((tm, D_MODEL), lambda i, j: (i, 0)),
            pl.BlockSpec((1, n_seg, CONV_W - 1, tf), lambda i, j: (i, 0, 0, j)),
        ],
        out_shape=[
            jax.ShapeDtypeStruct((n, D_MODEL), F32),
            jax.ShapeDtypeStruct((n // tm, n_seg, CONV_W - 1, D_FF), F32),
        ],
        scratch_shapes=[
            pltpu.VMEM((tm, D_MODEL), BF16),
            pltpu.VMEM((n_seg, CONV_HALO + seg_len, tf), F32),
            pltpu.VMEM((nff, CONV_W - 1, tf), F32),
        ],
        compiler_params=_params(("arbitrary", "arbitrary")),
        name="ffn",
    )(h, norm2_g, w_up, w_up, conv_w, conv_b, prefix, w_down, final_g)


def _rope_tables(pos0, length, rows):
    pos = (pos0 + jnp.arange(length)).astype(F32)
    inv = ROPE_BASE ** (-(jnp.arange(ROPE_DIM // 2, dtype=F32) * 2.0 / ROPE_DIM))
    ang = pos[:, None] * inv[None, :]
    cos, sin = jnp.cos(ang), jnp.sin(ang)
    cos_t = jnp.concatenate([cos, cos], axis=-1)
    sin_t = jnp.concatenate([-sin, sin], axis=-1)
    reps = max(rows // length, 1)
    return jnp.tile(cos_t, (reps, 1)), jnp.tile(sin_t, (reps, 1))


def _swap_halves(w):
    return jnp.concatenate([w[..., ROPE_DIM // 2:], w[..., :ROPE_DIM // 2]], axis=-1)


def kernel(x_prompt, x_sample, cache_ckv, cache_krope, state_pool, state_conv, norm1_g, w_in, pool_w, pool_scale, w_pool_out, q_norm_g, w_uq, kv_norm_g, w_uk, w_uv, w_mla_out, w_out, norm2_g, w_up, conv_w, conv_b, w_down, final_g):
    l = 0
    bp, sp, _ = x_prompt.shape
    bs, ss, _ = x_sample.shape
    past = cache_ckv.shape[2]

    w = w_in[l]
    w_main = w[:, :COL_GATE].astype(BF16)
    w_gate = w[:, COL_GATE + ROPE_DIM:].astype(BF16)
    w_rope = w[:, COL_GATE:COL_GATE + ROPE_DIM]
    w_rope = jnp.concatenate([w_rope, _swap_halves(w_rope)], axis=-1).astype(BF16)
    w_in_parts = (w_main, w_gate, w_rope)
    wq = w_uq[l].reshape(Q_LORA, N_HEADS, QK_DIM)
    wq = jnp.concatenate([wq, _swap_halves(wq[..., NOPE_DIM:])], axis=-1)
    wq = wq.transpose(1, 0, 2).astype(BF16)
    wuk = w_uk[l].astype(BF16)
    wuv = w_uv[l].astype(BF16)
    wuk2 = wuk.reshape(KV_LORA, N_HEADS // 2, 2 * NOPE_DIM).transpose(1, 0, 2)
    wuvt2 = wuv.reshape(KV_LORA, N_HEADS // 2, 2 * V_DIM).transpose(1, 2, 0)
    wuk_h = wuk.transpose(1, 0, 2)
    wuv_h = wuv.transpose(1, 0, 2)
    pw = pool_w[l].astype(BF16)
    wpo = w_pool_out[l].astype(BF16)
    wmo = w_mla_out[l].astype(BF16)
    wo = w_out[l].astype(BF16)
    wup = w_up[l].astype(BF16)
    wdn = w_down[l].astype(BF16)
    g1 = norm1_g[l][None]
    qg = q_norm_g[l][None]
    kvg = kv_norm_g[l][None]
    g2 = norm2_g[l][None]
    fg = final_g[None]
    ps = pool_scale[l][None]
    cb = conv_b[l][None]
    cw = conv_w[l]

    tm_in, tm_head, tm_merge, tm_ffn = 256, 512, 256, 512
    xp = x_prompt.reshape(bp * sp, D_MODEL)
    cos_p, sin_p = _rope_tables(0, sp, sp)
    z_p, cq_p, ckv_p, kr_p, gates_p = _in_proj(xp, g1, *w_in_parts, qg, kvg, cos_p, sin_p, tm_in)
    q_p, k_p, vt_p = _head_proj(cq_p, cos_p, sin_p, wq, bp, sp, tm_head, kv=(ckv_p, kr_p, wuk2, wuvt2))
    att_p = _flash(q_p, k_p, vt_p)
    pool0 = jnp.zeros((bp, POOL_HALO, D_POOL), F32)
    h_p = _merge(z_p, pool0, att_p, gates_p, xp, pw, ps, wpo, wmo, wo, sp, 0, tm_merge)
    conv0 = jnp.zeros((bp, CONV_W - 1, D_FF), F32)
    y_p, tail_p = _ffn(h_p, g2, wup, cw, cb, conv0, wdn, fg, sp, tm_ffn)

    ns = bs * ss
    xs = x_sample.reshape(ns, D_MODEL)
    cos_s, sin_s = _rope_tables(past, ss, ns)
    z_s, cq_s, ckv_s, kr_s, gates_s = _in_proj(xs, g1, *w_in_parts, qg, kvg, cos_s, sin_s, 256)
    (q_s,) = _head_proj(cq_s, cos_s, sin_s, wq, bs, ss, ns)
    att_s = _decode_attn(q_s, wuk_h, wuv_h, cache_ckv[l], cache_krope[l], ckv_s, kr_s)
    pool_pre = jnp.pad(state_pool[l], ((0, 0), (POOL_HALO - POOL_STATE, 0), (0, 0)))
    h_s = _merge(z_s, pool_pre, att_s, gates_s, xs, pw, ps, wpo, wmo, wo, ss, past, 256)
    y_s, tail_s = _ffn(h_s, g2, wup, cw, cb, state_conv[l], wdn, fg, ss, ns)

    tiles_per_seq = sp // tm_ffn
    p_conv = tail_p.reshape(bp, tiles_per_seq, CONV_W - 1, D_FF)[:, -1]
    s_conv = tail_s.reshape(bs, CONV_W - 1, D_FF)
    return (
        y_p.reshape(bp, sp, D_MODEL),
        y_s.reshape(bs, ss, D_MODEL),
        ckv_p.reshape(1, bp, sp, KV_LORA),
        kr_p.reshape(1, bp, sp, ROPE_DIM),
        z_p.reshape(bp, sp, D_POOL)[:, sp - POOL_STATE:][None],
        p_conv[None],
        ckv_s.reshape(1, bs, ss, KV_LORA),
        kr_s.reshape(1, bs, ss, ROPE_DIM),
        z_s.reshape(bs, ss, D_POOL)[:, ss - POOL_STATE:][None],
        s_conv[None],
    )
```

```python
import functools

import jax
import jax.numpy as jnp
from jax import lax
from jax.experimental import pallas as pl
from jax.experimental.pallas import tpu as pltpu

D_MODEL = 2048
CHUNK = 64
D_POOL = 1024
POOL_WINDOWS = (2, 4, 8, 16)
POOL_GROUP = D_POOL // len(POOL_WINDOWS)
POOL_STATE = max(POOL_WINDOWS) - 1
POOL_HALO = 16
N_HEADS = 16
Q_LORA = 512
KV_LORA = 512
NOPE_DIM = 128
ROPE_DIM = 64
V_DIM = 128
QK_DIM = NOPE_DIM + ROPE_DIM
V_AUG = V_DIM + 16
ATTN_SCALE = QK_DIM ** -0.5
LOG2E = 1.4426950408889634
Q_SCALE = ATTN_SCALE * LOG2E
ROPE_BASE = 10000.0
D_FF = 6144
CONV_W = 3
CONV_HALO = 8
FFN_ROW_CHUNKS = 2
EPS = 1e-6
MASK_VALUE = -1e30

COL_Q = D_POOL
COL_KV = COL_Q + Q_LORA
COL_GATE = COL_KV + KV_LORA

V7X_VMEM_LIMIT = 56 * 1024 * 1024

F32 = jnp.float32
BF16 = jnp.bfloat16


def _rms(x, g):
    return x * lax.rsqrt(jnp.mean(x * x, axis=-1, keepdims=True) + EPS) * g


def _dot(a, b):
    return jnp.dot(a, b, preferred_element_type=F32)


def _dot_nt(a, b):
    return lax.dot_general(a, b, (((1,), (1,)), ((), ())), preferred_element_type=F32)


def _resident(shape):
    zeros = (0,) * len(shape)
    return pl.BlockSpec(shape, lambda *_: zeros, pipeline_mode=pl.Buffered(1))


def _params(semantics):
    return pltpu.CompilerParams(dimension_semantics=semantics, vmem_limit_bytes=V7X_VMEM_LIMIT)


def _in_proj_kernel(x_ref, g1_ref, w_ref, wg_ref, wr_ref, qg_ref, kvg_ref, cos_ref, sin_ref,
                    z_ref, cq_ref, ckv_ref, kr_ref, gate_ref):
    u = _rms(x_ref[...], g1_ref[...]).astype(BF16)
    z_ref[...] = _dot(u, w_ref[:, 0:COL_Q])
    cq_ref[...] = _rms(_dot(u, w_ref[:, COL_Q:COL_KV]), qg_ref[...]).astype(BF16)
    ckv_ref[...] = _rms(_dot(u, w_ref[:, COL_KV:COL_GATE]), kvg_ref[...])
    r = _dot(u, wr_ref[...])
    kr_ref[...] = r[:, :ROPE_DIM] * cos_ref[...] + r[:, ROPE_DIM:] * sin_ref[...]
    step = 1024
    for lo in range(0, 2 * D_MODEL, step):
        logits = _dot(u, wg_ref[:, lo:lo + step])
        gate_ref[:, lo:lo + step] = 1.0 / (1.0 + jnp.exp(-logits))


def _in_proj(x, g1, w_main, w_gate, w_rope, qg, kvg, cos_t, sin_t, tm):
    n = x.shape[0]
    table_blocks = cos_t.shape[0] // tm
    row = lambda i: (i, 0)
    tab = lambda i: (i % table_blocks, 0)
    return pl.pallas_call(
        _in_proj_kernel,
        grid=(n // tm,),
        in_specs=[
            pl.BlockSpec((tm, D_MODEL), row),
            _resident((1, D_MODEL)),
            _resident((D_MODEL, COL_GATE)),
            _resident((D_MODEL, 2 * D_MODEL)),
            _resident((D_MODEL, 2 * ROPE_DIM)),
            _resident((1, Q_LORA)),
            _resident((1, KV_LORA)),
            pl.BlockSpec((tm, ROPE_DIM), tab),
            pl.BlockSpec((tm, ROPE_DIM), tab),
        ],
        out_specs=[
            pl.BlockSpec((tm, D_POOL), row),
            pl.BlockSpec((tm, Q_LORA), row),
            pl.BlockSpec((tm, KV_LORA), row),
            pl.BlockSpec((tm, ROPE_DIM), row),
            pl.BlockSpec((tm, 2 * D_MODEL), row),
        ],
        out_shape=[
            jax.ShapeDtypeStruct((n, D_POOL), F32),
            jax.ShapeDtypeStruct((n, Q_LORA), BF16),
            jax.ShapeDtypeStruct((n, KV_LORA), F32),
            jax.ShapeDtypeStruct((n, ROPE_DIM), F32),
            jax.ShapeDtypeStruct((n, 2 * D_MODEL), F32),
        ],
        compiler_params=_params(("arbitrary",)),
        name="in_proj",
    )(x, g1, w_main, w_gate, w_rope, qg, kvg, cos_t, sin_t)


def _q_proj_kernel(cq_ref, cos_ref, sin_ref, wq_ref, q_ref, *, n_seg, seg_len):
    cq = cq_ref[...]
    cos = cos_ref[...] * Q_SCALE
    sin = sin_ref[...] * Q_SCALE

    def head(h, carry):
        res = _dot(cq, wq_ref[h])
        qn = (res[:, :NOPE_DIM] * Q_SCALE).astype(BF16)
        qr = (res[:, NOPE_DIM:QK_DIM] * cos + res[:, QK_DIM:] * sin).astype(BF16)
        for s in range(n_seg):
            rows = slice(s * seg_len, (s + 1) * seg_len)
            q_ref[s, h, :, 0:NOPE_DIM] = qn[rows]
            q_ref[s, h, :, NOPE_DIM:QK_DIM] = qr[rows]
        return carry

    lax.fori_loop(0, N_HEADS, head, 0)


def _q_proj(cq, cos_t, sin_t, wq, batch, seq, tm):
    n = cq.shape[0]
    n_seg = tm // seq
    table_blocks = cos_t.shape[0] // tm
    return pl.pallas_call(
        functools.partial(_q_proj_kernel, n_seg=n_seg, seg_len=seq),
        grid=(n // tm,),
        in_specs=[
            pl.BlockSpec((tm, Q_LORA), lambda i: (i, 0)),
            pl.BlockSpec((tm, ROPE_DIM), lambda i: (i % table_blocks, 0)),
            pl.BlockSpec((tm, ROPE_DIM), lambda i: (i % table_blocks, 0)),
            _resident((N_HEADS, Q_LORA, 2 * NOPE_DIM)),
        ],
        out_specs=pl.BlockSpec((n_seg, N_HEADS, seq, QK_DIM), lambda i: (i, 0, 0, 0)),
        out_shape=jax.ShapeDtypeStruct((batch, N_HEADS, seq, QK_DIM), BF16),
        compiler_params=_params(("arbitrary",)),
        name="q_proj",
    )(cq, cos_t, sin_t, wq)


def _qkv_proj_kernel(cq_ref, cost_ref, sint_ref, wqt_ref, ckv_ref, kr_ref, wuk_ref, wuvt_ref,
                     qt_ref, k_ref, vt_ref):
    cq = cq_ref[...]
    cos_t = cost_ref[...] * Q_SCALE
    sin_t = sint_ref[...] * Q_SCALE
    ckv = ckv_ref[...].astype(BF16)
    krb = kr_ref[...].astype(BF16)
    ones = jnp.ones((V_AUG - V_DIM, ckv.shape[0]), BF16)

    def head_pair(hp, carry):
        for hh in range(2):
            h = 2 * hp + hh
            res = _dot_nt(wqt_ref[h], cq)
            qt_ref[0, h, 0:NOPE_DIM, :] = (res[0:NOPE_DIM] * Q_SCALE).astype(BF16)
            qt_ref[0, h, NOPE_DIM:QK_DIM, :] = (
                res[NOPE_DIM:QK_DIM] * cos_t + res[QK_DIM:] * sin_t).astype(BF16)
        kn = _dot(ckv, wuk_ref[hp]).astype(BF16)
        vt = _dot_nt(wuvt_ref[hp], ckv).astype(BF16)
        for hh in range(2):
            h = 2 * hp + hh
            k_ref[0, h, 0, :, 0:NOPE_DIM] = kn[:, hh * NOPE_DIM:(hh + 1) * NOPE_DIM]
            k_ref[0, h, 0, :, NOPE_DIM:QK_DIM] = krb
            vt_ref[0, h, 0, 0:V_DIM, :] = vt[hh * V_DIM:(hh + 1) * V_DIM]
            vt_ref[0, h, 0, V_DIM:V_AUG, :] = ones
        return carry

    lax.fori_loop(0, N_HEADS // 2, head_pair, 0)


def _qkv_proj(cq, cos_tt, sin_tt, wqt, ckv, kr, wuk2, wuvt2, batch, seq, tm):
    n = cq.shape[0]
    tiles_per_seq = seq // tm
    row = lambda i: (i, 0)
    tab = lambda i: (0, i % tiles_per_seq)
    tile = lambda i: (i // tiles_per_seq, 0, i % tiles_per_seq, 0, 0)
    return pl.pallas_call(
        _qkv_proj_kernel,
        grid=(n // tm,),
        in_specs=[
            pl.BlockSpec((tm, Q_LORA), row),
            pl.BlockSpec((ROPE_DIM, tm), tab),
            pl.BlockSpec((ROPE_DIM, tm), tab),
            _resident((N_HEADS, 2 * NOPE_DIM, Q_LORA)),
            pl.BlockSpec((tm, KV_LORA), row),
            pl.BlockSpec((tm, ROPE_DIM), row),
            _resident((N_HEADS // 2, KV_LORA, 2 * NOPE_DIM)),
            _resident((N_HEADS // 2, 2 * V_DIM, KV_LORA)),
        ],
        out_specs=[
            pl.BlockSpec((1, N_HEADS, QK_DIM, tm), lambda i: (i // tiles_per_seq, 0, 0, i % tiles_per_seq)),
            pl.BlockSpec((1, N_HEADS, 1, tm, QK_DIM), tile),
            pl.BlockSpec((1, N_HEADS, 1, V_AUG, tm), tile),
        ],
        out_shape=[
            jax.ShapeDtypeStruct((batch, N_HEADS, QK_DIM, seq), BF16),
            jax.ShapeDtypeStruct((batch, N_HEADS, tiles_per_seq, tm, QK_DIM), BF16),
            jax.ShapeDtypeStruct((batch, N_HEADS, tiles_per_seq, V_AUG, tm), BF16),
        ],
        compiler_params=_params(("arbitrary",)),
        name="qkv_proj",
    )(cq, cos_tt, sin_tt, wqt, ckv, kr, wuk2, wuvt2)


def _flash_kernel(qt_ref, k_ref, vt_ref, o_ref, s_ref, acc_ref, m_ref, cmax_ref, bias_ref, *, heads, tile):
    qi = pl.program_id(2)

    @pl.when((pl.program_id(0) == 0) & (pl.program_id(1) == 0) & (qi == 0))
    def _():
        key_chunk = lax.broadcasted_iota(jnp.int32, (tile, tile), 0) // CHUNK
        query_chunk = lax.broadcasted_iota(jnp.int32, (tile, tile), 1) // CHUNK
        bias_ref[...] = jnp.where(key_chunk <= query_chunk, 0.0, MASK_VALUE)

    def scores(g, j, slot):
        st = _dot(k_ref[0, g, j], qt_ref[0, g])
        s_ref[slot, g] = st
        cmax_ref[slot, g] = jnp.max(st, axis=0, keepdims=True)

    def softmax_update(g, slot, diagonal):
        st = s_ref[slot, g]
        if diagonal:
            st = st + bias_ref[...]
            tile_max = jnp.max(st, axis=0, keepdims=True)
        else:
            tile_max = cmax_ref[slot, g]
        m_old = m_ref[g]
        m_new = jnp.maximum(m_old, tile_max)
        m_ref[g] = m_new
        return jnp.exp2(st - m_new).astype(BF16), jnp.exp2(m_old - m_new)

    def values(g, j, p, alpha):
        acc_ref[g] = alpha * acc_ref[g] + _dot(vt_ref[0, g, j], p)

    def steps(j0, count):
        pending = None
        for t in range(count):
            for g in range(heads):
                p, alpha = softmax_update(g, t % 2, False)
                scores(g, j0 + t + 1, (t + 1) % 2)
                if pending is not None:
                    values(*pending)
                pending = (g, j0 + t, p, alpha)
        values(*pending)

    for g in range(heads):
        scores(g, 0, 0)
        m_ref[g] = jnp.full((1, tile), MASK_VALUE, F32)
        acc_ref[g] = jnp.zeros((V_AUG, tile), F32)

    def two_steps(jj, carry):
        steps(2 * jj, 2)
        return carry

    lax.fori_loop(0, qi // 2, two_steps, 0)

    @pl.when(qi % 2 == 1)
    def _():
        steps(qi - 1, 1)

    diag_slot = qi % 2
    for g in range(heads):
        p, alpha = softmax_update(g, diag_slot, True)
        acc = alpha * acc_ref[g] + _dot(vt_ref[0, g, qi], p)
        o = acc[0:V_DIM] / acc[V_DIM:V_DIM + 1]
        o_ref[:, g * V_DIM:(g + 1) * V_DIM] = o.T.astype(BF16)


def _flash(qt, k, vt, heads=4):
    batch, _, _, seq = qt.shape
    n_tiles, tile = k.shape[2], k.shape[3]
    once = pl.Buffered(1)
    return pl.pallas_call(
        functools.partial(_flash_kernel, heads=heads, tile=tile),
        grid=(batch, N_HEADS // heads, n_tiles),
        in_specs=[
            pl.BlockSpec((1, heads, QK_DIM, tile), lambda b, hg, qi: (b, hg, 0, qi)),
            pl.BlockSpec((1, heads, n_tiles, tile, QK_DIM), lambda b, hg, qi: (b, hg, 0, 0, 0),
                         pipeline_mode=once),
            pl.BlockSpec((1, heads, n_tiles, V_AUG, tile), lambda b, hg, qi: (b, hg, 0, 0, 0),
                         pipeline_mode=once),
        ],
        out_specs=pl.BlockSpec((tile, heads * V_DIM), lambda b, hg, qi: (b * n_tiles + qi, hg)),
        out_shape=jax.ShapeDtypeStruct((batch * seq, N_HEADS * V_DIM), BF16),
        scratch_shapes=[
            pltpu.VMEM((2, heads, tile, tile), F32),
            pltpu.VMEM((heads, V_AUG, tile), F32),
            pltpu.VMEM((heads, 1, tile), F32),
            pltpu.VMEM((2, heads, 1, tile), F32),
            pltpu.VMEM((tile, tile), F32),
        ],
        compiler_params=_params(("arbitrary", "arbitrary", "arbitrary")),
        name="flash",
    )(qt, k, vt)


def _decode_kernel(q_ref, wuk_ref, wuv_ref, cc_ref, ckr_ref, cn_ref, krn_ref, o_ref,
                   qlat, qrope, m_s, l_s, acc_s, *, seq):
    kt = pl.program_id(1)

    @pl.when(kt == 0)
    def _():
        for h in range(N_HEADS):
            rows = slice(h * seq, (h + 1) * seq)
            qlat[rows, :] = _dot_nt(q_ref[0, h, :, 0:NOPE_DIM], wuk_ref[h]).astype(BF16)
            qrope[rows, :] = q_ref[0, h, :, NOPE_DIM:QK_DIM]
        m_s[...] = jnp.full(m_s.shape, MASK_VALUE, F32)
        l_s[...] = jnp.zeros(l_s.shape, F32)
        acc_s[...] = jnp.zeros(acc_s.shape, F32)

    def update(kc, kr):
        s = _dot_nt(qlat[...], kc) + _dot_nt(qrope[...], kr)
        m_prev = m_s[...]
        m_new = jnp.maximum(m_prev, jnp.max(s, axis=-1, keepdims=True))
        alpha = jnp.exp2(m_prev - m_new)
        p = jnp.exp2(s - m_new)
        l_s[...] = alpha * l_s[...] + jnp.sum(p, axis=-1, keepdims=True)
        acc_s[...] = alpha * acc_s[...] + _dot(p.astype(BF16), kc)
        m_s[...] = m_new

    update(cc_ref[0].astype(BF16), ckr_ref[0].astype(BF16))

    @pl.when(kt == pl.num_programs(1) - 1)
    def _():
        update(cn_ref[...].astype(BF16), krn_ref[...].astype(BF16))
        o_lat = (acc_s[...] / l_s[...]).astype(BF16)
        for h in range(N_HEADS):
            o_h = _dot(o_lat[h * seq:(h + 1) * seq], wuv_ref[h])
            o_ref[:, h * V_DIM:(h + 1) * V_DIM] = o_h.astype(BF16)


def _decode_attn(q, wuk, wuv, cache_ckv, cache_kr, ckv_new, kr_new, key_tile=1024):
    batch, _, seq, _ = q.shape
    past = cache_ckv.shape[1]
    rows = N_HEADS * seq
    return pl.pallas_call(
        functools.partial(_decode_kernel, seq=seq),
        grid=(batch, past // key_tile),
        in_specs=[
            pl.BlockSpec((1, N_HEADS, seq, QK_DIM), lambda b, kt: (b, 0, 0, 0)),
            _resident((N_HEADS, KV_LORA, NOPE_DIM)),
            _resident((N_HEADS, KV_LORA, V_DIM)),
            pl.BlockSpec((1, key_tile, KV_LORA), lambda b, kt: (b, kt, 0)),
            pl.BlockSpec((1, key_tile, ROPE_DIM), lambda b, kt: (b, kt, 0)),
            pl.BlockSpec((seq, KV_LORA), lambda b, kt: (b, 0)),
            pl.BlockSpec((seq, ROPE_DIM), lambda b, kt: (b, 0)),
        ],
        out_specs=pl.BlockSpec((seq, N_HEADS * V_DIM), lambda b, kt: (b, 0)),
        out_shape=jax.ShapeDtypeStruct((batch * seq, N_HEADS * V_DIM), BF16),
        scratch_shapes=[
            pltpu.VMEM((rows, KV_LORA), BF16),
            pltpu.VMEM((rows, ROPE_DIM), BF16),
            pltpu.VMEM((rows, 1), F32),
            pltpu.VMEM((rows, 1), F32),
            pltpu.VMEM((rows, KV_LORA), F32),
        ],
        compiler_params=_params(("arbitrary", "arbitrary")),
        name="decode_attn",
    )(q, wuk, wuv, cache_ckv, cache_kr, ckv_new, kr_new)


def _merge_kernel(z_ref, zprev_ref, pre_ref, att_ref, gate_ref, x_ref, pw_ref, ps_ref,
                  wpo_ref, wmo_ref, wo_ref, h_ref, ext, ypool, *, n_seg, seg_len, tiles_per_seq, pos0):
    t = pl.program_id(0) % tiles_per_seq
    row = lax.broadcasted_iota(jnp.int32, (seg_len, 1), 0)
    pos = pos0 + t * seg_len + row
    for s in range(n_seg):
        if tiles_per_seq == 1:
            halo = pre_ref[s]
        else:
            halo = jnp.where(t == 0, pre_ref[0], zprev_ref[...])
        ext[0:POOL_HALO, :] = halo
        ext[POOL_HALO:, :] = z_ref[s * seg_len:(s + 1) * seg_len, :]
        for g, w in enumerate(POOL_WINDOWS):
            cols = slice(g * POOL_GROUP, (g + 1) * POOL_GROUP)
            cur = ext[POOL_HALO:, cols]
            total = cur
            for d in range(1, w):
                total = total + ext[POOL_HALO - d:POOL_HALO - d + seg_len, cols]
            count = jnp.minimum(pos + 1, w).astype(F32)
            diff = total / count - cur
            y = _dot(diff.astype(BF16), pw_ref[g]) * ps_ref[:, cols]
            ypool[s * seg_len:(s + 1) * seg_len, cols] = y.astype(BF16)
    br_a = _dot(ypool[...], wpo_ref[...])
    br_b = _dot(att_ref[...], wmo_ref[...])
    merged = gate_ref[:, 0:D_MODEL] * br_a + gate_ref[:, D_MODEL:] * br_b
    h_ref[...] = x_ref[...] + _dot(merged.astype(BF16), wo_ref[...])


def _merge(z, prefix, att, gates, x, pool_w, pool_scale, w_pool_out, w_mla_out, w_out, seq, pos0, tm):
    n = x.shape[0]
    tiles_per_seq = max(seq // tm, 1)
    n_seg = max(tm // seq, 1)
    seg_len = tm // n_seg
    halo_blocks = tm // POOL_HALO
    row = lambda i: (i, 0)
    return pl.pallas_call(
        functools.partial(_merge_kernel, n_seg=n_seg, seg_len=seg_len,
                          tiles_per_seq=tiles_per_seq, pos0=pos0),
        grid=(n // tm,),
        in_specs=[
            pl.BlockSpec((tm, D_POOL), row),
            pl.BlockSpec((POOL_HALO, D_POOL), lambda i: (jnp.maximum(i * halo_blocks - 1, 0), 0)),
            pl.BlockSpec((n_seg, POOL_HALO, D_POOL), lambda i: (i // tiles_per_seq, 0, 0)),
            pl.BlockSpec((tm, N_HEADS * V_DIM), row),
            pl.BlockSpec((tm, 2 * D_MODEL), row),
            pl.BlockSpec((tm, D_MODEL), row),
            _resident((len(POOL_WINDOWS), POOL_GROUP, POOL_GROUP)),
            _resident((1, D_POOL)),
            _resident((D_POOL, D_MODEL)),
            _resident((N_HEADS * V_DIM, D_MODEL)),
            _resident((D_MODEL, D_MODEL)),
        ],
        out_specs=pl.BlockSpec((tm, D_MODEL), row),
        out_shape=jax.ShapeDtypeStruct((n, D_MODEL), F32),
        scratch_shapes=[
            pltpu.VMEM((POOL_HALO + seg_len, D_POOL), F32),
            pltpu.VMEM((tm, D_POOL), BF16),
        ],
        compiler_params=_params(("arbitrary",)),
        name="merge",
    )(z, z, prefix, att, gates, x, pool_w, pool_scale, w_pool_out, w_mla_out, w_out)


def _ffn_kernel(h_ref, g2_ref, wa_ref, wb_ref, cw_ref, cb_ref, pre_ref, wd_ref, fg_ref,
                y_ref, tail_ref, hn, ext, carry, *, n_seg, seg_len, tiles_per_seq):
    i = pl.program_id(0)
    j = pl.program_id(1)
    t = i % tiles_per_seq

    @pl.when(j == 0)
    def _():
        hn[...] = _rms(h_ref[...], g2_ref[...]).astype(BF16)
        y_ref[...] = jnp.zeros(y_ref.shape, F32)

    if tiles_per_seq > 1:
        @pl.when(t == 0)
        def _():
            carry[j] = pre_ref[0]

    tm = hn.shape[0]
    rc = tm // FFN_ROW_CHUNKS
    up = []
    for c in range(FFN_ROW_CHUNKS):
        hc = hn[c * rc:(c + 1) * rc, :]
        up.append((_dot(hc, wa_ref[...]), _dot(hc, wb_ref[...])))

    def conv_gelu(seg, lo, a_rows, b_rows):
        n = a_rows.shape[0]
        base = CONV_HALO + lo
        ext[seg, base:base + n, :] = a_rows
        c = cb_ref[...] + ext[seg, base - 2:base - 2 + n, :] * cw_ref[0:1, :]
        c = c + ext[seg, base - 1:base - 1 + n, :] * cw_ref[1:2, :]
        c = c + a_rows * cw_ref[2:3, :]
        gelu = 0.5 * c * (1.0 + lax.erf(c * (2.0 ** -0.5)))
        return (gelu * b_rows).astype(BF16)

    for c in range(FFN_ROW_CHUNKS):
        a_c, b_c = up[c]
        if n_seg == 1:
            if c == 0:
                ext[0, CONV_HALO - (CONV_W - 1):CONV_HALO, :] = carry[j] if tiles_per_seq > 1 else pre_ref[0]
            gated_c = conv_gelu(0, c * rc, a_c, b_c)
            if c == FFN_ROW_CHUNKS - 1:
                last = a_c[rc - (CONV_W - 1):]
                tail_ref[0, 0] = last
                if tiles_per_seq > 1:
                    carry[j] = last
        else:
            per_chunk = n_seg // FFN_ROW_CHUNKS
            parts = []
            for k in range(per_chunk):
                s = c * per_chunk + k
                rows = slice(k * seg_len, (k + 1) * seg_len)
                ext[s, CONV_HALO - (CONV_W - 1):CONV_HALO, :] = pre_ref[s]
                parts.append(conv_gelu(s, 0, a_c[rows], b_c[rows]))
                tail_ref[0, s] = a_c[rows][seg_len - (CONV_W - 1):]
            gated_c = jnp.concatenate(parts, axis=0)
        y_ref[c * rc:(c + 1) * rc, :] += _dot(gated_c, wd_ref[...])

    @pl.when(j == pl.num_programs(1) - 1)
    def _():
        y_ref[...] = _rms(h_ref[...] + y_ref[...], fg_ref[...])


def _ffn(h, norm2_g, w_up, conv_w, conv_b, prefix, w_down, final_g, seq, tm, tf=512):
    n = h.shape[0]
    tiles_per_seq = max(seq // tm, 1)
    n_seg = max(tm // seq, 1)
    seg_len = tm // n_seg
    nff = D_FF // tf
    return pl.pallas_call(
        functools.partial(_ffn_kernel, n_seg=n_seg, seg_len=seg_len, tiles_per_seq=tiles_per_seq),
        grid=(n // tm, nff),
        in_specs=[
            pl.BlockSpec((tm, D_MODEL), lambda i, j: (i, 0)),
            _resident((1, D_MODEL)),
            pl.BlockSpec((D_MODEL, tf), lambda i, j: (0, j)),
            pl.BlockSpec((D_MODEL, tf), lambda i, j: (0, nff + j)),
            pl.BlockSpec((CONV_W, tf), lambda i, j: (0, j)),
            pl.BlockSpec((1, tf), lambda i, j: (0, j)),
            pl.BlockSpec((n_seg, CONV_W - 1, tf), lambda i, j: (i // tiles_per_seq, 0, j)),
            pl.BlockSpec((tf, D_MODEL), lambda i, j: (j, 0)),
            _resident((1, D_MODEL)),
        ],
        out_specs=[
            pl.BlockSpec((tm, D_MODEL), lambda i, j: (i, 0)),
            pl.BlockSpec((1, n_seg, CONV_W - 1, tf), lambda i, j: (i, 0, 0, j)),
        ],
        out_shape=[
            jax.ShapeDtypeStruct((n, D_MODEL), F32),
            jax.ShapeDtypeStruct((n // tm, n_seg, CONV_W - 1, D_FF), F32),
        ],
        scratch_shapes=[
            pltpu.VMEM((tm, D_MODEL), BF16),
            pltpu.VMEM((n_seg, CONV_HALO + seg_len, tf), F32),
            pltpu.VMEM((nff, CONV_W - 1, tf), F32),
        ],
        compiler_params=_params(("arbitrary", "arbitrary")),
        name="ffn",
    )(h, norm2_g, w_up, w_up, conv_w, conv_b, prefix, w_down, final_g)


def _rope_tables(pos0, length, rows):
    pos = (pos0 + jnp.arange(length)).astype(F32)
    inv = ROPE_BASE ** (-(jnp.arange(ROPE_DIM // 2, dtype=F32) * 2.0 / ROPE_DIM))
    ang = pos[:, None] * inv[None, :]
    cos, sin = jnp.cos(ang), jnp.sin(ang)
    cos_t = jnp.concatenate([cos, cos], axis=-1)
    sin_t = jnp.concatenate([-sin, sin], axis=-1)
    reps = max(rows // length, 1)
    return jnp.tile(cos_t, (reps, 1)), jnp.tile(sin_t, (reps, 1))


def _swap_halves(w):
    return jnp.concatenate([w[..., ROPE_DIM // 2:], w[..., :ROPE_DIM // 2]], axis=-1)


def kernel(x_prompt, x_sample, cache_ckv, cache_krope, state_pool, state_conv, norm1_g, w_in, pool_w, pool_scale, w_pool_out, q_norm_g, w_uq, kv_norm_g, w_uk, w_uv, w_mla_out, w_out, norm2_g, w_up, conv_w, conv_b, w_down, final_g):
    l = 0
    bp, sp, _ = x_prompt.shape
    bs, ss, _ = x_sample.shape
    past = cache_ckv.shape[2]

    w = w_in[l]
    w_main = w[:, :COL_GATE].astype(BF16)
    w_gate = w[:, COL_GATE + ROPE_DIM:].astype(BF16)
    w_rope = w[:, COL_GATE:COL_GATE + ROPE_DIM]
    w_rope = jnp.concatenate([w_rope, _swap_halves(w_rope)], axis=-1).astype(BF16)
    w_in_parts = (w_main, w_gate, w_rope)
    wq = w_uq[l].reshape(Q_LORA, N_HEADS, QK_DIM)
    wq = jnp.concatenate([wq, _swap_halves(wq[..., NOPE_DIM:])], axis=-1)
    wqt = wq.transpose(1, 2, 0).astype(BF16)
    wq = wq.transpose(1, 0, 2).astype(BF16)
    wuk = w_uk[l].astype(BF16)
    wuv = w_uv[l].astype(BF16)
    wuk2 = wuk.reshape(KV_LORA, N_HEADS // 2, 2 * NOPE_DIM).transpose(1, 0, 2)
    wuvt2 = wuv.reshape(KV_LORA, N_HEADS // 2, 2 * V_DIM).transpose(1, 2, 0)
    wuk_h = wuk.transpose(1, 0, 2)
    wuv_h = wuv.transpose(1, 0, 2)
    pw = pool_w[l].astype(BF16)
    wpo = w_pool_out[l].astype(BF16)
    wmo = w_mla_out[l].astype(BF16)
    wo = w_out[l].astype(BF16)
    wup = w_up[l].astype(BF16)
    wdn = w_down[l].astype(BF16)
    g1 = norm1_g[l][None]
    qg = q_norm_g[l][None]
    kvg = kv_norm_g[l][None]
    g2 = norm2_g[l][None]
    fg = final_g[None]
    ps = pool_scale[l][None]
    cb = conv_b[l][None]
    cw = conv_w[l]

    tm_in, tm_head, tm_merge, tm_ffn = 256, 512, 256, 512
    xp = x_prompt.reshape(bp * sp, D_MODEL)
    cos_p, sin_p = _rope_tables(0, sp, sp)
    z_p, cq_p, ckv_p, kr_p, gates_p = _in_proj(xp, g1, *w_in_parts, qg, kvg, cos_p, sin_p, tm_in)
    qt_p, k_p, vt_p = _qkv_proj(cq_p, cos_p.T, sin_p.T, wqt, ckv_p, kr_p, wuk2, wuvt2, bp, sp, tm_head)
    att_p = _flash(qt_p, k_p, vt_p)
    pool0 = jnp.zeros((bp, POOL_HALO, D_POOL), F32)
    h_p = _merge(z_p, pool0, att_p, gates_p, xp, pw, ps, wpo, wmo, wo, sp, 0, tm_merge)
    conv0 = jnp.zeros((bp, CONV_W - 1, D_FF), F32)
    y_p, tail_p = _ffn(h_p, g2, wup, cw, cb, conv0, wdn, fg, sp, tm_ffn)

    ns = bs * ss
    xs = x_sample.reshape(ns, D_MODEL)
    cos_s, sin_s = _rope_tables(past, ss, ns)
    z_s, cq_s, ckv_s, kr_s, gates_s = _in_proj(xs, g1, *w_in_parts, qg, kvg, cos_s, sin_s, 256)
    q_s = _q_proj(cq_s, cos_s, sin_s, wq, bs, ss, ns)
    att_s = _decode_attn(q_s, wuk_h, wuv_h, cache_ckv[l], cache_krope[l], ckv_s, kr_s)
    pool_pre = jnp.pad(state_pool[l], ((0, 0), (POOL_HALO - POOL_STATE, 0), (0, 0)))
    h_s = _merge(z_s, pool_pre, att_s, gates_s, xs, pw, ps, wpo, wmo, wo, ss, past, 256)
    y_s, tail_s = _ffn(h_s, g2, wup, cw, cb, state_conv[l], wdn, fg, ss, ns)

    tiles_per_seq = sp // tm_ffn
    p_conv = tail_p.reshape(bp, tiles_per_seq, CONV_W - 1, D_FF)[:, -1]
    s_conv = tail_s.reshape(bs, CONV_W - 1, D_FF)
    return (
        y_p.reshape(bp, sp, D_MODEL),
        y_s.reshape(bs, ss, D_MODEL),
        ckv_p.reshape(1, bp, sp, KV_LORA),
        kr_p.reshape(1, bp, sp, ROPE_DIM),
        z_p.reshape(bp, sp, D_POOL)[:, sp - POOL_STATE:][None],
        p_conv[None],
        ckv_s.reshape(1, bs, ss, KV_LORA),
        kr_s.reshape(1, bs, ss, ROPE_DIM),
        z_s.reshape(bs, ss, D_POOL)[:, ss - POOL_STATE:][None],
        s_conv[None],
    )
```

```python
import functools

import jax
import jax.numpy as jnp
import numpy as np
from jax import lax
from jax.experimental import pallas as pl
from jax.experimental.pallas import tpu as pltpu

D_MODEL = 2048
CHUNK = 64
D_POOL = 1024
POOL_WINDOWS = (2, 4, 8, 16)
POOL_GROUP = D_POOL // len(POOL_WINDOWS)
POOL_STATE = max(POOL_WINDOWS) - 1
POOL_HALO = 8 * (max(POOL_WINDOWS).bit_length() - 1)
N_HEADS = 16
Q_LORA = 512
KV_LORA = 512
NOPE_DIM = 128
ROPE_DIM = 64
V_DIM = 128
QK_DIM = NOPE_DIM + ROPE_DIM
V_AUG = V_DIM + 16
ATTN_SCALE = QK_DIM ** -0.5
LOG2E = 1.4426950408889634
Q_SCALE = ATTN_SCALE * LOG2E
ROPE_BASE = 10000.0
D_FF = 6144
CONV_W = 3
CONV_HALO = 8
FFN_ROW_CHUNKS = 2
EPS = 1e-6
MASK_VALUE = -1e30

COL_Q = D_POOL
COL_KV = COL_Q + Q_LORA
COL_GATE = COL_KV + KV_LORA

V7X_VMEM_LIMIT = 56 * 1024 * 1024

F32 = jnp.float32
BF16 = jnp.bfloat16


def _rms(x, g):
    return x * lax.rsqrt(jnp.mean(x * x, axis=-1, keepdims=True) + EPS) * g


def _dot(a, b):
    return jnp.dot(a, b, preferred_element_type=F32)


def _dot_nt(a, b):
    return lax.dot_general(a, b, (((1,), (1,)), ((), ())), preferred_element_type=F32)


def _dot_tn(a, b):
    return lax.dot_general(a, b, (((0,), (0,)), ((), ())), preferred_element_type=F32)


def _resident(shape):
    zeros = (0,) * len(shape)
    return pl.BlockSpec(shape, lambda *_: zeros, pipeline_mode=pl.Buffered(1))


def _params(semantics):
    return pltpu.CompilerParams(dimension_semantics=semantics, vmem_limit_bytes=V7X_VMEM_LIMIT)


def _in_proj_kernel(x_ref, g1_ref, w_ref, wg_ref, wr_ref, qg_ref, kvg_ref, cos_ref, sin_ref,
                    z_ref, cq_ref, ckv_ref, kr_ref, gate_ref):
    u = _rms(x_ref[...], g1_ref[...]).astype(BF16)
    z_ref[...] = _dot(u, w_ref[:, 0:COL_Q])
    cq_ref[...] = _rms(_dot(u, w_ref[:, COL_Q:COL_KV]), qg_ref[...]).astype(BF16)
    ckv_ref[...] = _rms(_dot(u, w_ref[:, COL_KV:COL_GATE]), kvg_ref[...])
    r = _dot(u, wr_ref[...])
    kr_ref[...] = r[:, :ROPE_DIM] * cos_ref[...] + r[:, ROPE_DIM:] * sin_ref[...]
    step = 1024
    for lo in range(0, 2 * D_MODEL, step):
        logits = _dot(u, wg_ref[:, lo:lo + step])
        gate_ref[:, lo:lo + step] = 1.0 / (1.0 + jnp.exp(-logits))


def _in_proj(x, g1, w_main, w_gate, w_rope, qg, kvg, cos_t, sin_t, tm):
    n = x.shape[0]
    table_blocks = cos_t.shape[0] // tm
    row = lambda i: (i, 0)
    tab = lambda i: (i % table_blocks, 0)
    return pl.pallas_call(
        _in_proj_kernel,
        grid=(n // tm,),
        in_specs=[
            pl.BlockSpec((tm, D_MODEL), row),
            _resident((1, D_MODEL)),
            _resident((D_MODEL, COL_GATE)),
            _resident((D_MODEL, 2 * D_MODEL)),
            _resident((D_MODEL, 2 * ROPE_DIM)),
            _resident((1, Q_LORA)),
            _resident((1, KV_LORA)),
            pl.BlockSpec((tm, ROPE_DIM), tab),
            pl.BlockSpec((tm, ROPE_DIM), tab),
        ],
        out_specs=[
            pl.BlockSpec((tm, D_POOL), row),
            pl.BlockSpec((tm, Q_LORA), row),
            pl.BlockSpec((tm, KV_LORA), row),
            pl.BlockSpec((tm, ROPE_DIM), row),
            pl.BlockSpec((tm, 2 * D_MODEL), row),
        ],
        out_shape=[
            jax.ShapeDtypeStruct((n, D_POOL), F32),
            jax.ShapeDtypeStruct((n, Q_LORA), BF16),
            jax.ShapeDtypeStruct((n, KV_LORA), F32),
            jax.ShapeDtypeStruct((n, ROPE_DIM), F32),
            jax.ShapeDtypeStruct((n, 2 * D_MODEL), F32),
        ],
        compiler_params=_params(("arbitrary",)),
        name="in_proj",
    )(x, g1, w_main, w_gate, w_rope, qg, kvg, cos_t, sin_t)


def _q_proj_kernel(cq_ref, cos_ref, sin_ref, wq_ref, q_ref, *, n_seg, seg_len):
    cq = cq_ref[...]
    cos = cos_ref[...] * Q_SCALE
    sin = sin_ref[...] * Q_SCALE

    def head(h, carry):
        res = _dot(cq, wq_ref[h])
        qn = (res[:, :NOPE_DIM] * Q_SCALE).astype(BF16)
        qr = (res[:, NOPE_DIM:QK_DIM] * cos + res[:, QK_DIM:] * sin).astype(BF16)
        for s in range(n_seg):
            rows = slice(s * seg_len, (s + 1) * seg_len)
            q_ref[s, h, :, 0:NOPE_DIM] = qn[rows]
            q_ref[s, h, :, NOPE_DIM:QK_DIM] = qr[rows]
        return carry

    lax.fori_loop(0, N_HEADS, head, 0)


def _q_proj(cq, cos_t, sin_t, wq, batch, seq, tm):
    n = cq.shape[0]
    n_seg = tm // seq
    table_blocks = cos_t.shape[0] // tm
    return pl.pallas_call(
        functools.partial(_q_proj_kernel, n_seg=n_seg, seg_len=seq),
        grid=(n // tm,),
        in_specs=[
            pl.BlockSpec((tm, Q_LORA), lambda i: (i, 0)),
            pl.BlockSpec((tm, ROPE_DIM), lambda i: (i % table_blocks, 0)),
            pl.BlockSpec((tm, ROPE_DIM), lambda i: (i % table_blocks, 0)),
            _resident((N_HEADS, Q_LORA, 2 * NOPE_DIM)),
        ],
        out_specs=pl.BlockSpec((n_seg, N_HEADS, seq, QK_DIM), lambda i: (i, 0, 0, 0)),
        out_shape=jax.ShapeDtypeStruct((batch, N_HEADS, seq, QK_DIM), BF16),
        compiler_params=_params(("arbitrary",)),
        name="q_proj",
    )(cq, cos_t, sin_t, wq)


def _qkv_proj_kernel(cq_ref, cost_ref, sint_ref, wqt_ref, ckv_ref, kr_ref, wuk_ref, wuvt_ref,
                     qt_ref, k_ref, vt_ref):
    cq = cq_ref[...]
    cos_t = cost_ref[...] * Q_SCALE
    sin_t = sint_ref[...] * Q_SCALE
    ckv = ckv_ref[...].astype(BF16)
    krb = kr_ref[...].astype(BF16)
    ones = jnp.ones((V_AUG - V_DIM, ckv.shape[0]), BF16)

    for hp in range(N_HEADS // 2):
        for hh in range(2):
            h = 2 * hp + hh
            res = _dot_nt(wqt_ref[h], cq)
            qt_ref[0, h, 0:NOPE_DIM, :] = (res[0:NOPE_DIM] * Q_SCALE).astype(BF16)
            qt_ref[0, h, NOPE_DIM:QK_DIM, :] = (
                res[NOPE_DIM:QK_DIM] * cos_t + res[QK_DIM:] * sin_t).astype(BF16)
        kn = _dot(ckv, wuk_ref[hp]).astype(BF16)
        vt = _dot_nt(wuvt_ref[hp], ckv).astype(BF16)
        for hh in range(2):
            h = 2 * hp + hh
            k_ref[0, h, 0, :, 0:NOPE_DIM] = kn[:, hh * NOPE_DIM:(hh + 1) * NOPE_DIM]
            k_ref[0, h, 0, :, NOPE_DIM:QK_DIM] = krb
            vt_ref[0, h, 0, 0:V_DIM, :] = vt[hh * V_DIM:(hh + 1) * V_DIM]
            vt_ref[0, h, 0, V_DIM:V_AUG, :] = ones


def _qkv_proj(cq, cos_tt, sin_tt, wqt, ckv, kr, wuk2, wuvt2, batch, seq, tm):
    n = cq.shape[0]
    tiles_per_seq = seq // tm
    row = lambda i: (i, 0)
    tab = lambda i: (0, i % tiles_per_seq)
    tile = lambda i: (i // tiles_per_seq, 0, i % tiles_per_seq, 0, 0)
    return pl.pallas_call(
        _qkv_proj_kernel,
        grid=(n // tm,),
        in_specs=[
            pl.BlockSpec((tm, Q_LORA), row),
            pl.BlockSpec((ROPE_DIM, tm), tab),
            pl.BlockSpec((ROPE_DIM, tm), tab),
            _resident((N_HEADS, 2 * NOPE_DIM, Q_LORA)),
            pl.BlockSpec((tm, KV_LORA), row),
            pl.BlockSpec((tm, ROPE_DIM), row),
            _resident((N_HEADS // 2, KV_LORA, 2 * NOPE_DIM)),
            _resident((N_HEADS // 2, 2 * V_DIM, KV_LORA)),
        ],
        out_specs=[
            pl.BlockSpec((1, N_HEADS, QK_DIM, tm), lambda i: (i // tiles_per_seq, 0, 0, i % tiles_per_seq)),
            pl.BlockSpec((1, N_HEADS, 1, tm, QK_DIM), tile),
            pl.BlockSpec((1, N_HEADS, 1, V_AUG, tm), tile),
        ],
        out_shape=[
            jax.ShapeDtypeStruct((batch, N_HEADS, QK_DIM, seq), BF16),
            jax.ShapeDtypeStruct((batch, N_HEADS, tiles_per_seq, tm, QK_DIM), BF16),
            jax.ShapeDtypeStruct((batch, N_HEADS, tiles_per_seq, V_AUG, tm), BF16),
        ],
        compiler_params=_params(("arbitrary",)),
        name="qkv_proj",
    )(cq, cos_tt, sin_tt, wqt, ckv, kr, wuk2, wuvt2)


def _flash_kernel(qt_ref, qtn_ref, k_ref, vt_ref, o_ref,
                  s_ref, sf_ref, acc_ref, m_ref, cmax_ref, cmaxf_ref, bias_ref, *, heads, tile):
    qi = pl.program_id(2)

    @pl.when((pl.program_id(0) == 0) & (pl.program_id(1) == 0) & (qi == 0))
    def _():
        key_chunk = lax.broadcasted_iota(jnp.int32, (tile, tile), 0) // CHUNK
        query_chunk = lax.broadcasted_iota(jnp.int32, (tile, tile), 1) // CHUNK
        bias_ref[...] = jnp.where(key_chunk <= query_chunk, 0.0, MASK_VALUE)

    def scores(g, j, slot):
        st = _dot(k_ref[0, g, j], qt_ref[0, g])
        s_ref[slot, g] = st
        cmax_ref[slot, g] = jnp.max(st, axis=0, keepdims=True)

    def next_first_scores(g):
        st = _dot(k_ref[0, g, 0], qtn_ref[0, g])
        sf_ref[g] = st
        cmaxf_ref[g] = jnp.max(st, axis=0, keepdims=True)

    def softmax(g, st, tile_max):
        m_old = m_ref[g]
        m_new = jnp.maximum(m_old, tile_max)
        m_ref[g] = m_new
        return jnp.exp2(st - m_new).astype(BF16), jnp.exp2(m_old - m_new)

    def values(g, j, p, alpha):
        acc_ref[g] = alpha * acc_ref[g] + _dot(vt_ref[0, g, j], p)

    def steps(j0, count, slot0, from_first=False):
        pending = None
        for t in range(count):
            cur = (slot0 + t) % 2
            for g in range(heads):
                if from_first and t == 0:
                    p, alpha = softmax(g, sf_ref[g], cmaxf_ref[g])
                else:
                    p, alpha = softmax(g, s_ref[cur, g], cmax_ref[cur, g])
                scores(g, j0 + t + 1, 1 - cur)
                if pending is not None:
                    values(*pending)
                pending = (g, j0 + t, p, alpha)
        values(*pending)

    for g in range(heads):
        m_ref[g] = jnp.full((1, tile), MASK_VALUE, F32)
        acc_ref[g] = jnp.zeros((V_AUG, tile), F32)

    @pl.when(qi == 0)
    def _():
        for g in range(heads):
            scores(g, 0, 0)

    @pl.when(qi >= 1)
    def _():
        steps(0, 1, 0, from_first=True)

    def two_steps(jj, carry):
        steps(1 + 2 * jj, 2, 1)
        return carry

    lax.fori_loop(0, jnp.maximum(qi - 1, 0) // 2, two_steps, 0)

    @pl.when((qi >= 2) & (qi % 2 == 0))
    def _():
        steps(qi - 1, 1, 1)

    def finish(g, p, alpha):
        acc = alpha * acc_ref[g] + _dot(vt_ref[0, g, qi], p)
        o = acc[0:V_DIM] / acc[V_DIM:V_DIM + 1]
        o_ref[:, g * V_DIM:(g + 1) * V_DIM] = o.T.astype(BF16)

    diag_slot = qi % 2
    pending = None
    for g in range(heads):
        st = s_ref[diag_slot, g] + bias_ref[...]
        p, alpha = softmax(g, st, jnp.max(st, axis=0, keepdims=True))
        next_first_scores(g)
        if pending is not None:
            finish(*pending)
        pending = (g, p, alpha)
    finish(*pending)


def _flash(qt, k, vt, heads=4):
    batch, _, _, seq = qt.shape
    n_tiles, tile = k.shape[2], k.shape[3]
    once = pl.Buffered(1)
    return pl.pallas_call(
        functools.partial(_flash_kernel, heads=heads, tile=tile),
        grid=(batch, N_HEADS // heads, n_tiles),
        in_specs=[
            pl.BlockSpec((1, heads, QK_DIM, tile), lambda b, hg, qi: (b, hg, 0, qi)),
            pl.BlockSpec((1, heads, QK_DIM, tile),
                         lambda b, hg, qi: (b, hg, 0, jnp.minimum(qi + 1, n_tiles - 1))),
            pl.BlockSpec((1, heads, n_tiles, tile, QK_DIM), lambda b, hg, qi: (b, hg, 0, 0, 0),
                         pipeline_mode=once),
            pl.BlockSpec((1, heads, n_tiles, V_AUG, tile), lambda b, hg, qi: (b, hg, 0, 0, 0),
                         pipeline_mode=once),
        ],
        out_specs=pl.BlockSpec((tile, heads * V_DIM), lambda b, hg, qi: (b * n_tiles + qi, hg)),
        out_shape=jax.ShapeDtypeStruct((batch * seq, N_HEADS * V_DIM), BF16),
        scratch_shapes=[
            pltpu.VMEM((2, heads, tile, tile), F32),
            pltpu.VMEM((heads, tile, tile), F32),
            pltpu.VMEM((heads, V_AUG, tile), F32),
            pltpu.VMEM((heads, 1, tile), F32),
            pltpu.VMEM((2, heads, 1, tile), F32),
            pltpu.VMEM((heads, 1, tile), F32),
            pltpu.VMEM((tile, tile), F32),
        ],
        compiler_params=_params(("arbitrary", "arbitrary", "arbitrary")),
        name="flash",
    )(qt, qt, k, vt)


def _decode_kernel(q_ref, wqa_ref, wuv_ref, cc_ref, ckr_ref, cn_ref, krn_ref, o_ref,
                   qt, s_ref, m_s, l_s, acc_s, *, seq, halves):
    kt = pl.program_id(1)
    half = cc_ref.shape[1] // halves

    @pl.when(kt == 0)
    def _():
        for hp in range(N_HEADS // 2):
            pair = [_dot_nt(wqa_ref[2 * hp + hh], q_ref[0, 2 * hp + hh]) for hh in range(2)]
            qt[:, 2 * hp * seq:2 * (hp + 1) * seq] = jnp.concatenate(pair, axis=1).astype(BF16)
        m_s[...] = jnp.full(m_s.shape, MASK_VALUE, F32)
        l_s[...] = jnp.zeros(l_s.shape, F32)
        acc_s[...] = jnp.zeros(acc_s.shape, F32)

    def scores(kc, kr, slot):
        n = kc.shape[0]
        s_ref[slot, 0:n, :] = _dot(kc, qt[0:KV_LORA, :]) + _dot(kr, qt[KV_LORA:, :])

    def update(kc, slot):
        n = kc.shape[0]
        st = s_ref[slot, 0:n, :]
        m_old = m_s[...]
        m_new = jnp.maximum(m_old, jnp.max(st, axis=0, keepdims=True))
        alpha = jnp.exp2(m_old - m_new)
        p = jnp.exp2(st - m_new)
        l_s[...] = alpha * l_s[...] + jnp.sum(p, axis=0, keepdims=True)
        acc_s[...] = alpha * acc_s[...] + _dot_tn(kc, p.astype(BF16))
        m_s[...] = m_new

    parts = [cc_ref[0, i * half:(i + 1) * half, :].astype(BF16) for i in range(halves)]
    for i in range(halves):
        scores(parts[i], ckr_ref[0, i * half:(i + 1) * half, :].astype(BF16), i)
    for i in range(halves):
        update(parts[i], i)

    @pl.when(kt == pl.num_programs(1) - 1)
    def _():
        new = cn_ref[...].astype(BF16)
        scores(new, krn_ref[...].astype(BF16), 0)
        update(new, 0)
        o_lat = (acc_s[...] / l_s[...]).T.astype(BF16)
        for h in range(N_HEADS):
            o_h = _dot(o_lat[h * seq:(h + 1) * seq], wuv_ref[h])
            o_ref[:, h * V_DIM:(h + 1) * V_DIM] = o_h.astype(BF16)


def _decode_attn(q, wqa, wuv, cache_ckv, cache_kr, ckv_new, kr_new, key_tile=1024, halves=2):
    batch, _, seq, _ = q.shape
    past = cache_ckv.shape[1]
    cols = N_HEADS * seq
    return pl.pallas_call(
        functools.partial(_decode_kernel, seq=seq, halves=halves),
        grid=(batch, past // key_tile),
        in_specs=[
            pl.BlockSpec((1, N_HEADS, seq, QK_DIM), lambda b, kt: (b, 0, 0, 0)),
            _resident((N_HEADS, KV_LORA + ROPE_DIM, QK_DIM)),
            _resident((N_HEADS, KV_LORA, V_DIM)),
            pl.BlockSpec((1, key_tile, KV_LORA), lambda b, kt: (b, kt, 0)),
            pl.BlockSpec((1, key_tile, ROPE_DIM), lambda b, kt: (b, kt, 0)),
            pl.BlockSpec((seq, KV_LORA), lambda b, kt: (b, 0)),
            pl.BlockSpec((seq, ROPE_DIM), lambda b, kt: (b, 0)),
        ],
        out_specs=pl.BlockSpec((seq, N_HEADS * V_DIM), lambda b, kt: (b, 0)),
        out_shape=jax.ShapeDtypeStruct((batch * seq, N_HEADS * V_DIM), BF16),
        scratch_shapes=[
            pltpu.VMEM((KV_LORA + ROPE_DIM, cols), BF16),
            pltpu.VMEM((halves, key_tile // halves, cols), F32),
            pltpu.VMEM((1, cols), F32),
            pltpu.VMEM((1, cols), F32),
            pltpu.VMEM((KV_LORA, cols), F32),
        ],
        compiler_params=_params(("arbitrary", "arbitrary")),
        name="decode_attn",
    )(q, wqa, wuv, cache_ckv, cache_kr, ckv_new, kr_new)


def _window_sum(ext, tmp, cols, w):
    end = ext.shape[0]
    levels = w.bit_length() - 1
    for k in range(levels):
        lo, shift = 8 * (k + 1), 2 ** k
        if k == 0:
            val = ext[lo:end, cols] + ext[lo - shift:end - shift, cols]
        else:
            below = tmp.at[(k - 1) % 2]
            val = below[lo:end, :] + below[lo - shift:end - shift, :]
        if k < levels - 1:
            tmp[k % 2, lo:end, :] = val
    return val[POOL_HALO - 8 * levels:]


def _merge_kernel(z_ref, zprev_ref, pre_ref, att_ref, gate_ref, x_ref, pw_ref, ps_ref,
                  wpo_ref, wmo_ref, wo_ref, h_ref, ext, tmp, ypool, *, n_seg, seg_len, tiles_per_seq, pos0):
    t = pl.program_id(0) % tiles_per_seq
    row = lax.broadcasted_iota(jnp.int32, (seg_len, 1), 0)
    pos = pos0 + t * seg_len + row
    br_b = _dot(att_ref[...], wmo_ref[...])
    for s in range(n_seg):
        if tiles_per_seq == 1:
            halo = pre_ref[s]
        else:
            halo = jnp.where(t == 0, pre_ref[0], zprev_ref[...])
        ext[0:POOL_HALO, :] = halo
        ext[POOL_HALO:, :] = z_ref[s * seg_len:(s + 1) * seg_len, :]
        for g, w in enumerate(POOL_WINDOWS):
            cols = slice(g * POOL_GROUP, (g + 1) * POOL_GROUP)
            cur = ext[POOL_HALO:, cols]
            total = _window_sum(ext, tmp, cols, w)
            count = jnp.minimum(pos + 1, w).astype(F32)
            diff = total / count - cur
            y = _dot(diff.astype(BF16), pw_ref[g]) * ps_ref[:, cols]
            ypool[s * seg_len:(s + 1) * seg_len, cols] = y.astype(BF16)
    br_a = _dot(ypool[...], wpo_ref[...])
    merged = gate_ref[:, 0:D_MODEL] * br_a + gate_ref[:, D_MODEL:] * br_b
    h_ref[...] = x_ref[...] + _dot(merged.astype(BF16), wo_ref[...])


def _merge(z, prefix, att, gates, x, pool_w, pool_scale, w_pool_out, w_mla_out, w_out, seq, pos0, tm):
    n = x.shape[0]
    tiles_per_seq = max(seq // tm, 1)
    n_seg = max(tm // seq, 1)
    seg_len = tm // n_seg
    halo_blocks = tm // POOL_HALO
    row = lambda i: (i, 0)
    return pl.pallas_call(
        functools.partial(_merge_kernel, n_seg=n_seg, seg_len=seg_len,
                          tiles_per_seq=tiles_per_seq, pos0=pos0),
        grid=(n // tm,),
        in_specs=[
            pl.BlockSpec((tm, D_POOL), row),
            pl.BlockSpec((POOL_HALO, D_POOL), lambda i: (jnp.maximum(i * halo_blocks - 1, 0), 0)),
            pl.BlockSpec((n_seg, POOL_HALO, D_POOL), lambda i: (i // tiles_per_seq, 0, 0)),
            pl.BlockSpec((tm, N_HEADS * V_DIM), row),
            pl.BlockSpec((tm, 2 * D_MODEL), row),
            pl.BlockSpec((tm, D_MODEL), row),
            _resident((len(POOL_WINDOWS), POOL_GROUP, POOL_GROUP)),
            _resident((1, D_POOL)),
            _resident((D_POOL, D_MODEL)),
            _resident((N_HEADS * V_DIM, D_MODEL)),
            _resident((D_MODEL, D_MODEL)),
        ],
        out_specs=pl.BlockSpec((tm, D_MODEL), row),
        out_shape=jax.ShapeDtypeStruct((n, D_MODEL), F32),
        scratch_shapes=[
            pltpu.VMEM((POOL_HALO + seg_len, D_POOL), F32),
            pltpu.VMEM((2, POOL_HALO + seg_len, POOL_GROUP), F32),
            pltpu.VMEM((tm, D_POOL), BF16),
        ],
        compiler_params=_params(("arbitrary",)),
        name="merge",
    )(z, z, prefix, att, gates, x, pool_w, pool_scale, w_pool_out, w_mla_out, w_out)


def _ffn_kernel(h_ref, g2_ref, wa_ref, wb_ref, cw_ref, cb_ref, pre_ref, wd_ref, fg_ref,
                y_ref, tail_ref, hn, ext, carry, *, n_seg, seg_len, tiles_per_seq):
    i = pl.program_id(0)
    j = pl.program_id(1)
    t = i % tiles_per_seq

    @pl.when(j == 0)
    def _():
        hn[...] = _rms(h_ref[...], g2_ref[...]).astype(BF16)
        y_ref[...] = jnp.zeros(y_ref.shape, F32)

    if tiles_per_seq > 1:
        @pl.when(t == 0)
        def _():
            carry[j] = pre_ref[0]

    tm = hn.shape[0]
    rc = tm // FFN_ROW_CHUNKS
    up = []
    for c in range(FFN_ROW_CHUNKS):
        hc = hn[c * rc:(c + 1) * rc, :]
        up.append((_dot(hc, wa_ref[...]), _dot(hc, wb_ref[...])))

    def conv_gelu(seg, lo, a_rows, b_rows):
        n = a_rows.shape[0]
        base = CONV_HALO + lo
        ext[seg, base:base + n, :] = a_rows
        c = cb_ref[...] + ext[seg, base - 2:base - 2 + n, :] * cw_ref[0:1, :]
        c = c + ext[seg, base - 1:base - 1 + n, :] * cw_ref[1:2, :]
        c = c + a_rows * cw_ref[2:3, :]
        gelu = 0.5 * c * (1.0 + lax.erf(c * (2.0 ** -0.5)))
        return (gelu * b_rows).astype(BF16)

    for c in range(FFN_ROW_CHUNKS):
        a_c, b_c = up[c]
        if n_seg == 1:
            if c == 0:
                ext[0, CONV_HALO - (CONV_W - 1):CONV_HALO, :] = carry[j] if tiles_per_seq > 1 else pre_ref[0]
            gated_c = conv_gelu(0, c * rc, a_c, b_c)
            if c == FFN_ROW_CHUNKS - 1:
                last = a_c[rc - (CONV_W - 1):]
                tail_ref[0, 0] = last
                if tiles_per_seq > 1:
                    carry[j] = last
        else:
            per_chunk = n_seg // FFN_ROW_CHUNKS
            parts = []
            for k in range(per_chunk):
                s = c * per_chunk + k
                rows = slice(k * seg_len, (k + 1) * seg_len)
                ext[s, CONV_HALO - (CONV_W - 1):CONV_HALO, :] = pre_ref[s]
                parts.append(conv_gelu(s, 0, a_c[rows], b_c[rows]))
                tail_ref[0, s] = a_c[rows][seg_len - (CONV_W - 1):]
            gated_c = jnp.concatenate(parts, axis=0)
        y_ref[c * rc:(c + 1) * rc, :] += _dot(gated_c, wd_ref[...])

    @pl.when(j == pl.num_programs(1) - 1)
    def _():
        y_ref[...] = _rms(h_ref[...] + y_ref[...], fg_ref[...])


def _ffn(h, norm2_g, w_up, conv_w, conv_b, prefix, w_down, final_g, seq, tm, tf=512):
    n = h.shape[0]
    tiles_per_seq = max(seq // tm, 1)
    n_seg = max(tm // seq, 1)
    seg_len = tm // n_seg
    nff = D_FF // tf
    return pl.pallas_call(
        functools.partial(_ffn_kernel, n_seg=n_seg, seg_len=seg_len, tiles_per_seq=tiles_per_seq),
        grid=(n // tm, nff),
        in_specs=[
            pl.BlockSpec((tm, D_MODEL), lambda i, j: (i, 0)),
            _resident((1, D_MODEL)),
            pl.BlockSpec((D_MODEL, tf), lambda i, j: (0, j)),
            pl.BlockSpec((D_MODEL, tf), lambda i, j: (0, nff + j)),
            pl.BlockSpec((CONV_W, tf), lambda i, j: (0, j)),
            pl.BlockSpec((1, tf), lambda i, j: (0, j)),
            pl.BlockSpec((n_seg, CONV_W - 1, tf), lambda i, j: (i // tiles_per_seq, 0, j)),
            pl.BlockSpec((tf, D_MODEL), lambda i, j: (j, 0)),
            _resident((1, D_MODEL)),
        ],
        out_specs=[
            pl.BlockSpec((tm, D_MODEL), lambda i, j: (i, 0)),
            pl.BlockSpec((1, n_seg, CONV_W - 1, tf), lambda i, j: (i, 0, 0, j)),
        ],
        out_shape=[
            jax.ShapeDtypeStruct((n, D_MODEL), F32),
            jax.ShapeDtypeStruct((n // tm, n_seg, CONV_W - 1, D_FF), F32),
        ],
        scratch_shapes=[
            pltpu.VMEM((tm, D_MODEL), BF16),
            pltpu.VMEM((n_seg, CONV_HALO + seg_len, tf), F32),
            pltpu.VMEM((nff, CONV_W - 1, tf), F32),
        ],
        compiler_params=_params(("arbitrary", "arbitrary")),
        name="ffn",
    )(h, norm2_g, w_up, w_up, conv_w, conv_b, prefix, w_down, final_g)


def _rope_tables(pos0, length, rows):
    pos = (pos0 + np.arange(length)).astype(np.float64)
    inv = ROPE_BASE ** (-(np.arange(ROPE_DIM // 2, dtype=np.float64) * 2.0 / ROPE_DIM))
    ang = pos[:, None] * inv[None, :]
    cos, sin = np.cos(ang), np.sin(ang)
    cos_t = np.concatenate([cos, cos], axis=-1).astype(np.float32)
    sin_t = np.concatenate([-sin, sin], axis=-1).astype(np.float32)
    reps = max(rows // length, 1)
    return np.tile(cos_t, (reps, 1)), np.tile(sin_t, (reps, 1))


def _swap_halves(w):
    return jnp.concatenate([w[..., ROPE_DIM // 2:], w[..., :ROPE_DIM // 2]], axis=-1)


def kernel(x_prompt, x_sample, cache_ckv, cache_krope, state_pool, state_conv, norm1_g, w_in, pool_w, pool_scale, w_pool_out, q_norm_g, w_uq, kv_norm_g, w_uk, w_uv, w_mla_out, w_out, norm2_g, w_up, conv_w, conv_b, w_down, final_g):
    l = 0
    bp, sp, _ = x_prompt.shape
    bs, ss, _ = x_sample.shape
    past = cache_ckv.shape[2]

    w = w_in[l]
    w_main = w[:, :COL_GATE].astype(BF16)
    w_gate = w[:, COL_GATE + ROPE_DIM:].astype(BF16)
    w_rope = w[:, COL_GATE:COL_GATE + ROPE_DIM]
    w_rope = jnp.concatenate([w_rope, _swap_halves(w_rope)], axis=-1).astype(BF16)
    w_in_parts = (w_main, w_gate, w_rope)
    wq = w_uq[l].reshape(Q_LORA, N_HEADS, QK_DIM)
    wq = jnp.concatenate([wq, _swap_halves(wq[..., NOPE_DIM:])], axis=-1)
    wqt = wq.transpose(1, 2, 0).astype(BF16)
    wq = wq.transpose(1, 0, 2).astype(BF16)
    wuk = w_uk[l].astype(BF16)
    wuv = w_uv[l].astype(BF16)
    wuk2 = wuk.reshape(KV_LORA, N_HEADS // 2, 2 * NOPE_DIM).transpose(1, 0, 2)
    wuvt2 = wuv.reshape(KV_LORA, N_HEADS // 2, 2 * V_DIM).transpose(1, 2, 0)
    wuk_h = wuk.transpose(1, 0, 2)
    wuv_h = wuv.transpose(1, 0, 2)
    eye = jnp.broadcast_to(jnp.eye(ROPE_DIM, dtype=BF16), (N_HEADS, ROPE_DIM, ROPE_DIM))
    wqa = jnp.concatenate([
        jnp.concatenate([wuk_h, jnp.zeros((N_HEADS, KV_LORA, ROPE_DIM), BF16)], axis=-1),
        jnp.concatenate([jnp.zeros((N_HEADS, ROPE_DIM, NOPE_DIM), BF16), eye], axis=-1)], axis=1)
    pw = pool_w[l].astype(BF16)
    wpo = w_pool_out[l].astype(BF16)
    wmo = w_mla_out[l].astype(BF16)
    wo = w_out[l].astype(BF16)
    wup = w_up[l].astype(BF16)
    wdn = w_down[l].astype(BF16)
    g1 = norm1_g[l][None]
    qg = q_norm_g[l][None]
    kvg = kv_norm_g[l][None]
    g2 = norm2_g[l][None]
    fg = final_g[None]
    ps = pool_scale[l][None]
    cb = conv_b[l][None]
    cw = conv_w[l]

    tm_in, tm_head, tm_merge, tm_ffn = 256, 512, 256, 512
    xp = x_prompt.reshape(bp * sp, D_MODEL)
    cos_p, sin_p = _rope_tables(0, sp, sp)
    z_p, cq_p, ckv_p, kr_p, gates_p = _in_proj(xp, g1, *w_in_parts, qg, kvg, cos_p, sin_p, tm_in)
    qt_p, k_p, vt_p = _qkv_proj(cq_p, cos_p.T, sin_p.T, wqt, ckv_p, kr_p, wuk2, wuvt2, bp, sp, tm_head)
    att_p = _flash(qt_p, k_p, vt_p)
    pool0 = jnp.zeros((bp, POOL_HALO, D_POOL), F32)
    h_p = _merge(z_p, pool0, att_p, gates_p, xp, pw, ps, wpo, wmo, wo, sp, 0, tm_merge)
    conv0 = jnp.zeros((bp, CONV_W - 1, D_FF), F32)
    y_p, tail_p = _ffn(h_p, g2, wup, cw, cb, conv0, wdn, fg, sp, tm_ffn)

    ns = bs * ss
    xs = x_sample.reshape(ns, D_MODEL)
    cos_s, sin_s = _rope_tables(past, ss, ns)
    z_s, cq_s, ckv_s, kr_s, gates_s = _in_proj(xs, g1, *w_in_parts, qg, kvg, cos_s, sin_s, 256)
    q_s = _q_proj(cq_s, cos_s, sin_s, wq, bs, ss, ns)
    att_s = _decode_attn(q_s, wqa, wuv_h, cache_ckv[l], cache_krope[l], ckv_s, kr_s)
    pool_pre = jnp.pad(state_pool[l], ((0, 0), (POOL_HALO - POOL_STATE, 0), (0, 0)))
    h_s = _merge(z_s, pool_pre, att_s, gates_s, xs, pw, ps, wpo, wmo, wo, ss, past, 256)
    y_s, tail_s = _ffn(h_s, g2, wup, cw, cb, state_conv[l], wdn, fg, ss, ns)

    tiles_per_seq = sp // tm_ffn
    p_conv = tail_p.reshape(bp, tiles_per_seq, CONV_W - 1, D_FF)[:, -1]
    s_conv = tail_s.reshape(bs, CONV_W - 1, D_FF)
    return (
        y_p.reshape(bp, sp, D_MODEL),
        y_s.reshape(bs, ss, D_MODEL),
        ckv_p.reshape(1, bp, sp, KV_LORA),
        kr_p.reshape(1, bp, sp, ROPE_DIM),
        z_p.reshape(bp, sp, D_POOL)[:, sp - POOL_STATE:][None],
        p_conv[None],
        ckv_s.reshape(1, bs, ss, KV_LORA),
        kr_s.reshape(1, bs, ss, ROPE_DIM),
        z_s.reshape(bs, ss, D_POOL)[:, ss - POOL_STATE:][None],
        s_conv[None],
    )
```

```python
import functools

import jax
import jax.numpy as jnp
import numpy as np
from jax import lax
from jax.experimental import pallas as pl
from jax.experimental.pallas import tpu as pltpu

D_MODEL = 2048
CHUNK = 64
D_POOL = 1024
POOL_WINDOWS = (2, 4, 8, 16)
POOL_GROUP = D_POOL // len(POOL_WINDOWS)
POOL_STATE = max(POOL_WINDOWS) - 1
POOL_HALO = 8 * (max(POOL_WINDOWS).bit_length() - 1)
N_HEADS = 16
Q_LORA = 512
KV_LORA = 512
NOPE_DIM = 128
ROPE_DIM = 64
V_DIM = 128
V_AUG = V_DIM + 16
QK_DIM = NOPE_DIM + ROPE_DIM
ATTN_SCALE = QK_DIM ** -0.5
LOG2E = 1.4426950408889634
Q_SCALE = ATTN_SCALE * LOG2E
ROPE_BASE = 10000.0
D_FF = 6144
CONV_W = 3
CONV_HALO = 8
FFN_ROW_CHUNKS = 2
EPS = 1e-6
MASK_VALUE = -1e30

COL_Q = D_POOL
COL_KV = COL_Q + Q_LORA
COL_GATE = COL_KV + KV_LORA

V7X_VMEM_LIMIT = 60000 * 1024

F32 = jnp.float32
BF16 = jnp.bfloat16


def _rms(x, g):
    return x * lax.rsqrt(jnp.mean(x * x, axis=-1, keepdims=True) + EPS) * g


def _dot(a, b):
    return jnp.dot(a, b, preferred_element_type=F32)


def _dot_nt(a, b):
    return lax.dot_general(a, b, (((1,), (1,)), ((), ())), preferred_element_type=F32)


def _dot_tn(a, b):
    return lax.dot_general(a, b, (((0,), (0,)), ((), ())), preferred_element_type=F32)


def _resident(shape):
    zeros = (0,) * len(shape)
    return pl.BlockSpec(shape, lambda *_: zeros, pipeline_mode=pl.Buffered(1))


def _params(semantics):
    return pltpu.CompilerParams(dimension_semantics=semantics, vmem_limit_bytes=V7X_VMEM_LIMIT)


def _in_proj_kernel(x_ref, g1_ref, w_ref, wg_ref, wr_ref, qg_ref, kvg_ref, cos_ref, sin_ref,
                    z_ref, cq_ref, ckv_ref, kr_ref, gate_ref):
    u = _rms(x_ref[...], g1_ref[...]).astype(BF16)
    z_ref[...] = _dot(u, w_ref[:, 0:COL_Q])
    cq_ref[...] = _rms(_dot(u, w_ref[:, COL_Q:COL_KV]), qg_ref[...]).astype(BF16)
    ckv_ref[...] = _rms(_dot(u, w_ref[:, COL_KV:COL_GATE]), kvg_ref[...])
    r = _dot(u, wr_ref[...])
    kr_ref[...] = r[:, :ROPE_DIM] * cos_ref[...] + r[:, ROPE_DIM:] * sin_ref[...]
    step = 1024
    for lo in range(0, 2 * D_MODEL, step):
        logits = _dot(u, wg_ref[:, lo:lo + step])
        gate_ref[:, lo:lo + step] = 1.0 / (1.0 + jnp.exp(-logits))


def _in_proj(x, g1, w_main, w_gate, w_rope, qg, kvg, cos_t, sin_t, tm):
    n = x.shape[0]
    table_blocks = cos_t.shape[0] // tm
    row = lambda i: (i, 0)
    tab = lambda i: (i % table_blocks, 0)
    return pl.pallas_call(
        _in_proj_kernel,
        grid=(n // tm,),
        in_specs=[
            pl.BlockSpec((tm, D_MODEL), row),
            _resident((1, D_MODEL)),
            _resident((D_MODEL, COL_GATE)),
            _resident((D_MODEL, 2 * D_MODEL)),
            _resident((D_MODEL, 2 * ROPE_DIM)),
            _resident((1, Q_LORA)),
            _resident((1, KV_LORA)),
            pl.BlockSpec((tm, ROPE_DIM), tab),
            pl.BlockSpec((tm, ROPE_DIM), tab),
        ],
        out_specs=[
            pl.BlockSpec((tm, D_POOL), row),
            pl.BlockSpec((tm, Q_LORA), row),
            pl.BlockSpec((tm, KV_LORA), row),
            pl.BlockSpec((tm, ROPE_DIM), row),
            pl.BlockSpec((tm, 2 * D_MODEL), row),
        ],
        out_shape=[
            jax.ShapeDtypeStruct((n, D_POOL), F32),
            jax.ShapeDtypeStruct((n, Q_LORA), BF16),
            jax.ShapeDtypeStruct((n, KV_LORA), F32),
            jax.ShapeDtypeStruct((n, ROPE_DIM), F32),
            jax.ShapeDtypeStruct((n, 2 * D_MODEL), F32),
        ],
        compiler_params=_params(("arbitrary",)),
        name="in_proj",
    )(x, g1, w_main, w_gate, w_rope, qg, kvg, cos_t, sin_t)


def _q_proj_kernel(cq_ref, cos_ref, sin_ref, wqt_ref, q_ref, *, n_seg, seg_len):
    cq = cq_ref[...]
    cos = cos_ref[...] * Q_SCALE
    sin = sin_ref[...] * Q_SCALE

    def head(h, carry):
        res = _dot_nt(cq, wqt_ref[h])
        qn = (res[:, :NOPE_DIM] * Q_SCALE).astype(BF16)
        qr = (res[:, NOPE_DIM:QK_DIM] * cos + res[:, QK_DIM:] * sin).astype(BF16)
        for s in range(n_seg):
            rows = slice(s * seg_len, (s + 1) * seg_len)
            q_ref[s, h, :, 0:NOPE_DIM] = qn[rows]
            q_ref[s, h, :, NOPE_DIM:QK_DIM] = qr[rows]
        return carry

    lax.fori_loop(0, N_HEADS, head, 0)


def _q_proj(cq, cos_t, sin_t, wqt, batch, seq, tm):
    n = cq.shape[0]
    n_seg = tm // seq
    table_blocks = cos_t.shape[0] // tm
    return pl.pallas_call(
        functools.partial(_q_proj_kernel, n_seg=n_seg, seg_len=seq),
        grid=(n // tm,),
        in_specs=[
            pl.BlockSpec((tm, Q_LORA), lambda i: (i, 0)),
            pl.BlockSpec((tm, ROPE_DIM), lambda i: (i % table_blocks, 0)),
            pl.BlockSpec((tm, ROPE_DIM), lambda i: (i % table_blocks, 0)),
            _resident((N_HEADS, 2 * NOPE_DIM, Q_LORA)),
        ],
        out_specs=pl.BlockSpec((n_seg, N_HEADS, seq, QK_DIM), lambda i: (i, 0, 0, 0)),
        out_shape=jax.ShapeDtypeStruct((batch, N_HEADS, seq, QK_DIM), BF16),
        compiler_params=_params(("arbitrary",)),
        name="q_proj",
    )(cq, cos_t, sin_t, wqt)


def _qkv_proj_kernel(cq_ref, cost_ref, sint_ref, wqt_ref, ckv_ref, kr_ref, wuk_ref, wuvt_ref,
                     qt_ref, k_ref, vt_ref):
    cq = cq_ref[...]
    cos_t = cost_ref[...] * Q_SCALE
    sin_t = sint_ref[...] * Q_SCALE
    ckv = ckv_ref[...].astype(BF16)
    krb = kr_ref[...].astype(BF16)
    ones = jnp.ones((V_AUG - V_DIM, ckv.shape[0]), BF16)

    for hp in range(N_HEADS // 2):
        for hh in range(2):
            h = 2 * hp + hh
            res = _dot_nt(wqt_ref[h], cq)
            qt_ref[0, h, 0:NOPE_DIM, :] = (res[0:NOPE_DIM] * Q_SCALE).astype(BF16)
            qt_ref[0, h, NOPE_DIM:QK_DIM, :] = (
                res[NOPE_DIM:QK_DIM] * cos_t + res[QK_DIM:] * sin_t).astype(BF16)
        pair_cols = slice(2 * hp * NOPE_DIM, 2 * (hp + 1) * NOPE_DIM)
        kn = _dot(ckv, wuk_ref[:, pair_cols]).astype(BF16)
        vt = _dot_nt(wuvt_ref[hp], ckv).astype(BF16)
        for hh in range(2):
            h = 2 * hp + hh
            k_ref[0, h, 0, :, 0:NOPE_DIM] = kn[:, hh * NOPE_DIM:(hh + 1) * NOPE_DIM]
            k_ref[0, h, 0, :, NOPE_DIM:QK_DIM] = krb
            vt_ref[0, h, 0, 0:V_DIM, :] = vt[hh * V_DIM:(hh + 1) * V_DIM]
            vt_ref[0, h, 0, V_DIM:V_AUG, :] = ones


def _qkv_proj(cq, cos_tt, sin_tt, wqt, ckv, kr, wuk, wuvt2, batch, seq, tm):
    n = cq.shape[0]
    tiles_per_seq = seq // tm
    row = lambda i: (i, 0)
    tab = lambda i: (0, i % tiles_per_seq)
    tile = lambda i: (i // tiles_per_seq, 0, i % tiles_per_seq, 0, 0)
    return pl.pallas_call(
        _qkv_proj_kernel,
        grid=(n // tm,),
        in_specs=[
            pl.BlockSpec((tm, Q_LORA), row),
            pl.BlockSpec((ROPE_DIM, tm), tab),
            pl.BlockSpec((ROPE_DIM, tm), tab),
            _resident((N_HEADS, 2 * NOPE_DIM, Q_LORA)),
            pl.BlockSpec((tm, KV_LORA), row),
            pl.BlockSpec((tm, ROPE_DIM), row),
            _resident((KV_LORA, N_HEADS * NOPE_DIM)),
            _resident((N_HEADS // 2, 2 * V_DIM, KV_LORA)),
        ],
        out_specs=[
            pl.BlockSpec((1, N_HEADS, QK_DIM, tm), lambda i: (i // tiles_per_seq, 0, 0, i % tiles_per_seq)),
            pl.BlockSpec((1, N_HEADS, 1, tm, QK_DIM), tile),
            pl.BlockSpec((1, N_HEADS, 1, V_AUG, tm), tile),
        ],
        out_shape=[
            jax.ShapeDtypeStruct((batch, N_HEADS, QK_DIM, seq), BF16),
            jax.ShapeDtypeStruct((batch, N_HEADS, tiles_per_seq, tm, QK_DIM), BF16),
            jax.ShapeDtypeStruct((batch, N_HEADS, tiles_per_seq, V_AUG, tm), BF16),
        ],
        compiler_params=_params(("arbitrary",)),
        name="qkv_proj",
    )(cq, cos_tt, sin_tt, wqt, ckv, kr, wuk, wuvt2)


def _flash_kernel(qt_ref, qtn_ref, k_ref, vt_ref, o_ref,
                  s_ref, sf_ref, acc_ref, m_ref, cmax_ref, cmaxf_ref, bias_ref, *, heads, tile):
    qi = pl.program_id(2)

    @pl.when((pl.program_id(0) == 0) & (pl.program_id(1) == 0) & (qi == 0))
    def _():
        key_chunk = lax.broadcasted_iota(jnp.int32, (tile, tile), 0) // CHUNK
        query_chunk = lax.broadcasted_iota(jnp.int32, (tile, tile), 1) // CHUNK
        bias_ref[...] = jnp.where(key_chunk <= query_chunk, 0.0, MASK_VALUE)

    def scores(g, j, slot):
        st = _dot(k_ref[0, g, j], qt_ref[0, g])
        s_ref[slot, g] = st
        cmax_ref[slot, g] = jnp.max(st, axis=0, keepdims=True)

    def next_first_scores(g):
        st = _dot(k_ref[0, g, 0], qtn_ref[0, g])
        sf_ref[g] = st
        cmaxf_ref[g] = jnp.max(st, axis=0, keepdims=True)

    def softmax(g, st, tile_max):
        m_old = m_ref[g]
        m_new = jnp.maximum(m_old, tile_max)
        m_ref[g] = m_new
        return jnp.exp2(st - m_new).astype(BF16), jnp.exp2(m_old - m_new)

    def values(g, j, p, alpha):
        acc_ref[g] = alpha * acc_ref[g] + _dot(vt_ref[0, g, j], p)

    def steps(j0, count, slot0, from_first=False):
        pending = None
        for t in range(count):
            cur = (slot0 + t) % 2
            for g in range(heads):
                if from_first and t == 0:
                    p, alpha = softmax(g, sf_ref[g], cmaxf_ref[g])
                else:
                    p, alpha = softmax(g, s_ref[cur, g], cmax_ref[cur, g])
                scores(g, j0 + t + 1, 1 - cur)
                if pending is not None:
                    values(*pending)
                pending = (g, j0 + t, p, alpha)
        values(*pending)

    for g in range(heads):
        m_ref[g] = jnp.full((1, tile), MASK_VALUE, F32)
        acc_ref[g] = jnp.zeros((V_AUG, tile), F32)

    @pl.when(qi == 0)
    def _():
        for g in range(heads):
            scores(g, 0, 0)

    @pl.when(qi >= 1)
    def _():
        steps(0, 1, 0, from_first=True)

    def two_steps(jj, carry):
        steps(1 + 2 * jj, 2, 1)
        return carry

    lax.fori_loop(0, jnp.maximum(qi - 1, 0) // 2, two_steps, 0)

    @pl.when((qi >= 2) & (qi % 2 == 0))
    def _():
        steps(qi - 1, 1, 1)

    def finish(g, p, alpha):
        acc = alpha * acc_ref[g] + _dot(vt_ref[0, g, qi], p)
        o = acc[0:V_DIM] / acc[V_DIM:V_DIM + 1]
        o_ref[:, g * V_DIM:(g + 1) * V_DIM] = o.T.astype(BF16)

    diag_slot = qi % 2
    pending = None
    for g in range(heads):
        st = s_ref[diag_slot, g] + bias_ref[...]
        p, alpha = softmax(g, st, jnp.max(st, axis=0, keepdims=True))
        next_first_scores(g)
        if pending is not None:
            finish(*pending)
        pending = (g, p, alpha)
    finish(*pending)


def _flash(qt, k, vt, heads=4):
    batch, _, _, seq = qt.shape
    n_tiles, tile = k.shape[2], k.shape[3]
    once = pl.Buffered(1)
    return pl.pallas_call(
        functools.partial(_flash_kernel, heads=heads, tile=tile),
        grid=(batch, N_HEADS // heads, n_tiles),
        in_specs=[
            pl.BlockSpec((1, heads, QK_DIM, tile), lambda b, hg, qi: (b, hg, 0, qi)),
            pl.BlockSpec((1, heads, QK_DIM, tile),
                         lambda b, hg, qi: (b, hg, 0, jnp.minimum(qi + 1, n_tiles - 1))),
            pl.BlockSpec((1, heads, n_tiles, tile, QK_DIM), lambda b, hg, qi: (b, hg, 0, 0, 0),
                         pipeline_mode=once),
            pl.BlockSpec((1, heads, n_tiles, V_AUG, tile), lambda b, hg, qi: (b, hg, 0, 0, 0),
                         pipeline_mode=once),
        ],
        out_specs=pl.BlockSpec((tile, heads * V_DIM), lambda b, hg, qi: (b * n_tiles + qi, hg)),
        out_shape=jax.ShapeDtypeStruct((batch * seq, N_HEADS * V_DIM), BF16),
        scratch_shapes=[
            pltpu.VMEM((2, heads, tile, tile), F32),
            pltpu.VMEM((heads, tile, tile), F32),
            pltpu.VMEM((heads, V_AUG, tile), F32),
            pltpu.VMEM((heads, 1, tile), F32),
            pltpu.VMEM((2, heads, 1, tile), F32),
            pltpu.VMEM((heads, 1, tile), F32),
            pltpu.VMEM((tile, tile), F32),
        ],
        compiler_params=_params(("arbitrary", "arbitrary", "arbitrary")),
        name="flash",
    )(qt, qt, k, vt)


def _decode_kernel(q_ref, wuk_ref, wuv_ref, cc_ref, ckr_ref, cn_ref, krn_ref, o_ref,
                   qt, s_ref, m_s, l_s, acc_s, *, seq, halves):
    kt = pl.program_id(1)
    half = cc_ref.shape[1] // halves

    @pl.when(kt == 0)
    def _():
        eye = (lax.broadcasted_iota(jnp.int32, (ROPE_DIM, ROPE_DIM), 0)
               == lax.broadcasted_iota(jnp.int32, (ROPE_DIM, ROPE_DIM), 1)).astype(BF16)
        for hp in range(N_HEADS // 2):
            lat, rope_t = [], []
            for h in (2 * hp, 2 * hp + 1):
                w_h = wuk_ref[:, h * NOPE_DIM:(h + 1) * NOPE_DIM]
                lat.append(_dot_nt(w_h, q_ref[0, h, :, 0:NOPE_DIM]))
                rope_t.append(_dot_nt(eye, q_ref[0, h, :, NOPE_DIM:QK_DIM]))
            cols = slice(2 * hp * seq, 2 * (hp + 1) * seq)
            qt[0:KV_LORA, cols] = jnp.concatenate(lat, axis=1).astype(BF16)
            qt[KV_LORA:, cols] = jnp.concatenate(rope_t, axis=1).astype(BF16)
        m_s[...] = jnp.full(m_s.shape, MASK_VALUE, F32)
        l_s[...] = jnp.zeros(l_s.shape, F32)
        acc_s[...] = jnp.zeros(acc_s.shape, F32)

    def scores(kc, kr, slot):
        n = kc.shape[0]
        s_ref[slot, 0:n, :] = _dot(kc, qt[0:KV_LORA, :]) + _dot(kr, qt[KV_LORA:, :])

    def update(kc, slot):
        n = kc.shape[0]
        st = s_ref[slot, 0:n, :]
        m_old = m_s[...]
        m_new = jnp.maximum(m_old, jnp.max(st, axis=0, keepdims=True))
        alpha = jnp.exp2(m_old - m_new)
        p = jnp.exp2(st - m_new)
        l_s[...] = alpha * l_s[...] + jnp.sum(p, axis=0, keepdims=True)
        acc_s[...] = alpha * acc_s[...] + _dot_tn(kc, p.astype(BF16))
        m_s[...] = m_new

    parts = [cc_ref[0, i * half:(i + 1) * half, :].astype(BF16) for i in range(halves)]
    for i in range(halves):
        scores(parts[i], ckr_ref[0, i * half:(i + 1) * half, :].astype(BF16), i)
    for i in range(halves):
        update(parts[i], i)

    @pl.when(kt == pl.num_programs(1) - 1)
    def _():
        new = cn_ref[...].astype(BF16)
        scores(new, krn_ref[...].astype(BF16), 0)
        update(new, 0)
        o_lat = (acc_s[...] / l_s[...]).T.astype(BF16)
        for h in range(N_HEADS):
            o_h = _dot(o_lat[h * seq:(h + 1) * seq], wuv_ref[:, h * V_DIM:(h + 1) * V_DIM])
            o_ref[:, h * V_DIM:(h + 1) * V_DIM] = o_h.astype(BF16)


def _decode_attn(q, wuk, wuv, cache_ckv, cache_kr, ckv_new, kr_new, key_tile=1024, halves=2):
    batch, _, seq, _ = q.shape
    past = cache_ckv.shape[1]
    cols = N_HEADS * seq
    return pl.pallas_call(
        functools.partial(_decode_kernel, seq=seq, halves=halves),
        grid=(batch, past // key_tile),
        in_specs=[
            pl.BlockSpec((1, N_HEADS, seq, QK_DIM), lambda b, kt: (b, 0, 0, 0)),
            _resident((KV_LORA, N_HEADS * NOPE_DIM)),
            _resident((KV_LORA, N_HEADS * V_DIM)),
            pl.BlockSpec((1, key_tile, KV_LORA), lambda b, kt: (b, kt, 0)),
            pl.BlockSpec((1, key_tile, ROPE_DIM), lambda b, kt: (b, kt, 0)),
            pl.BlockSpec((seq, KV_LORA), lambda b, kt: (b, 0)),
            pl.BlockSpec((seq, ROPE_DIM), lambda b, kt: (b, 0)),
        ],
        out_specs=pl.BlockSpec((seq, N_HEADS * V_DIM), lambda b, kt: (b, 0)),
        out_shape=jax.ShapeDtypeStruct((batch * seq, N_HEADS * V_DIM), BF16),
        scratch_shapes=[
            pltpu.VMEM((KV_LORA + ROPE_DIM, cols), BF16),
            pltpu.VMEM((halves, key_tile // halves, cols), F32),
            pltpu.VMEM((1, cols), F32),
            pltpu.VMEM((1, cols), F32),
            pltpu.VMEM((KV_LORA, cols), F32),
        ],
        compiler_params=_params(("arbitrary", "arbitrary")),
        name="decode_attn",
    )(q, wuk, wuv, cache_ckv, cache_kr, ckv_new, kr_new)


def _window_sum(ext, tmp, cols, w):
    end = ext.shape[0]
    levels = w.bit_length() - 1
    for k in range(levels):
        lo, shift = 8 * (k + 1), 2 ** k
        if k == 0:
            val = ext[lo:end, cols] + ext[lo - shift:end - shift, cols]
        else:
            below = tmp.at[(k - 1) % 2]
            val = below[lo:end, :] + below[lo - shift:end - shift, :]
        if k < levels - 1:
            tmp[k % 2, lo:end, :] = val
    return val[POOL_HALO - 8 * levels:]


def _merge_kernel(z_ref, zprev_ref, pre_ref, att_ref, gate_ref, x_ref, pw_ref, ps_ref,
                  wpo_ref, wmo_ref, wo_ref, h_ref, ext, tmp, ypool, *, n_seg, seg_len, tiles_per_seq, pos0):
    t = pl.program_id(0) % tiles_per_seq
    row = lax.broadcasted_iota(jnp.int32, (seg_len, 1), 0)
    pos = pos0 + t * seg_len + row
    br_b = _dot(att_ref[...], wmo_ref[...])
    for s in range(n_seg):
        if tiles_per_seq == 1:
            halo = pre_ref[s]
        else:
            halo = jnp.where(t == 0, pre_ref[0], zprev_ref[...])
        ext[0:POOL_HALO, :] = halo
        ext[POOL_HALO:, :] = z_ref[s * seg_len:(s + 1) * seg_len, :]
        for g, w in enumerate(POOL_WINDOWS):
            cols = slice(g * POOL_GROUP, (g + 1) * POOL_GROUP)
            cur = ext[POOL_HALO:, cols]
            total = _window_sum(ext, tmp, cols, w)
            count = jnp.minimum(pos + 1, w).astype(F32)
            diff = total / count - cur
            y = _dot(diff.astype(BF16), pw_ref[g]) * ps_ref[:, cols]
            ypool[s * seg_len:(s + 1) * seg_len, cols] = y.astype(BF16)
    br_a = _dot(ypool[...], wpo_ref[...])
    merged = gate_ref[:, 0:D_MODEL] * br_a + gate_ref[:, D_MODEL:] * br_b
    h_ref[...] = x_ref[...] + _dot(merged.astype(BF16), wo_ref[...])


def _merge(z, prefix, att, gates, x, pool_w, pool_scale, w_pool_out, w_mla_out, w_out, seq, pos0, tm):
    n = x.shape[0]
    tiles_per_seq = max(seq // tm, 1)
    n_seg = max(tm // seq, 1)
    seg_len = tm // n_seg
    halo_blocks = tm // POOL_HALO
    row = lambda i: (i, 0)
    return pl.pallas_call(
        functools.partial(_merge_kernel, n_seg=n_seg, seg_len=seg_len,
                          tiles_per_seq=tiles_per_seq, pos0=pos0),
        grid=(n // tm,),
        in_specs=[
            pl.BlockSpec((tm, D_POOL), row),
            pl.BlockSpec((POOL_HALO, D_POOL), lambda i: (jnp.maximum(i * halo_blocks - 1, 0), 0)),
            pl.BlockSpec((n_seg, POOL_HALO, D_POOL), lambda i: (i // tiles_per_seq, 0, 0)),
            pl.BlockSpec((tm, N_HEADS * V_DIM), row),
            pl.BlockSpec((tm, 2 * D_MODEL), row),
            pl.BlockSpec((tm, D_MODEL), row),
            _resident((len(POOL_WINDOWS), POOL_GROUP, POOL_GROUP)),
            _resident((1, D_POOL)),
            _resident((D_POOL, D_MODEL)),
            _resident((N_HEADS * V_DIM, D_MODEL)),
            _resident((D_MODEL, D_MODEL)),
        ],
        out_specs=pl.BlockSpec((tm, D_MODEL), row),
        out_shape=jax.ShapeDtypeStruct((n, D_MODEL), F32),
        scratch_shapes=[
            pltpu.VMEM((POOL_HALO + seg_len, D_POOL), F32),
            pltpu.VMEM((2, POOL_HALO + seg_len, POOL_GROUP), F32),
            pltpu.VMEM((tm, D_POOL), BF16),
        ],
        compiler_params=_params(("arbitrary",)),
        name="merge",
    )(z, z, prefix, att, gates, x, pool_w, pool_scale, w_pool_out, w_mla_out, w_out)


def _ffn_kernel(h_ref, g2_ref, wa_ref, wb_ref, cw_ref, cb_ref, pre_ref, wd_ref, fg_ref,
                y_ref, tail_ref, hn, ext, carry, *, n_seg, seg_len, tiles_per_seq):
    i = pl.program_id(0)
    j = pl.program_id(1)
    t = i % tiles_per_seq

    @pl.when(j == 0)
    def _():
        hn[...] = _rms(h_ref[...], g2_ref[...]).astype(BF16)
        y_ref[...] = jnp.zeros(y_ref.shape, F32)

    if tiles_per_seq > 1:
        @pl.when(t == 0)
        def _():
            carry[j] = pre_ref[0]

    tm = hn.shape[0]
    rc = tm // FFN_ROW_CHUNKS
    up = []
    for c in range(FFN_ROW_CHUNKS):
        hc = hn[c * rc:(c + 1) * rc, :]
        up.append((_dot(hc, wa_ref[...]), _dot(hc, wb_ref[...])))

    def conv_gelu(seg, lo, a_rows, b_rows):
        n = a_rows.shape[0]
        base = CONV_HALO + lo
        ext[seg, base:base + n, :] = a_rows
        c = cb_ref[...] + ext[seg, base - 2:base - 2 + n, :] * cw_ref[0:1, :]
        c = c + ext[seg, base - 1:base - 1 + n, :] * cw_ref[1:2, :]
        c = c + a_rows * cw_ref[2:3, :]
        gelu = 0.5 * c * (1.0 + lax.erf(c * (2.0 ** -0.5)))
        return (gelu * b_rows).astype(BF16)

    for c in range(FFN_ROW_CHUNKS):
        a_c, b_c = up[c]
        if n_seg == 1:
            if c == 0:
                ext[0, CONV_HALO - (CONV_W - 1):CONV_HALO, :] = carry[j] if tiles_per_seq > 1 else pre_ref[0]
            gated_c = conv_gelu(0, c * rc, a_c, b_c)
            if c == FFN_ROW_CHUNKS - 1:
                last = a_c[rc - (CONV_W - 1):]
                tail_ref[0, 0] = last
                if tiles_per_seq > 1:
                    carry[j] = last
        else:
            per_chunk = n_seg // FFN_ROW_CHUNKS
            parts = []
            for k in range(per_chunk):
                s = c * per_chunk + k
                rows = slice(k * seg_len, (k + 1) * seg_len)
                ext[s, CONV_HALO - (CONV_W - 1):CONV_HALO, :] = pre_ref[s]
                parts.append(conv_gelu(s, 0, a_c[rows], b_c[rows]))
                tail_ref[0, s] = a_c[rows][seg_len - (CONV_W - 1):]
            gated_c = jnp.concatenate(parts, axis=0)
        y_ref[c * rc:(c + 1) * rc, :] += _dot(gated_c, wd_ref[...])

    @pl.when(j == pl.num_programs(1) - 1)
    def _():
        y_ref[...] = _rms(h_ref[...] + y_ref[...], fg_ref[...])


def _ffn(h, norm2_g, w_up, conv_w, conv_b, prefix, w_down, final_g, seq, tm, tf=1024):
    n = h.shape[0]
    tiles_per_seq = max(seq // tm, 1)
    n_seg = max(tm // seq, 1)
    seg_len = tm // n_seg
    nff = D_FF // tf
    return pl.pallas_call(
        functools.partial(_ffn_kernel, n_seg=n_seg, seg_len=seg_len, tiles_per_seq=tiles_per_seq),
        grid=(n // tm, nff),
        in_specs=[
            pl.BlockSpec((tm, D_MODEL), lambda i, j: (i, 0)),
            _resident((1, D_MODEL)),
            pl.BlockSpec((D_MODEL, tf), lambda i, j: (0, j)),
            pl.BlockSpec((D_MODEL, tf), lambda i, j: (0, nff + j)),
            pl.BlockSpec((CONV_W, tf), lambda i, j: (0, j)),
            pl.BlockSpec((1, tf), lambda i, j: (0, j)),
            pl.BlockSpec((n_seg, CONV_W - 1, tf), lambda i, j: (i // tiles_per_seq, 0, j)),
            pl.BlockSpec((tf, D_MODEL), lambda i, j: (j, 0)),
            _resident((1, D_MODEL)),
        ],
        out_specs=[
            pl.BlockSpec((tm, D_MODEL), lambda i, j: (i, 0)),
            pl.BlockSpec((1, n_seg, CONV_W - 1, tf), lambda i, j: (i, 0, 0, j)),
        ],
        out_shape=[
            jax.ShapeDtypeStruct((n, D_MODEL), F32),
            jax.ShapeDtypeStruct((n // tm, n_seg, CONV_W - 1, D_FF), F32),
        ],
        scratch_shapes=[
            pltpu.VMEM((tm, D_MODEL), BF16),
            pltpu.VMEM((n_seg, CONV_HALO + seg_len, tf), F32),
            pltpu.VMEM((nff, CONV_W - 1, tf), F32),
        ],
        compiler_params=_params(("arbitrary", "arbitrary")),
        name="ffn",
    )(h, norm2_g, w_up, w_up, conv_w, conv_b, prefix, w_down, final_g)


def _split_w_in_kernel(w_ref, main_ref, gate_ref, rope_ref):
    w = w_ref[...]
    half = ROPE_DIM // 2
    main_ref[...] = w[:, 0:COL_GATE].astype(BF16)
    gate_ref[...] = w[:, COL_GATE + ROPE_DIM:].astype(BF16)
    rope_ref[...] = jnp.concatenate(
        [w[:, COL_GATE:COL_GATE + ROPE_DIM], w[:, COL_GATE + half:COL_GATE + ROPE_DIM],
         w[:, COL_GATE:COL_GATE + half]], axis=1).astype(BF16)


def _split_w_in(w, rows=256):
    k, d_in = w.shape
    return pl.pallas_call(
        _split_w_in_kernel,
        grid=(k // rows,),
        in_specs=[pl.BlockSpec((rows, d_in), lambda i: (i, 0))],
        out_specs=[
            pl.BlockSpec((rows, COL_GATE), lambda i: (i, 0)),
            pl.BlockSpec((rows, 2 * D_MODEL), lambda i: (i, 0)),
            pl.BlockSpec((rows, 2 * ROPE_DIM), lambda i: (i, 0)),
        ],
        out_shape=[
            jax.ShapeDtypeStruct((k, COL_GATE), BF16),
            jax.ShapeDtypeStruct((k, 2 * D_MODEL), BF16),
            jax.ShapeDtypeStruct((k, 2 * ROPE_DIM), BF16),
        ],
        compiler_params=_params(("arbitrary",)),
        name="split_w_in",
    )(w)


def _rope_tables(pos0, length, rows):
    pos = (pos0 + np.arange(length)).astype(np.float64)
    inv = ROPE_BASE ** (-(np.arange(ROPE_DIM // 2, dtype=np.float64) * 2.0 / ROPE_DIM))
    ang = pos[:, None] * inv[None, :]
    cos, sin = np.cos(ang), np.sin(ang)
    cos_t = np.concatenate([cos, cos], axis=-1).astype(np.float32)
    sin_t = np.concatenate([-sin, sin], axis=-1).astype(np.float32)
    reps = max(rows // length, 1)
    return np.tile(cos_t, (reps, 1)), np.tile(sin_t, (reps, 1))


def _swap_halves(w):
    return jnp.concatenate([w[..., ROPE_DIM // 2:], w[..., :ROPE_DIM // 2]], axis=-1)


def kernel(x_prompt, x_sample, cache_ckv, cache_krope, state_pool, state_conv, norm1_g, w_in, pool_w, pool_scale, w_pool_out, q_norm_g, w_uq, kv_norm_g, w_uk, w_uv, w_mla_out, w_out, norm2_g, w_up, conv_w, conv_b, w_down, final_g):
    l = 0
    bp, sp, _ = x_prompt.shape
    bs, ss, _ = x_sample.shape
    past = cache_ckv.shape[2]

    w = w_in[l]
    w_in_parts = _split_w_in(w)
    wq = w_uq[l].reshape(Q_LORA, N_HEADS, QK_DIM)
    wq = jnp.concatenate([wq, _swap_halves(wq[..., NOPE_DIM:])], axis=-1)
    wqt = wq.transpose(1, 2, 0).astype(BF16)
    wuk = w_uk[l].astype(BF16).reshape(KV_LORA, N_HEADS * NOPE_DIM)
    wuv = w_uv[l].astype(BF16).reshape(KV_LORA, N_HEADS * V_DIM)
    wuvt2 = wuv.reshape(KV_LORA, N_HEADS // 2, 2 * V_DIM).transpose(1, 2, 0)
    pw = pool_w[l].astype(BF16)
    wpo = w_pool_out[l].astype(BF16)
    wmo = w_mla_out[l].astype(BF16)
    wo = w_out[l].astype(BF16)
    wup = w_up[l].astype(BF16)
    wdn = w_down[l].astype(BF16)
    g1 = norm1_g[l][None]
    qg = q_norm_g[l][None]
    kvg = kv_norm_g[l][None]
    g2 = norm2_g[l][None]
    fg = final_g[None]
    ps = pool_scale[l][None]
    cb = conv_b[l][None]
    cw = conv_w[l]

    tm_in, tm_head, tm_merge, tm_ffn = 256, 512, 256, 512
    xp = x_prompt.reshape(bp * sp, D_MODEL)
    cos_p, sin_p = _rope_tables(0, sp, sp)
    z_p, cq_p, ckv_p, kr_p, gates_p = _in_proj(xp, g1, *w_in_parts, qg, kvg, cos_p, sin_p, tm_in)
    qt_p, k_p, vt_p = _qkv_proj(cq_p, cos_p.T, sin_p.T, wqt, ckv_p, kr_p, wuk, wuvt2, bp, sp, tm_head)
    att_p = _flash(qt_p, k_p, vt_p)
    pool0 = jnp.zeros((bp, POOL_HALO, D_POOL), F32)
    h_p = _merge(z_p, pool0, att_p, gates_p, xp, pw, ps, wpo, wmo, wo, sp, 0, tm_merge)
    conv0 = jnp.zeros((bp, CONV_W - 1, D_FF), F32)
    y_p, tail_p = _ffn(h_p, g2, wup, cw, cb, conv0, wdn, fg, sp, tm_ffn)

    ns = bs * ss
    xs = x_sample.reshape(ns, D_MODEL)
    cos_s, sin_s = _rope_tables(past, ss, ns)
    z_s, cq_s, ckv_s, kr_s, gates_s = _in_proj(xs, g1, *w_in_parts, qg, kvg, cos_s, sin_s, 256)
    q_s = _q_proj(cq_s, cos_s, sin_s, wqt, bs, ss, ns)
    att_s = _decode_attn(q_s, wuk, wuv, cache_ckv[l], cache_krope[l], ckv_s, kr_s)
    pool_pre = jnp.pad(state_pool[l], ((0, 0), (POOL_HALO - POOL_STATE, 0), (0, 0)))
    h_s = _merge(z_s, pool_pre, att_s, gates_s, xs, pw, ps, wpo, wmo, wo, ss, past, 256)
    y_s, tail_s = _ffn(h_s, g2, wup, cw, cb, state_conv[l], wdn, fg, ss, ns)

    tiles_per_seq = sp // tm_ffn
    p_conv = tail_p.reshape(bp, tiles_per_seq, CONV_W - 1, D_FF)[:, -1]
    s_conv = tail_s.reshape(bs, CONV_W - 1, D_FF)
    return (
        y_p.reshape(bp, sp, D_MODEL),
        y_s.reshape(bs, ss, D_MODEL),
        ckv_p.reshape(1, bp, sp, KV_LORA),
        kr_p.reshape(1, bp, sp, ROPE_DIM),
        z_p.reshape(bp, sp, D_POOL)[:, sp - POOL_STATE:][None],
        p_conv[None],
        ckv_s.reshape(1, bs, ss, KV_LORA),
        kr_s.reshape(1, bs, ss, ROPE_DIM),
        z_s.reshape(bs, ss, D_POOL)[:, ss - POOL_STATE:][None],
        s_conv[None],
    )
```

```python
import functools

import jax
import jax.numpy as jnp
import numpy as np
from jax import lax
from jax.experimental import pallas as pl
from jax.experimental.pallas import tpu as pltpu

D_MODEL = 2048
CHUNK = 64
D_POOL = 1024
POOL_WINDOWS = (2, 4, 8, 16)
POOL_GROUP = D_POOL // len(POOL_WINDOWS)
POOL_STATE = max(POOL_WINDOWS) - 1
POOL_HALO = 8 * (max(POOL_WINDOWS).bit_length() - 1)
N_HEADS = 16
Q_LORA = 512
KV_LORA = 512
NOPE_DIM = 128
ROPE_DIM = 64
V_DIM = 128
V_AUG = V_DIM + 16
QK_DIM = NOPE_DIM + ROPE_DIM
ATTN_SCALE = QK_DIM ** -0.5
LOG2E = 1.4426950408889634
Q_SCALE = ATTN_SCALE * LOG2E
ROPE_BASE = 10000.0
D_FF = 6144
CONV_W = 3
CONV_HALO = 8
FFN_ROW_CHUNKS = 2
EPS = 1e-6
MASK_VALUE = -1e30

COL_Q = D_POOL
COL_KV = COL_Q + Q_LORA
COL_GATE = COL_KV + KV_LORA

V7X_VMEM_LIMIT = 60000 * 1024

F32 = jnp.float32
BF16 = jnp.bfloat16


def _rms(x, g):
    return x * lax.rsqrt(jnp.mean(x * x, axis=-1, keepdims=True) + EPS) * g


def _dot(a, b):
    return jnp.dot(a, b, preferred_element_type=F32)


def _dot_nt(a, b):
    return lax.dot_general(a, b, (((1,), (1,)), ((), ())), preferred_element_type=F32)


def _dot_tn(a, b):
    return lax.dot_general(a, b, (((0,), (0,)), ((), ())), preferred_element_type=F32)


def _resident(shape):
    zeros = (0,) * len(shape)
    return pl.BlockSpec(shape, lambda *_: zeros, pipeline_mode=pl.Buffered(1))


def _params(semantics):
    return pltpu.CompilerParams(dimension_semantics=semantics, vmem_limit_bytes=V7X_VMEM_LIMIT)


def _in_proj_kernel(x_ref, g1_ref, w_ref, wg_ref, wr_ref, qg_ref, kvg_ref, cos_ref, sin_ref,
                    cost_ref, sint_ref, z_ref, cq_ref, ckv_ref, kr_ref, krt_ref, gate_ref, *, n_seg, seg_len):
    u = _rms(x_ref[...], g1_ref[...]).astype(BF16)
    z_ref[...] = _dot_nt(u, w_ref[0:COL_Q, :])
    cq_ref[...] = _rms(_dot_nt(u, w_ref[COL_Q:COL_KV, :]), qg_ref[...]).astype(BF16)
    ckv_ref[...] = _rms(_dot_nt(u, w_ref[COL_KV:COL_GATE, :]), kvg_ref[...])
    r = _dot_nt(u, wr_ref[...])
    kr_ref[...] = (r[:, :ROPE_DIM] * cos_ref[...] + r[:, ROPE_DIM:] * sin_ref[...]).astype(BF16)
    rt = _dot_nt(wr_ref[...], u)
    krt = rt[:ROPE_DIM] * cost_ref[...] + rt[ROPE_DIM:] * sint_ref[...]
    for s in range(n_seg):
        krt_ref[s] = krt[:, s * seg_len:(s + 1) * seg_len]
    step = 1024
    for lo in range(0, 2 * D_MODEL, step):
        logits = _dot_nt(u, wg_ref[lo:lo + step, :])
        gate_ref[:, lo:lo + step] = 1.0 / (1.0 + jnp.exp(-logits))


def _in_proj(x, g1, w_main, w_gate, w_rope, qg, kvg, cos_t, sin_t, batch, seq, tm):
    n = x.shape[0]
    table_blocks = cos_t.shape[0] // tm
    tiles_per_seq = max(seq // tm, 1)
    n_seg = max(tm // seq, 1)
    row = lambda i: (i, 0)
    tab = lambda i: (i % table_blocks, 0)
    tab_t = lambda i: (0, i % table_blocks)
    return pl.pallas_call(
        functools.partial(_in_proj_kernel, n_seg=n_seg, seg_len=tm // n_seg),
        grid=(n // tm,),
        in_specs=[
            pl.BlockSpec((tm, D_MODEL), row),
            _resident((1, D_MODEL)),
            _resident((COL_GATE, D_MODEL)),
            _resident((2 * D_MODEL, D_MODEL)),
            _resident((2 * ROPE_DIM, D_MODEL)),
            _resident((1, Q_LORA)),
            _resident((1, KV_LORA)),
            pl.BlockSpec((tm, ROPE_DIM), tab),
            pl.BlockSpec((tm, ROPE_DIM), tab),
            pl.BlockSpec((ROPE_DIM, tm), tab_t),
            pl.BlockSpec((ROPE_DIM, tm), tab_t),
        ],
        out_specs=[
            pl.BlockSpec((tm, D_POOL), row),
            pl.BlockSpec((tm, Q_LORA), row),
            pl.BlockSpec((tm, KV_LORA), row),
            pl.BlockSpec((tm, ROPE_DIM), row),
            pl.BlockSpec((n_seg, ROPE_DIM, tm // n_seg), lambda i: (i // tiles_per_seq, 0, i % tiles_per_seq)),
            pl.BlockSpec((tm, 2 * D_MODEL), row),
        ],
        out_shape=[
            jax.ShapeDtypeStruct((n, D_POOL), F32),
            jax.ShapeDtypeStruct((n, Q_LORA), BF16),
            jax.ShapeDtypeStruct((n, KV_LORA), F32),
            jax.ShapeDtypeStruct((n, ROPE_DIM), BF16),
            jax.ShapeDtypeStruct((batch, ROPE_DIM, seq), F32),
            jax.ShapeDtypeStruct((n, 2 * D_MODEL), F32),
        ],
        compiler_params=_params(("arbitrary",)),
        name="in_proj",
    )(x, g1, w_main, w_gate, w_rope, qg, kvg, cos_t, sin_t, cos_t.T, sin_t.T)


def _q_proj_kernel(cq_ref, cos_ref, sin_ref, wqt_ref, q_ref, *, n_seg, seg_len):
    cq = cq_ref[...]
    cos = cos_ref[...] * Q_SCALE
    sin = sin_ref[...] * Q_SCALE

    def head(h, carry):
        res = _dot_nt(cq, wqt_ref[h])
        qn = (res[:, :NOPE_DIM] * Q_SCALE).astype(BF16)
        qr = (res[:, NOPE_DIM:QK_DIM] * cos + res[:, QK_DIM:] * sin).astype(BF16)
        for s in range(n_seg):
            rows = slice(s * seg_len, (s + 1) * seg_len)
            q_ref[s, h, :, 0:NOPE_DIM] = qn[rows]
            q_ref[s, h, :, NOPE_DIM:QK_DIM] = qr[rows]
        return carry

    lax.fori_loop(0, N_HEADS, head, 0)


def _q_proj(cq, cos_t, sin_t, wqt, batch, seq, tm):
    n = cq.shape[0]
    n_seg = tm // seq
    table_blocks = cos_t.shape[0] // tm
    return pl.pallas_call(
        functools.partial(_q_proj_kernel, n_seg=n_seg, seg_len=seq),
        grid=(n // tm,),
        in_specs=[
            pl.BlockSpec((tm, Q_LORA), lambda i: (i, 0)),
            pl.BlockSpec((tm, ROPE_DIM), lambda i: (i % table_blocks, 0)),
            pl.BlockSpec((tm, ROPE_DIM), lambda i: (i % table_blocks, 0)),
            _resident((N_HEADS, 2 * NOPE_DIM, Q_LORA)),
        ],
        out_specs=pl.BlockSpec((n_seg, N_HEADS, seq, QK_DIM), lambda i: (i, 0, 0, 0)),
        out_shape=jax.ShapeDtypeStruct((batch, N_HEADS, seq, QK_DIM), BF16),
        compiler_params=_params(("arbitrary",)),
        name="q_proj",
    )(cq, cos_t, sin_t, wqt)


def _qkv_proj_kernel(cq_ref, cost_ref, sint_ref, wqt_ref, ckv_ref, kr_ref, wuk_ref, wuvt_ref,
                     qt_ref, k_ref, vt_ref):
    cq = cq_ref[...]
    cos_t = cost_ref[...] * Q_SCALE
    sin_t = sint_ref[...] * Q_SCALE
    ckv = ckv_ref[...].astype(BF16)
    krb = kr_ref[...].astype(BF16)
    ones = jnp.ones((V_AUG - V_DIM, ckv.shape[0]), BF16)

    for hp in range(N_HEADS // 2):
        for hh in range(2):
            h = 2 * hp + hh
            res = _dot_nt(wqt_ref[h], cq)
            qt_ref[0, h, 0:NOPE_DIM, :] = (res[0:NOPE_DIM] * Q_SCALE).astype(BF16)
            qt_ref[0, h, NOPE_DIM:QK_DIM, :] = (
                res[NOPE_DIM:QK_DIM] * cos_t + res[QK_DIM:] * sin_t).astype(BF16)
        pair_cols = slice(2 * hp * NOPE_DIM, 2 * (hp + 1) * NOPE_DIM)
        kn = _dot(ckv, wuk_ref[:, pair_cols]).astype(BF16)
        vt = _dot_nt(wuvt_ref[hp], ckv).astype(BF16)
        for hh in range(2):
            h = 2 * hp + hh
            k_ref[0, h, 0, :, 0:NOPE_DIM] = kn[:, hh * NOPE_DIM:(hh + 1) * NOPE_DIM]
            k_ref[0, h, 0, :, NOPE_DIM:QK_DIM] = krb
            vt_ref[0, h, 0, 0:V_DIM, :] = vt[hh * V_DIM:(hh + 1) * V_DIM]
            vt_ref[0, h, 0, V_DIM:V_AUG, :] = ones


def _qkv_proj(cq, cos_tt, sin_tt, wqt, ckv, kr, wuk, wuvt2, batch, seq, tm):
    n = cq.shape[0]
    tiles_per_seq = seq // tm
    row = lambda i: (i, 0)
    tab = lambda i: (0, i % tiles_per_seq)
    tile = lambda i: (i // tiles_per_seq, 0, i % tiles_per_seq, 0, 0)
    return pl.pallas_call(
        _qkv_proj_kernel,
        grid=(n // tm,),
        in_specs=[
            pl.BlockSpec((tm, Q_LORA), row),
            pl.BlockSpec((ROPE_DIM, tm), tab),
            pl.BlockSpec((ROPE_DIM, tm), tab),
            _resident((N_HEADS, 2 * NOPE_DIM, Q_LORA)),
            pl.BlockSpec((tm, KV_LORA), row),
            pl.BlockSpec((tm, ROPE_DIM), row),
            _resident((KV_LORA, N_HEADS * NOPE_DIM)),
            _resident((N_HEADS // 2, 2 * V_DIM, KV_LORA)),
        ],
        out_specs=[
            pl.BlockSpec((1, N_HEADS, QK_DIM, tm), lambda i: (i // tiles_per_seq, 0, 0, i % tiles_per_seq)),
            pl.BlockSpec((1, N_HEADS, 1, tm, QK_DIM), tile),
            pl.BlockSpec((1, N_HEADS, 1, V_AUG, tm), tile),
        ],
        out_shape=[
            jax.ShapeDtypeStruct((batch, N_HEADS, QK_DIM, seq), BF16),
            jax.ShapeDtypeStruct((batch, N_HEADS, tiles_per_seq, tm, QK_DIM), BF16),
            jax.ShapeDtypeStruct((batch, N_HEADS, tiles_per_seq, V_AUG, tm), BF16),
        ],
        compiler_params=_params(("arbitrary",)),
        name="qkv_proj",
    )(cq, cos_tt, sin_tt, wqt, ckv, kr, wuk, wuvt2)


def _flash_kernel(qt_ref, qtn_ref, knew_ref, vtnew_ref, o_ref,
                  k_ref, vt_ref, s_ref, sf_ref, acc_ref, m_ref, cmax_ref, cmaxf_ref, bias_ref, *, heads, tile):
    qi = pl.program_id(2)
    for g in range(heads):
        k_ref[g, qi] = knew_ref[0, g, 0]
        vt_ref[g, qi] = vtnew_ref[0, g, 0]

    @pl.when((pl.program_id(0) == 0) & (pl.program_id(1) == 0) & (qi == 0))
    def _():
        key_chunk = lax.broadcasted_iota(jnp.int32, (tile, tile), 0) // CHUNK
        query_chunk = lax.broadcasted_iota(jnp.int32, (tile, tile), 1) // CHUNK
        bias_ref[...] = jnp.where(key_chunk <= query_chunk, 0.0, MASK_VALUE)

    def scores(g, j, slot):
        st = _dot(k_ref[g, j], qt_ref[0, g])
        s_ref[slot, g] = st
        cmax_ref[slot, g] = jnp.max(st, axis=0, keepdims=True)

    def next_first_scores(g):
        st = _dot(k_ref[g, 0], qtn_ref[0, g])
        sf_ref[g] = st
        cmaxf_ref[g] = jnp.max(st, axis=0, keepdims=True)

    def softmax(g, st, tile_max):
        m_old = m_ref[g]
        m_new = jnp.maximum(m_old, tile_max)
        m_ref[g] = m_new
        return jnp.exp2(st - m_new).astype(BF16), jnp.exp2(m_old - m_new)

    def values(g, j, p, alpha):
        acc_ref[g] = alpha * acc_ref[g] + _dot(vt_ref[g, j], p)

    def steps(j0, count, slot0, from_first=False):
        pending = None
        for t in range(count):
            cur = (slot0 + t) % 2
            for g in range(heads):
                if from_first and t == 0:
                    p, alpha = softmax(g, sf_ref[g], cmaxf_ref[g])
                else:
                    p, alpha = softmax(g, s_ref[cur, g], cmax_ref[cur, g])
                scores(g, j0 + t + 1, 1 - cur)
                if pending is not None:
                    values(*pending)
                pending = (g, j0 + t, p, alpha)
        values(*pending)

    for g in range(heads):
        m_ref[g] = jnp.full((1, tile), MASK_VALUE, F32)
        acc_ref[g] = jnp.zeros((V_AUG, tile), F32)

    @pl.when(qi == 0)
    def _():
        for g in range(heads):
            scores(g, 0, 0)

    @pl.when(qi >= 1)
    def _():
        steps(0, 1, 0, from_first=True)

    def two_steps(jj, carry):
        steps(1 + 2 * jj, 2, 1)
        return carry

    lax.fori_loop(0, jnp.maximum(qi - 1, 0) // 2, two_steps, 0)

    @pl.when((qi >= 2) & (qi % 2 == 0))
    def _():
        steps(qi - 1, 1, 1)

    def finish(g, p, alpha):
        acc = alpha * acc_ref[g] + _dot(vt_ref[g, qi], p)
        o = acc[0:V_DIM] / acc[V_DIM:V_DIM + 1]
        o_ref[:, g * V_DIM:(g + 1) * V_DIM] = o.T.astype(BF16)

    diag_slot = qi % 2
    pending = None
    for g in range(heads):
        st = s_ref[diag_slot, g] + bias_ref[...]
        p, alpha = softmax(g, st, jnp.max(st, axis=0, keepdims=True))
        next_first_scores(g)
        if pending is not None:
            finish(*pending)
        pending = (g, p, alpha)
    finish(*pending)


def _flash(qt, k, vt, heads=4):
    batch, _, _, seq = qt.shape
    n_tiles, tile = k.shape[2], k.shape[3]
    return pl.pallas_call(
        functools.partial(_flash_kernel, heads=heads, tile=tile),
        grid=(batch, N_HEADS // heads, n_tiles),
        in_specs=[
            pl.BlockSpec((1, heads, QK_DIM, tile), lambda b, hg, qi: (b, hg, 0, qi)),
            pl.BlockSpec((1, heads, QK_DIM, tile),
                         lambda b, hg, qi: (b, hg, 0, jnp.minimum(qi + 1, n_tiles - 1))),
            pl.BlockSpec((1, heads, 1, tile, QK_DIM), lambda b, hg, qi: (b, hg, qi, 0, 0)),
            pl.BlockSpec((1, heads, 1, V_AUG, tile), lambda b, hg, qi: (b, hg, qi, 0, 0)),
        ],
        out_specs=pl.BlockSpec((tile, heads * V_DIM), lambda b, hg, qi: (b * n_tiles + qi, hg)),
        out_shape=jax.ShapeDtypeStruct((batch * seq, N_HEADS * V_DIM), BF16),
        scratch_shapes=[
            pltpu.VMEM((heads, n_tiles, tile, QK_DIM), BF16),
            pltpu.VMEM((heads, n_tiles, V_AUG, tile), BF16),
            pltpu.VMEM((2, heads, tile, tile), F32),
            pltpu.VMEM((heads, tile, tile), F32),
            pltpu.VMEM((heads, V_AUG, tile), F32),
            pltpu.VMEM((heads, 1, tile), F32),
            pltpu.VMEM((2, heads, 1, tile), F32),
            pltpu.VMEM((heads, 1, tile), F32),
            pltpu.VMEM((tile, tile), F32),
        ],
        compiler_params=_params(("arbitrary", "arbitrary", "arbitrary")),
        name="flash",
    )(qt, qt, k, vt)


def _decode_kernel(q_ref, wuk_ref, wuv_ref, cc_ref, ckr_ref, cn_ref, krn_ref, o_ref,
                   qt, s_ref, m_s, l_s, acc_s, *, seq, halves):
    kt = pl.program_id(1)
    half = cc_ref.shape[1] // halves

    @pl.when(kt == 0)
    def _():
        eye = (lax.broadcasted_iota(jnp.int32, (ROPE_DIM, ROPE_DIM), 0)
               == lax.broadcasted_iota(jnp.int32, (ROPE_DIM, ROPE_DIM), 1)).astype(BF16)
        for hp in range(N_HEADS // 2):
            lat, rope_t = [], []
            for h in (2 * hp, 2 * hp + 1):
                w_h = wuk_ref[:, h * NOPE_DIM:(h + 1) * NOPE_DIM]
                lat.append(_dot_nt(w_h, q_ref[0, h, :, 0:NOPE_DIM]))
                rope_t.append(_dot_nt(eye, q_ref[0, h, :, NOPE_DIM:QK_DIM]))
            cols = slice(2 * hp * seq, 2 * (hp + 1) * seq)
            qt[0:KV_LORA, cols] = jnp.concatenate(lat, axis=1).astype(BF16)
            qt[KV_LORA:, cols] = jnp.concatenate(rope_t, axis=1).astype(BF16)
        m_s[...] = jnp.full(m_s.shape, MASK_VALUE, F32)
        l_s[...] = jnp.zeros(l_s.shape, F32)
        acc_s[...] = jnp.zeros(acc_s.shape, F32)

    def scores(kc, krt, slot):
        n = kc.shape[0]
        s_ref[slot, 0:n, :] = _dot(kc, qt[0:KV_LORA, :]) + _dot_tn(krt, qt[KV_LORA:, :])

    def update(kc, slot):
        n = kc.shape[0]
        st = s_ref[slot, 0:n, :]
        m_old = m_s[...]
        m_new = jnp.maximum(m_old, jnp.max(st, axis=0, keepdims=True))
        alpha = jnp.exp2(m_old - m_new)
        p = jnp.exp2(st - m_new)
        l_s[...] = alpha * l_s[...] + jnp.sum(p, axis=0, keepdims=True)
        acc_s[...] = alpha * acc_s[...] + _dot_tn(kc, p.astype(BF16))
        m_s[...] = m_new

    parts = [cc_ref[0, i * half:(i + 1) * half, :].astype(BF16) for i in range(halves)]
    for i in range(halves):
        scores(parts[i], ckr_ref[0, :, i * half:(i + 1) * half].astype(BF16), i)
    for i in range(halves):
        update(parts[i], i)

    @pl.when(kt == pl.num_programs(1) - 1)
    def _():
        new = cn_ref[...].astype(BF16)
        scores(new, krn_ref[0].astype(BF16), 0)
        update(new, 0)
        o_lat = (acc_s[...] / l_s[...]).T.astype(BF16)
        for h in range(N_HEADS):
            o_h = _dot(o_lat[h * seq:(h + 1) * seq], wuv_ref[:, h * V_DIM:(h + 1) * V_DIM])
            o_ref[:, h * V_DIM:(h + 1) * V_DIM] = o_h.astype(BF16)


def _decode_attn(q, wuk, wuv, cache_ckv, cache_kr, ckv_new, kr_new, key_tile=1024, halves=2):
    batch, _, seq, _ = q.shape
    past = cache_ckv.shape[1]
    cols = N_HEADS * seq
    return pl.pallas_call(
        functools.partial(_decode_kernel, seq=seq, halves=halves),
        grid=(batch, past // key_tile),
        in_specs=[
            pl.BlockSpec((1, N_HEADS, seq, QK_DIM), lambda b, kt: (b, 0, 0, 0)),
            _resident((KV_LORA, N_HEADS * NOPE_DIM)),
            _resident((KV_LORA, N_HEADS * V_DIM)),
            pl.BlockSpec((1, key_tile, KV_LORA), lambda b, kt: (b, kt, 0)),
            pl.BlockSpec((1, ROPE_DIM, key_tile), lambda b, kt: (b, 0, kt)),
            pl.BlockSpec((seq, KV_LORA), lambda b, kt: (b, 0)),
            pl.BlockSpec((1, ROPE_DIM, seq), lambda b, kt: (b, 0, 0)),
        ],
        out_specs=pl.BlockSpec((seq, N_HEADS * V_DIM), lambda b, kt: (b, 0)),
        out_shape=jax.ShapeDtypeStruct((batch * seq, N_HEADS * V_DIM), BF16),
        scratch_shapes=[
            pltpu.VMEM((KV_LORA + ROPE_DIM, cols), BF16),
            pltpu.VMEM((halves, key_tile // halves, cols), F32),
            pltpu.VMEM((1, cols), F32),
            pltpu.VMEM((1, cols), F32),
            pltpu.VMEM((KV_LORA, cols), F32),
        ],
        compiler_params=_params(("arbitrary", "arbitrary")),
        name="decode_attn",
    )(q, wuk, wuv, cache_ckv, cache_kr, ckv_new, kr_new)


def _window_sum(ext, tmp, cols, w):
    end = ext.shape[0]
    levels = w.bit_length() - 1
    for k in range(levels):
        lo, shift = 8 * (k + 1), 2 ** k
        if k == 0:
            val = ext[lo:end, cols] + ext[lo - shift:end - shift, cols]
        else:
            below = tmp.at[(k - 1) % 2]
            val = below[lo:end, :] + below[lo - shift:end - shift, :]
        if k < levels - 1:
            tmp[k % 2, lo:end, :] = val
    return val[POOL_HALO - 8 * levels:]


def _merge_kernel(z_ref, zprev_ref, pre_ref, att_ref, gate_ref, x_ref, pw_ref, ps_ref,
                  wpo_ref, wmo_ref, wo_ref, h_ref, ext, tmp, ypool, *, n_seg, seg_len, tiles_per_seq, pos0):
    t = pl.program_id(0) % tiles_per_seq
    row = lax.broadcasted_iota(jnp.int32, (seg_len, 1), 0)
    pos = pos0 + t * seg_len + row
    br_b = _dot(att_ref[...], wmo_ref[...])
    for s in range(n_seg):
        if tiles_per_seq == 1:
            halo = pre_ref[s]
        else:
            halo = jnp.where(t == 0, pre_ref[0], zprev_ref[...])
        ext[0:POOL_HALO, :] = halo
        ext[POOL_HALO:, :] = z_ref[s * seg_len:(s + 1) * seg_len, :]
        for g, w in enumerate(POOL_WINDOWS):
            cols = slice(g * POOL_GROUP, (g + 1) * POOL_GROUP)
            cur = ext[POOL_HALO:, cols]
            total = _window_sum(ext, tmp, cols, w)
            count = jnp.minimum(pos + 1, w).astype(F32)
            diff = total / count - cur
            y = _dot(diff.astype(BF16), pw_ref[g]) * ps_ref[:, cols]
            ypool[s * seg_len:(s + 1) * seg_len, cols] = y.astype(BF16)
    br_a = _dot(ypool[...], wpo_ref[...])
    merged = gate_ref[:, 0:D_MODEL] * br_a + gate_ref[:, D_MODEL:] * br_b
    h_ref[...] = x_ref[...] + _dot(merged.astype(BF16), wo_ref[...])


def _merge(z, prefix, att, gates, x, pool_w, pool_scale, w_pool_out, w_mla_out, w_out, seq, pos0, tm):
    n = x.shape[0]
    tiles_per_seq = max(seq // tm, 1)
    n_seg = max(tm // seq, 1)
    seg_len = tm // n_seg
    halo_blocks = tm // POOL_HALO
    row = lambda i: (i, 0)
    return pl.pallas_call(
        functools.partial(_merge_kernel, n_seg=n_seg, seg_len=seg_len,
                          tiles_per_seq=tiles_per_seq, pos0=pos0),
        grid=(n // tm,),
        in_specs=[
            pl.BlockSpec((tm, D_POOL), row),
            pl.BlockSpec((POOL_HALO, D_POOL), lambda i: (jnp.maximum(i * halo_blocks - 1, 0), 0)),
            pl.BlockSpec((n_seg, POOL_HALO, D_POOL), lambda i: (i // tiles_per_seq, 0, 0)),
            pl.BlockSpec((tm, N_HEADS * V_DIM), row),
            pl.BlockSpec((tm, 2 * D_MODEL), row),
            pl.BlockSpec((tm, D_MODEL), row),
            _resident((len(POOL_WINDOWS), POOL_GROUP, POOL_GROUP)),
            _resident((1, D_POOL)),
            _resident((D_POOL, D_MODEL)),
            _resident((N_HEADS * V_DIM, D_MODEL)),
            _resident((D_MODEL, D_MODEL)),
        ],
        out_specs=pl.BlockSpec((tm, D_MODEL), row),
        out_shape=jax.ShapeDtypeStruct((n, D_MODEL), F32),
        scratch_shapes=[
            pltpu.VMEM((POOL_HALO + seg_len, D_POOL), F32),
            pltpu.VMEM((2, POOL_HALO + seg_len, POOL_GROUP), F32),
            pltpu.VMEM((tm, D_POOL), BF16),
        ],
        compiler_params=_params(("arbitrary",)),
        name="merge",
    )(z, z, prefix, att, gates, x, pool_w, pool_scale, w_pool_out, w_mla_out, w_out)


def _ffn_kernel(h_ref, g2_ref, wa_ref, wb_ref, cw_ref, cb_ref, pre_ref, wd_ref, fg_ref,
                y_ref, tail_ref, hn, ext, carry, *, n_seg, seg_len, tiles_per_seq):
    i = pl.program_id(0)
    j = pl.program_id(1)
    t = i % tiles_per_seq

    @pl.when(j == 0)
    def _():
        hn[...] = _rms(h_ref[...], g2_ref[...]).astype(BF16)
        y_ref[...] = jnp.zeros(y_ref.shape, F32)

    if tiles_per_seq > 1:
        @pl.when(t == 0)
        def _():
            carry[j] = pre_ref[0]

    tm = hn.shape[0]
    rc = tm // FFN_ROW_CHUNKS
    up = []
    for c in range(FFN_ROW_CHUNKS):
        hc = hn[c * rc:(c + 1) * rc, :]
        up.append((_dot(hc, wa_ref[...]), _dot(hc, wb_ref[...])))

    def conv_gelu(seg, lo, a_rows, b_rows):
        n = a_rows.shape[0]
        base = CONV_HALO + lo
        ext[seg, base:base + n, :] = a_rows
        c = cb_ref[...] + ext[seg, base - 2:base - 2 + n, :] * cw_ref[0:1, :]
        c = c + ext[seg, base - 1:base - 1 + n, :] * cw_ref[1:2, :]
        c = c + a_rows * cw_ref[2:3, :]
        gelu = 0.5 * c * (1.0 + lax.erf(c * (2.0 ** -0.5)))
        return (gelu * b_rows).astype(BF16)

    for c in range(FFN_ROW_CHUNKS):
        a_c, b_c = up[c]
        if n_seg == 1:
            if c == 0:
                ext[0, CONV_HALO - (CONV_W - 1):CONV_HALO, :] = carry[j] if tiles_per_seq > 1 else pre_ref[0]
            gated_c = conv_gelu(0, c * rc, a_c, b_c)
            if c == FFN_ROW_CHUNKS - 1:
                last = a_c[rc - (CONV_W - 1):]
                tail_ref[0, 0] = last
                if tiles_per_seq > 1:
                    carry[j] = last
        else:
            per_chunk = n_seg // FFN_ROW_CHUNKS
            parts = []
            for k in range(per_chunk):
                s = c * per_chunk + k
                rows = slice(k * seg_len, (k + 1) * seg_len)
                ext[s, CONV_HALO - (CONV_W - 1):CONV_HALO, :] = pre_ref[s]
                parts.append(conv_gelu(s, 0, a_c[rows], b_c[rows]))
                tail_ref[0, s] = a_c[rows][seg_len - (CONV_W - 1):]
            gated_c = jnp.concatenate(parts, axis=0)
        y_ref[c * rc:(c + 1) * rc, :] += _dot(gated_c, wd_ref[...])

    @pl.when(j == pl.num_programs(1) - 1)
    def _():
        y_ref[...] = _rms(h_ref[...] + y_ref[...], fg_ref[...])


def _ffn(h, norm2_g, w_up, conv_w, conv_b, prefix, w_down, final_g, seq, tm, tf=1024):
    n = h.shape[0]
    tiles_per_seq = max(seq // tm, 1)
    n_seg = max(tm // seq, 1)
    seg_len = tm // n_seg
    nff = D_FF // tf
    return pl.pallas_call(
        functools.partial(_ffn_kernel, n_seg=n_seg, seg_len=seg_len, tiles_per_seq=tiles_per_seq),
        grid=(n // tm, nff),
        in_specs=[
            pl.BlockSpec((tm, D_MODEL), lambda i, j: (i, 0)),
            _resident((1, D_MODEL)),
            pl.BlockSpec((D_MODEL, tf), lambda i, j: (0, j)),
            pl.BlockSpec((D_MODEL, tf), lambda i, j: (0, nff + j)),
            pl.BlockSpec((CONV_W, tf), lambda i, j: (0, j)),
            pl.BlockSpec((1, tf), lambda i, j: (0, j)),
            pl.BlockSpec((n_seg, CONV_W - 1, tf), lambda i, j: (i // tiles_per_seq, 0, j)),
            pl.BlockSpec((tf, D_MODEL), lambda i, j: (j, 0)),
            _resident((1, D_MODEL)),
        ],
        out_specs=[
            pl.BlockSpec((tm, D_MODEL), lambda i, j: (i, 0)),
            pl.BlockSpec((1, n_seg, CONV_W - 1, tf), lambda i, j: (i, 0, 0, j)),
        ],
        out_shape=[
            jax.ShapeDtypeStruct((n, D_MODEL), F32),
            jax.ShapeDtypeStruct((n // tm, n_seg, CONV_W - 1, D_FF), F32),
        ],
        scratch_shapes=[
            pltpu.VMEM((tm, D_MODEL), BF16),
            pltpu.VMEM((n_seg, CONV_HALO + seg_len, tf), F32),
            pltpu.VMEM((nff, CONV_W - 1, tf), F32),
        ],
        compiler_params=_params(("arbitrary", "arbitrary")),
        name="ffn",
    )(h, norm2_g, w_up, w_up, conv_w, conv_b, prefix, w_down, final_g)


def _rope_tables(pos0, length, rows):
    pos = (pos0 + np.arange(length)).astype(np.float64)
    inv = ROPE_BASE ** (-(np.arange(ROPE_DIM // 2, dtype=np.float64) * 2.0 / ROPE_DIM))
    ang = pos[:, None] * inv[None, :]
    cos, sin = np.cos(ang), np.sin(ang)
    cos_t = np.concatenate([cos, cos], axis=-1).astype(np.float32)
    sin_t = np.concatenate([-sin, sin], axis=-1).astype(np.float32)
    reps = max(rows // length, 1)
    return np.tile(cos_t, (reps, 1)), np.tile(sin_t, (reps, 1))


def _swap_halves(w):
    return jnp.concatenate([w[..., ROPE_DIM // 2:], w[..., :ROPE_DIM // 2]], axis=-1)


def kernel(x_prompt, x_sample, cache_ckv, cache_krope, state_pool, state_conv, norm1_g, w_in, pool_w, pool_scale, w_pool_out, q_norm_g, w_uq, kv_norm_g, w_uk, w_uv, w_mla_out, w_out, norm2_g, w_up, conv_w, conv_b, w_down, final_g):
    l = 0
    bp, sp, _ = x_prompt.shape
    bs, ss, _ = x_sample.shape
    past = cache_ckv.shape[2]

    w = w_in[l]
    wt = w.T
    half = ROPE_DIM // 2
    w_in_parts = (
        wt[:COL_GATE].astype(BF16),
        wt[COL_GATE + ROPE_DIM:].astype(BF16),
        jnp.concatenate([wt[COL_GATE:COL_GATE + ROPE_DIM], wt[COL_GATE + half:COL_GATE + ROPE_DIM],
                         wt[COL_GATE:COL_GATE + half]], axis=0).astype(BF16),
    )
    wq = w_uq[l].reshape(Q_LORA, N_HEADS, QK_DIM)
    wq = jnp.concatenate([wq, _swap_halves(wq[..., NOPE_DIM:])], axis=-1)
    wqt = wq.transpose(1, 2, 0).astype(BF16)
    wuk = w_uk[l].astype(BF16).reshape(KV_LORA, N_HEADS * NOPE_DIM)
    wuv = w_uv[l].astype(BF16).reshape(KV_LORA, N_HEADS * V_DIM)
    wuvt2 = wuv.reshape(KV_LORA, N_HEADS // 2, 2 * V_DIM).transpose(1, 2, 0)
    pw = pool_w[l].astype(BF16)
    wpo = w_pool_out[l].astype(BF16)
    wmo = w_mla_out[l].astype(BF16)
    wo = w_out[l].astype(BF16)
    wup = w_up[l].astype(BF16)
    wdn = w_down[l].astype(BF16)
    g1 = norm1_g[l][None]
    qg = q_norm_g[l][None]
    kvg = kv_norm_g[l][None]
    g2 = norm2_g[l][None]
    fg = final_g[None]
    ps = pool_scale[l][None]
    cb = conv_b[l][None]
    cw = conv_w[l]

    tm_in, tm_head, tm_merge, tm_ffn = 256, 512, 256, 512
    xp = x_prompt.reshape(bp * sp, D_MODEL)
    cos_p, sin_p = _rope_tables(0, sp, sp)
    z_p, cq_p, ckv_p, kr_p, krt_p, gates_p = _in_proj(
        xp, g1, *w_in_parts, qg, kvg, cos_p, sin_p, bp, sp, tm_in)
    qt_p, k_p, vt_p = _qkv_proj(cq_p, cos_p.T, sin_p.T, wqt, ckv_p, kr_p, wuk, wuvt2, bp, sp, tm_head)
    att_p = _flash(qt_p, k_p, vt_p)
    pool0 = jnp.zeros((bp, POOL_HALO, D_POOL), F32)
    h_p = _merge(z_p, pool0, att_p, gates_p, xp, pw, ps, wpo, wmo, wo, sp, 0, tm_merge)
    conv0 = jnp.zeros((bp, CONV_W - 1, D_FF), F32)
    y_p, tail_p = _ffn(h_p, g2, wup, cw, cb, conv0, wdn, fg, sp, tm_ffn)

    ns = bs * ss
    xs = x_sample.reshape(ns, D_MODEL)
    cos_s, sin_s = _rope_tables(past, ss, ns)
    z_s, cq_s, ckv_s, _, krt_s, gates_s = _in_proj(
        xs, g1, *w_in_parts, qg, kvg, cos_s, sin_s, bs, ss, 256)
    q_s = _q_proj(cq_s, cos_s, sin_s, wqt, bs, ss, ns)
    cache_krt = jnp.swapaxes(cache_krope[l], 1, 2)
    att_s = _decode_attn(q_s, wuk, wuv, cache_ckv[l], cache_krt, ckv_s, krt_s)
    pool_pre = jnp.pad(state_pool[l], ((0, 0), (POOL_HALO - POOL_STATE, 0), (0, 0)))
    h_s = _merge(z_s, pool_pre, att_s, gates_s, xs, pw, ps, wpo, wmo, wo, ss, past, 256)
    y_s, tail_s = _ffn(h_s, g2, wup, cw, cb, state_conv[l], wdn, fg, ss, ns)

    tiles_per_seq = sp // tm_ffn
    p_conv = tail_p.reshape(bp, tiles_per_seq, CONV_W - 1, D_FF)[:, -1]
    s_conv = tail_s.reshape(bs, CONV_W - 1, D_FF)
    return (
        y_p.reshape(bp, sp, D_MODEL),
        y_s.reshape(bs, ss, D_MODEL),
        ckv_p.reshape(1, bp, sp, KV_LORA),
        jnp.swapaxes(krt_p, 1, 2)[None],
        z_p.reshape(bp, sp, D_POOL)[:, sp - POOL_STATE:][None],
        p_conv[None],
        ckv_s.reshape(1, bs, ss, KV_LORA),
        jnp.swapaxes(krt_s, 1, 2)[None],
        z_s.reshape(bs, ss, D_POOL)[:, ss - POOL_STATE:][None],
        s_conv[None],
    )
```

```python
import functools

import jax
import jax.numpy as jnp
import numpy as np
from jax import lax
from jax.experimental import pallas as pl
from jax.experimental.pallas import tpu as pltpu

D_MODEL = 2048
CHUNK = 64
D_POOL = 1024
POOL_WINDOWS = (2, 4, 8, 16)
POOL_GROUP = D_POOL // len(POOL_WINDOWS)
POOL_STATE = max(POOL_WINDOWS) - 1
POOL_HALO = 8 * (max(POOL_WINDOWS).bit_length() - 1)
N_HEADS = 16
Q_LORA = 512
KV_LORA = 512
NOPE_DIM = 128
ROPE_DIM = 64
V_DIM = 128
V_AUG = V_DIM + 16
QK_DIM = NOPE_DIM + ROPE_DIM
ATTN_SCALE = QK_DIM ** -0.5
LOG2E = 1.4426950408889634
Q_SCALE = ATTN_SCALE * LOG2E
ROPE_BASE = 10000.0
D_FF = 6144
CONV_W = 3
CONV_HALO = 8
FFN_ROW_CHUNKS = 2
EPS = 1e-6
MASK_VALUE = -1e30

COL_Q = D_POOL
COL_KV = COL_Q + Q_LORA
COL_GATE = COL_KV + KV_LORA
COL_ROPE = COL_GATE + 2 * D_MODEL
W_IN_COLS = COL_ROPE + 2 * ROPE_DIM
W_IN_BLOCKS = W_IN_COLS // 128

V7X_VMEM_LIMIT = 60000 * 1024

F32 = jnp.float32
BF16 = jnp.bfloat16


def _rms(x, g):
    return x * lax.rsqrt(jnp.mean(x * x, axis=-1, keepdims=True) + EPS) * g


def _dot(a, b):
    return jnp.dot(a, b, preferred_element_type=F32)


def _dot_nt(a, b):
    return lax.dot_general(a, b, (((1,), (1,)), ((), ())), preferred_element_type=F32)


def _dot_tn(a, b):
    return lax.dot_general(a, b, (((0,), (0,)), ((), ())), preferred_element_type=F32)


def _resident(shape):
    zeros = (0,) * len(shape)
    return pl.BlockSpec(shape, lambda *_: zeros, pipeline_mode=pl.Buffered(1))


def _params(semantics):
    return pltpu.CompilerParams(dimension_semantics=semantics, vmem_limit_bytes=V7X_VMEM_LIMIT)


def _regroup_w_in_kernel(a_ref, b_ref, o_ref):
    first = a_ref[...]
    half = ROPE_DIM // 2
    swapped = jnp.concatenate([first[half:], first[:half]], axis=0)
    second = jnp.where(pl.program_id(0) == W_IN_BLOCKS - 1, swapped, b_ref[...])
    o_ref[...] = jnp.concatenate([first, second], axis=0).T.astype(BF16)


def _regroup_w_in(wt):
    rows = ROPE_DIM
    main_blocks, rope_block = COL_GATE // (2 * rows), COL_GATE // rows
    first = lambda c: jnp.where(c < main_blocks, 2 * c,
                                jnp.where(c < W_IN_BLOCKS - 1, rope_block + 1 + 2 * (c - main_blocks), rope_block))
    second = lambda c: jnp.where(c < W_IN_BLOCKS - 1, first(c) + 1, rope_block)
    return pl.pallas_call(
        _regroup_w_in_kernel,
        grid=(W_IN_BLOCKS,),
        in_specs=[pl.BlockSpec((rows, D_MODEL), lambda c: (first(c), 0)),
                  pl.BlockSpec((rows, D_MODEL), lambda c: (second(c), 0))],
        out_specs=pl.BlockSpec((D_MODEL, 2 * rows), lambda c: (0, c)),
        out_shape=jax.ShapeDtypeStruct((D_MODEL, W_IN_COLS), BF16),
        compiler_params=_params(("arbitrary",)),
        name="regroup_w_in",
    )(wt, wt)


def _in_proj_kernel(x_ref, g1_ref, w_ref, qg_ref, kvg_ref, cos_ref, sin_ref,
                    cost_ref, sint_ref, z_ref, cq_ref, ckv_ref, kr_ref, krt_ref, gate_ref, *, n_seg, seg_len):
    u = _rms(x_ref[...], g1_ref[...]).astype(BF16)
    z_ref[...] = _dot(u, w_ref[:, 0:COL_Q])
    cq_ref[...] = _rms(_dot(u, w_ref[:, COL_Q:COL_KV]), qg_ref[...]).astype(BF16)
    ckv_ref[...] = _rms(_dot(u, w_ref[:, COL_KV:COL_GATE]), kvg_ref[...])
    r = _dot(u, w_ref[:, COL_ROPE:W_IN_COLS])
    kr_ref[...] = (r[:, :ROPE_DIM] * cos_ref[...] + r[:, ROPE_DIM:] * sin_ref[...]).astype(BF16)
    rt = r.T
    krt = rt[:ROPE_DIM] * cost_ref[...] + rt[ROPE_DIM:] * sint_ref[...]
    for s in range(n_seg):
        krt_ref[s] = krt[:, s * seg_len:(s + 1) * seg_len]
    step = 1024
    for lo in range(0, 2 * D_MODEL, step):
        logits = _dot(u, w_ref[:, COL_GATE + lo:COL_GATE + lo + step])
        gate_ref[:, lo:lo + step] = 1.0 / (1.0 + jnp.exp(-logits))


def _in_proj(x, g1, w_all, qg, kvg, cos_t, sin_t, batch, seq, tm):
    n = x.shape[0]
    table_blocks = cos_t.shape[0] // tm
    tiles_per_seq = max(seq // tm, 1)
    n_seg = max(tm // seq, 1)
    row = lambda i: (i, 0)
    tab = lambda i: (i % table_blocks, 0)
    tab_t = lambda i: (0, i % table_blocks)
    return pl.pallas_call(
        functools.partial(_in_proj_kernel, n_seg=n_seg, seg_len=tm // n_seg),
        grid=(n // tm,),
        in_specs=[
            pl.BlockSpec((tm, D_MODEL), row),
            _resident((1, D_MODEL)),
            _resident((D_MODEL, W_IN_COLS)),
            _resident((1, Q_LORA)),
            _resident((1, KV_LORA)),
            pl.BlockSpec((tm, ROPE_DIM), tab),
            pl.BlockSpec((tm, ROPE_DIM), tab),
            pl.BlockSpec((ROPE_DIM, tm), tab_t),
            pl.BlockSpec((ROPE_DIM, tm), tab_t),
        ],
        out_specs=[
            pl.BlockSpec((tm, D_POOL), row),
            pl.BlockSpec((tm, Q_LORA), row),
            pl.BlockSpec((tm, KV_LORA), row),
            pl.BlockSpec((tm, ROPE_DIM), row),
            pl.BlockSpec((n_seg, ROPE_DIM, tm // n_seg), lambda i: (i // tiles_per_seq, 0, i % tiles_per_seq)),
            pl.BlockSpec((tm, 2 * D_MODEL), row),
        ],
        out_shape=[
            jax.ShapeDtypeStruct((n, D_POOL), F32),
            jax.ShapeDtypeStruct((n, Q_LORA), BF16),
            jax.ShapeDtypeStruct((n, KV_LORA), F32),
            jax.ShapeDtypeStruct((n, ROPE_DIM), BF16),
            jax.ShapeDtypeStruct((batch, ROPE_DIM, seq), F32),
            jax.ShapeDtypeStruct((n, 2 * D_MODEL), F32),
        ],
        compiler_params=_params(("arbitrary",)),
        name="in_proj",
    )(x, g1, w_all, qg, kvg, cos_t, sin_t, cos_t.T, sin_t.T)


def _q_proj_kernel(cq_ref, cos_ref, sin_ref, wqt_ref, q_ref, *, n_seg, seg_len):
    cq = cq_ref[...]
    cos = cos_ref[...] * Q_SCALE
    sin = sin_ref[...] * Q_SCALE

    def head(h, carry):
        res = _dot_nt(cq, wqt_ref[h])
        qn = (res[:, :NOPE_DIM] * Q_SCALE).astype(BF16)
        qr = (res[:, NOPE_DIM:QK_DIM] * cos + res[:, QK_DIM:] * sin).astype(BF16)
        for s in range(n_seg):
            rows = slice(s * seg_len, (s + 1) * seg_len)
            q_ref[s, h, :, 0:NOPE_DIM] = qn[rows]
            q_ref[s, h, :, NOPE_DIM:QK_DIM] = qr[rows]
        return carry

    lax.fori_loop(0, N_HEADS, head, 0)


def _q_proj(cq, cos_t, sin_t, wqt, batch, seq, tm):
    n = cq.shape[0]
    n_seg = tm // seq
    table_blocks = cos_t.shape[0] // tm
    return pl.pallas_call(
        functools.partial(_q_proj_kernel, n_seg=n_seg, seg_len=seq),
        grid=(n // tm,),
        in_specs=[
            pl.BlockSpec((tm, Q_LORA), lambda i: (i, 0)),
            pl.BlockSpec((tm, ROPE_DIM), lambda i: (i % table_blocks, 0)),
            pl.BlockSpec((tm, ROPE_DIM), lambda i: (i % table_blocks, 0)),
            _resident((N_HEADS, 2 * NOPE_DIM, Q_LORA)),
        ],
        out_specs=pl.BlockSpec((n_seg, N_HEADS, seq, QK_DIM), lambda i: (i, 0, 0, 0)),
        out_shape=jax.ShapeDtypeStruct((batch, N_HEADS, seq, QK_DIM), BF16),
        compiler_params=_params(("arbitrary",)),
        name="q_proj",
    )(cq, cos_t, sin_t, wqt)


def _qkv_proj_kernel(cq_ref, cost_ref, sint_ref, wqt_ref, ckv_ref, kr_ref, wuk_ref, wuvt_ref,
                     qt_ref, k_ref, vt_ref):
    cq = cq_ref[...]
    cos_t = cost_ref[...] * Q_SCALE
    sin_t = sint_ref[...] * Q_SCALE
    ckv = ckv_ref[...].astype(BF16)
    krb = kr_ref[...].astype(BF16)
    ones = jnp.ones((V_AUG - V_DIM, ckv.shape[0]), BF16)

    for hp in range(N_HEADS // 2):
        for hh in range(2):
            h = 2 * hp + hh
            res = _dot_nt(wqt_ref[h], cq)
            qt_ref[0, h, 0:NOPE_DIM, :] = (res[0:NOPE_DIM] * Q_SCALE).astype(BF16)
            qt_ref[0, h, NOPE_DIM:QK_DIM, :] = (
                res[NOPE_DIM:QK_DIM] * cos_t + res[QK_DIM:] * sin_t).astype(BF16)
        pair_cols = slice(2 * hp * NOPE_DIM, 2 * (hp + 1) * NOPE_DIM)
        kn = _dot(ckv, wuk_ref[:, pair_cols]).astype(BF16)
        vt = _dot_nt(wuvt_ref[hp], ckv).astype(BF16)
        for hh in range(2):
            h = 2 * hp + hh
            k_ref[0, h, 0, :, 0:NOPE_DIM] = kn[:, hh * NOPE_DIM:(hh + 1) * NOPE_DIM]
            k_ref[0, h, 0, :, NOPE_DIM:QK_DIM] = krb
            vt_ref[0, h, 0, 0:V_DIM, :] = vt[hh * V_DIM:(hh + 1) * V_DIM]
            vt_ref[0, h, 0, V_DIM:V_AUG, :] = ones


def _qkv_proj(cq, cos_tt, sin_tt, wqt, ckv, kr, wuk, wuvt2, batch, seq, tm):
    n = cq.shape[0]
    tiles_per_seq = seq // tm
    row = lambda i: (i, 0)
    tab = lambda i: (0, i % tiles_per_seq)
    tile = lambda i: (i // tiles_per_seq, 0, i % tiles_per_seq, 0, 0)
    return pl.pallas_call(
        _qkv_proj_kernel,
        grid=(n // tm,),
        in_specs=[
            pl.BlockSpec((tm, Q_LORA), row),
            pl.BlockSpec((ROPE_DIM, tm), tab),
            pl.BlockSpec((ROPE_DIM, tm), tab),
            _resident((N_HEADS, 2 * NOPE_DIM, Q_LORA)),
            pl.BlockSpec((tm, KV_LORA), row),
            pl.BlockSpec((tm, ROPE_DIM), row),
            _resident((KV_LORA, N_HEADS * NOPE_DIM)),
            _resident((N_HEADS // 2, 2 * V_DIM, KV_LORA)),
        ],
        out_specs=[
            pl.BlockSpec((1, N_HEADS, QK_DIM, tm), lambda i: (i // tiles_per_seq, 0, 0, i % tiles_per_seq)),
            pl.BlockSpec((1, N_HEADS, 1, tm, QK_DIM), tile),
            pl.BlockSpec((1, N_HEADS, 1, V_AUG, tm), tile),
        ],
        out_shape=[
            jax.ShapeDtypeStruct((batch, N_HEADS, QK_DIM, seq), BF16),
            jax.ShapeDtypeStruct((batch, N_HEADS, tiles_per_seq, tm, QK_DIM), BF16),
            jax.ShapeDtypeStruct((batch, N_HEADS, tiles_per_seq, V_AUG, tm), BF16),
        ],
        compiler_params=_params(("arbitrary",)),
        name="qkv_proj",
    )(cq, cos_tt, sin_tt, wqt, ckv, kr, wuk, wuvt2)


def _flash_kernel(qt_ref, qtn_ref, knew_ref, vtnew_ref, o_ref,
                  k_ref, vt_ref, s_ref, sf_ref, acc_ref, m_ref, cmax_ref, cmaxf_ref, bias_ref, *, heads, tile):
    qi = pl.program_id(2)
    for g in range(heads):
        k_ref[g, qi] = knew_ref[0, g, 0]
        vt_ref[g, qi] = vtnew_ref[0, g, 0]

    @pl.when((pl.program_id(0) == 0) & (pl.program_id(1) == 0) & (qi == 0))
    def _():
        key_chunk = lax.broadcasted_iota(jnp.int32, (tile, tile), 0) // CHUNK
        query_chunk = lax.broadcasted_iota(jnp.int32, (tile, tile), 1) // CHUNK
        bias_ref[...] = jnp.where(key_chunk <= query_chunk, 0.0, MASK_VALUE)

    def scores(g, j, slot):
        st = _dot(k_ref[g, j], qt_ref[0, g])
        s_ref[slot, g] = st
        cmax_ref[slot, g] = jnp.max(st, axis=0, keepdims=True)

    def next_first_scores(g):
        st = _dot(k_ref[g, 0], qtn_ref[0, g])
        sf_ref[g] = st
        cmaxf_ref[g] = jnp.max(st, axis=0, keepdims=True)

    def softmax(g, st, tile_max):
        m_old = m_ref[g]
        m_new = jnp.maximum(m_old, tile_max)
        m_ref[g] = m_new
        return jnp.exp2(st - m_new).astype(BF16), jnp.exp2(m_old - m_new)

    def values(g, j, p, alpha):
        acc_ref[g] = alpha * acc_ref[g] + _dot(vt_ref[g, j], p)

    def steps(j0, count, slot0, from_first=False):
        pending = None
        for t in range(count):
            cur = (slot0 + t) % 2
            for g in range(heads):
                if from_first and t == 0:
                    p, alpha = softmax(g, sf_ref[g], cmaxf_ref[g])
                else:
                    p, alpha = softmax(g, s_ref[cur, g], cmax_ref[cur, g])
                scores(g, j0 + t + 1, 1 - cur)
                if pending is not None:
                    values(*pending)
                pending = (g, j0 + t, p, alpha)
        values(*pending)

    for g in range(heads):
        m_ref[g] = jnp.full((1, tile), MASK_VALUE, F32)
        acc_ref[g] = jnp.zeros((V_AUG, tile), F32)

    @pl.when(qi == 0)
    def _():
        for g in range(heads):
            scores(g, 0, 0)

    @pl.when(qi >= 1)
    def _():
        steps(0, 1, 0, from_first=True)

    def two_steps(jj, carry):
        steps(1 + 2 * jj, 2, 1)
        return carry

    lax.fori_loop(0, jnp.maximum(qi - 1, 0) // 2, two_steps, 0)

    @pl.when((qi >= 2) & (qi % 2 == 0))
    def _():
        steps(qi - 1, 1, 1)

    def finish(g, p, alpha):
        acc = alpha * acc_ref[g] + _dot(vt_ref[g, qi], p)
        o = acc[0:V_DIM] / acc[V_DIM:V_DIM + 1]
        o_ref[:, g * V_DIM:(g + 1) * V_DIM] = o.T.astype(BF16)

    diag_slot = qi % 2
    pending = None
    for g in range(heads):
        st = s_ref[diag_slot, g] + bias_ref[...]
        p, alpha = softmax(g, st, jnp.max(st, axis=0, keepdims=True))
        next_first_scores(g)
        if pending is not None:
            finish(*pending)
        pending = (g, p, alpha)
    finish(*pending)


def _flash(qt, k, vt, heads=4):
    batch, _, _, seq = qt.shape
    n_tiles, tile = k.shape[2], k.shape[3]
    return pl.pallas_call(
        functools.partial(_flash_kernel, heads=heads, tile=tile),
        grid=(batch, N_HEADS // heads, n_tiles),
        in_specs=[
            pl.BlockSpec((1, heads, QK_DIM, tile), lambda b, hg, qi: (b, hg, 0, qi)),
            pl.BlockSpec((1, heads, QK_DIM, tile),
                         lambda b, hg, qi: (b, hg, 0, jnp.minimum(qi + 1, n_tiles - 1))),
            pl.BlockSpec((1, heads, 1, tile, QK_DIM), lambda b, hg, qi: (b, hg, qi, 0, 0)),
            pl.BlockSpec((1, heads, 1, V_AUG, tile), lambda b, hg, qi: (b, hg, qi, 0, 0)),
        ],
        out_specs=pl.BlockSpec((tile, heads * V_DIM), lambda b, hg, qi: (b * n_tiles + qi, hg)),
        out_shape=jax.ShapeDtypeStruct((batch * seq, N_HEADS * V_DIM), BF16),
        scratch_shapes=[
            pltpu.VMEM((heads, n_tiles, tile, QK_DIM), BF16),
            pltpu.VMEM((heads, n_tiles, V_AUG, tile), BF16),
            pltpu.VMEM((2, heads, tile, tile), F32),
            pltpu.VMEM((heads, tile, tile), F32),
            pltpu.VMEM((heads, V_AUG, tile), F32),
            pltpu.VMEM((heads, 1, tile), F32),
            pltpu.VMEM((2, heads, 1, tile), F32),
            pltpu.VMEM((heads, 1, tile), F32),
            pltpu.VMEM((tile, tile), F32),
        ],
        compiler_params=_params(("arbitrary", "arbitrary", "arbitrary")),
        name="flash",
    )(qt, qt, k, vt)


def _decode_kernel(q_ref, wuk_ref, wuv_ref, cc_ref, ckr_ref, cn_ref, krn_ref, o_ref,
                   qt, s_ref, m_s, l_s, acc_s, *, seq, halves):
    kt = pl.program_id(1)
    half = cc_ref.shape[1] // halves

    @pl.when(kt == 0)
    def _():
        eye = (lax.broadcasted_iota(jnp.int32, (ROPE_DIM, ROPE_DIM), 0)
               == lax.broadcasted_iota(jnp.int32, (ROPE_DIM, ROPE_DIM), 1)).astype(BF16)
        for hp in range(N_HEADS // 2):
            lat, rope_t = [], []
            for h in (2 * hp, 2 * hp + 1):
                w_h = wuk_ref[:, h * NOPE_DIM:(h + 1) * NOPE_DIM]
                lat.append(_dot_nt(w_h, q_ref[0, h, :, 0:NOPE_DIM]))
                rope_t.append(_dot_nt(eye, q_ref[0, h, :, NOPE_DIM:QK_DIM]))
            cols = slice(2 * hp * seq, 2 * (hp + 1) * seq)
            qt[0:KV_LORA, cols] = jnp.concatenate(lat, axis=1).astype(BF16)
            qt[KV_LORA:, cols] = jnp.concatenate(rope_t, axis=1).astype(BF16)
        m_s[...] = jnp.full(m_s.shape, MASK_VALUE, F32)
        l_s[...] = jnp.zeros(l_s.shape, F32)
        acc_s[...] = jnp.zeros(acc_s.shape, F32)

    def scores(kc, krt, slot):
        n = kc.shape[0]
        s_ref[slot, 0:n, :] = _dot(kc, qt[0:KV_LORA, :]) + _dot_tn(krt, qt[KV_LORA:, :])

    def update(kc, slot):
        n = kc.shape[0]
        st = s_ref[slot, 0:n, :]
        m_old = m_s[...]
        m_new = jnp.maximum(m_old, jnp.max(st, axis=0, keepdims=True))
        alpha = jnp.exp2(m_old - m_new)
        p = jnp.exp2(st - m_new)
        l_s[...] = alpha * l_s[...] + jnp.sum(p, axis=0, keepdims=True)
        acc_s[...] = alpha * acc_s[...] + _dot_tn(kc, p.astype(BF16))
        m_s[...] = m_new

    parts = [cc_ref[0, i * half:(i + 1) * half, :].astype(BF16) for i in range(halves)]
    for i in range(halves):
        scores(parts[i], ckr_ref[0, :, i * half:(i + 1) * half].astype(BF16), i)
    for i in range(halves):
        update(parts[i], i)

    @pl.when(kt == pl.num_programs(1) - 1)
    def _():
        new = cn_ref[...].astype(BF16)
        scores(new, krn_ref[0].astype(BF16), 0)
        update(new, 0)
        o_lat = (acc_s[...] / l_s[...]).T.astype(BF16)
        for h in range(N_HEADS):
            o_h = _dot(o_lat[h * seq:(h + 1) * seq], wuv_ref[:, h * V_DIM:(h + 1) * V_DIM])
            o_ref[:, h * V_DIM:(h + 1) * V_DIM] = o_h.astype(BF16)


def _decode_attn(q, wuk, wuv, cache_ckv, cache_kr, ckv_new, kr_new, key_tile=1024, halves=2):
    batch, _, seq, _ = q.shape
    past = cache_ckv.shape[1]
    cols = N_HEADS * seq
    return pl.pallas_call(
        functools.partial(_decode_kernel, seq=seq, halves=halves),
        grid=(batch, past // key_tile),
        in_specs=[
            pl.BlockSpec((1, N_HEADS, seq, QK_DIM), lambda b, kt: (b, 0, 0, 0)),
            _resident((KV_LORA, N_HEADS * NOPE_DIM)),
            _resident((KV_LORA, N_HEADS * V_DIM)),
            pl.BlockSpec((1, key_tile, KV_LORA), lambda b, kt: (b, kt, 0)),
            pl.BlockSpec((1, ROPE_DIM, key_tile), lambda b, kt: (b, 0, kt)),
            pl.BlockSpec((seq, KV_LORA), lambda b, kt: (b, 0)),
            pl.BlockSpec((1, ROPE_DIM, seq), lambda b, kt: (b, 0, 0)),
        ],
        out_specs=pl.BlockSpec((seq, N_HEADS * V_DIM), lambda b, kt: (b, 0)),
        out_shape=jax.ShapeDtypeStruct((batch * seq, N_HEADS * V_DIM), BF16),
        scratch_shapes=[
            pltpu.VMEM((KV_LORA + ROPE_DIM, cols), BF16),
            pltpu.VMEM((halves, key_tile // halves, cols), F32),
            pltpu.VMEM((1, cols), F32),
            pltpu.VMEM((1, cols), F32),
            pltpu.VMEM((KV_LORA, cols), F32),
        ],
        compiler_params=_params(("arbitrary", "arbitrary")),
        name="decode_attn",
    )(q, wuk, wuv, cache_ckv, cache_kr, ckv_new, kr_new)


def _window_sum(ext, tmp, cols, w):
    end = ext.shape[0]
    levels = w.bit_length() - 1
    for k in range(levels):
        lo, shift = 8 * (k + 1), 2 ** k
        if k == 0:
            val = ext[lo:end, cols] + ext[lo - shift:end - shift, cols]
        else:
            below = tmp.at[(k - 1) % 2]
            val = below[lo:end, :] + below[lo - shift:end - shift, :]
        if k < levels - 1:
            tmp[k % 2, lo:end, :] = val
    return val[POOL_HALO - 8 * levels:]


def _merge_kernel(z_ref, zprev_ref, pre_ref, att_ref, gate_ref, x_ref, pw_ref, ps_ref,
                  wpo_ref, wmo_ref, wo_ref, h_ref, ext, tmp, ypool, *, n_seg, seg_len, tiles_per_seq, pos0):
    t = pl.program_id(0) % tiles_per_seq
    row = lax.broadcasted_iota(jnp.int32, (seg_len, 1), 0)
    pos = pos0 + t * seg_len + row
    br_b = _dot(att_ref[...], wmo_ref[...])
    for s in range(n_seg):
        if tiles_per_seq == 1:
            halo = pre_ref[s]
        else:
            halo = jnp.where(t == 0, pre_ref[0], zprev_ref[...])
        ext[0:POOL_HALO, :] = halo
        ext[POOL_HALO:, :] = z_ref[s * seg_len:(s + 1) * seg_len, :]
        for g, w in enumerate(POOL_WINDOWS):
            cols = slice(g * POOL_GROUP, (g + 1) * POOL_GROUP)
            cur = ext[POOL_HALO:, cols]
            total = _window_sum(ext, tmp, cols, w)
            count = jnp.minimum(pos + 1, w).astype(F32)
            diff = total / count - cur
            y = _dot(diff.astype(BF16), pw_ref[g]) * ps_ref[:, cols]
            ypool[s * seg_len:(s + 1) * seg_len, cols] = y.astype(BF16)
    br_a = _dot(ypool[...], wpo_ref[...])
    merged = gate_ref[:, 0:D_MODEL] * br_a + gate_ref[:, D_MODEL:] * br_b
    h_ref[...] = x_ref[...] + _dot(merged.astype(BF16), wo_ref[...])


def _merge(z, prefix, att, gates, x, pool_w, pool_scale, w_pool_out, w_mla_out, w_out, seq, pos0, tm):
    n = x.shape[0]
    tiles_per_seq = max(seq // tm, 1)
    n_seg = max(tm // seq, 1)
    seg_len = tm // n_seg
    halo_blocks = tm // POOL_HALO
    row = lambda i: (i, 0)
    return pl.pallas_call(
        functools.partial(_merge_kernel, n_seg=n_seg, seg_len=seg_len,
                          tiles_per_seq=tiles_per_seq, pos0=pos0),
        grid=(n // tm,),
        in_specs=[
            pl.BlockSpec((tm, D_POOL), row),
            pl.BlockSpec((POOL_HALO, D_POOL), lambda i: (jnp.maximum(i * halo_blocks - 1, 0), 0)),
            pl.BlockSpec((n_seg, POOL_HALO, D_POOL), lambda i: (i // tiles_per_seq, 0, 0)),
            pl.BlockSpec((tm, N_HEADS * V_DIM), row),
            pl.BlockSpec((tm, 2 * D_MODEL), row),
            pl.BlockSpec((tm, D_MODEL), row),
            _resident((len(POOL_WINDOWS), POOL_GROUP, POOL_GROUP)),
            _resident((1, D_POOL)),
            _resident((D_POOL, D_MODEL)),
            _resident((N_HEADS * V_DIM, D_MODEL)),
            _resident((D_MODEL, D_MODEL)),
        ],
        out_specs=pl.BlockSpec((tm, D_MODEL), row),
        out_shape=jax.ShapeDtypeStruct((n, D_MODEL), F32),
        scratch_shapes=[
            pltpu.VMEM((POOL_HALO + seg_len, D_POOL), F32),
            pltpu.VMEM((2, POOL_HALO + seg_len, POOL_GROUP), F32),
            pltpu.VMEM((tm, D_POOL), BF16),
        ],
        compiler_params=_params(("arbitrary",)),
        name="merge",
    )(z, z, prefix, att, gates, x, pool_w, pool_scale, w_pool_out, w_mla_out, w_out)


def _ffn_kernel(h_ref, g2_ref, wa_ref, wb_ref, cw_ref, cb_ref, pre_ref, wd_ref, fg_ref,
                y_ref, tail_ref, hn, ext, carry, *, n_seg, seg_len, tiles_per_seq):
    i = pl.program_id(0)
    j = pl.program_id(1)
    t = i % tiles_per_seq

    if tiles_per_seq > 1:
        @pl.when(t == 0)
        def _():
            carry[j] = pre_ref[0]

    step = functools.partial(
        _ffn_step, h_ref, g2_ref, wa_ref, wb_ref, cw_ref, cb_ref, pre_ref, wd_ref, fg_ref, y_ref, tail_ref,
        hn, ext, carry, n_seg=n_seg, seg_len=seg_len, tiles_per_seq=tiles_per_seq, j=j)
    last_j = pl.num_programs(1) - 1

    @pl.when(j == 0)
    def _():
        step(first=True, last=False)

    @pl.when((j > 0) & (j < last_j))
    def _():
        step(first=False, last=False)

    @pl.when(j == last_j)
    def _():
        step(first=False, last=True)


def _ffn_step(h_ref, g2_ref, wa_ref, wb_ref, cw_ref, cb_ref, pre_ref, wd_ref, fg_ref,
              y_ref, tail_ref, hn, ext, carry, *, n_seg, seg_len, tiles_per_seq, j, first, last):
    tm = hn.shape[0]
    rc = tm // FFN_ROW_CHUNKS
    up = []
    for c in range(FFN_ROW_CHUNKS):
        rows = slice(c * rc, (c + 1) * rc)
        if first:
            hn[rows, :] = _rms(h_ref[rows, :], g2_ref[...]).astype(BF16)
        hc = hn[rows, :]
        up.append((_dot(hc, wa_ref[...]), _dot(hc, wb_ref[...])))

    def conv_gelu(seg, lo, a_rows, b_rows):
        n = a_rows.shape[0]
        base = CONV_HALO + lo
        ext[seg, base:base + n, :] = a_rows
        c = cb_ref[...] + ext[seg, base - 2:base - 2 + n, :] * cw_ref[0:1, :]
        c = c + ext[seg, base - 1:base - 1 + n, :] * cw_ref[1:2, :]
        c = c + a_rows * cw_ref[2:3, :]
        gelu = 0.5 * c * (1.0 + lax.erf(c * (2.0 ** -0.5)))
        return (gelu * b_rows).astype(BF16)

    for c in range(FFN_ROW_CHUNKS):
        a_c, b_c = up[c]
        if n_seg == 1:
            if c == 0:
                ext[0, CONV_HALO - (CONV_W - 1):CONV_HALO, :] = carry[j] if tiles_per_seq > 1 else pre_ref[0]
            gated_c = conv_gelu(0, c * rc, a_c, b_c)
            if c == FFN_ROW_CHUNKS - 1:
                tail_rows = a_c[rc - (CONV_W - 1):]
                tail_ref[0, 0] = tail_rows
                if tiles_per_seq > 1:
                    carry[j] = tail_rows
        else:
            per_chunk = n_seg // FFN_ROW_CHUNKS
            parts = []
            for k in range(per_chunk):
                s = c * per_chunk + k
                rows = slice(k * seg_len, (k + 1) * seg_len)
                ext[s, CONV_HALO - (CONV_W - 1):CONV_HALO, :] = pre_ref[s]
                parts.append(conv_gelu(s, 0, a_c[rows], b_c[rows]))
                tail_ref[0, s] = a_c[rows][seg_len - (CONV_W - 1):]
            gated_c = jnp.concatenate(parts, axis=0)
        rows = slice(c * rc, (c + 1) * rc)
        down = _dot(gated_c, wd_ref[...])
        if first:
            y_ref[rows, :] = down
        elif last:
            y_ref[rows, :] = _rms(h_ref[rows, :] + (y_ref[rows, :] + down), fg_ref[...])
        else:
            y_ref[rows, :] += down


def _ffn(h, norm2_g, w_up, conv_w, conv_b, prefix, w_down, final_g, seq, tm, tf=1024):
    n = h.shape[0]
    tiles_per_seq = max(seq // tm, 1)
    n_seg = max(tm // seq, 1)
    seg_len = tm // n_seg
    nff = D_FF // tf
    return pl.pallas_call(
        functools.partial(_ffn_kernel, n_seg=n_seg, seg_len=seg_len, tiles_per_seq=tiles_per_seq),
        grid=(n // tm, nff),
        in_specs=[
            pl.BlockSpec((tm, D_MODEL), lambda i, j: (i, 0)),
            _resident((1, D_MODEL)),
            pl.BlockSpec((D_MODEL, tf), lambda i, j: (0, j)),
            pl.BlockSpec((D_MODEL, tf), lambda i, j: (0, nff + j)),
            pl.BlockSpec((CONV_W, tf), lambda i, j: (0, j)),
            pl.BlockSpec((1, tf), lambda i, j: (0, j)),
            pl.BlockSpec((n_seg, CONV_W - 1, tf), lambda i, j: (i // tiles_per_seq, 0, j)),
            pl.BlockSpec((tf, D_MODEL), lambda i, j: (j, 0)),
            _resident((1, D_MODEL)),
        ],
        out_specs=[
            pl.BlockSpec((tm, D_MODEL), lambda i, j: (i, 0)),
            pl.BlockSpec((1, n_seg, CONV_W - 1, tf), lambda i, j: (i, 0, 0, j)),
        ],
        out_shape=[
            jax.ShapeDtypeStruct((n, D_MODEL), F32),
            jax.ShapeDtypeStruct((n // tm, n_seg, CONV_W - 1, D_FF), F32),
        ],
        scratch_shapes=[
            pltpu.VMEM((tm, D_MODEL), BF16),
            pltpu.VMEM((n_seg, CONV_HALO + seg_len, tf), F32),
            pltpu.VMEM((nff, CONV_W - 1, tf), F32),
        ],
        compiler_params=_params(("arbitrary", "arbitrary")),
        name="ffn",
    )(h, norm2_g, w_up, w_up, conv_w, conv_b, prefix, w_down, final_g)


def _rope_tables(pos0, length, rows):
    pos = (pos0 + np.arange(length)).astype(np.float64)
    inv = ROPE_BASE ** (-(np.arange(ROPE_DIM // 2, dtype=np.float64) * 2.0 / ROPE_DIM))
    ang = pos[:, None] * inv[None, :]
    cos, sin = np.cos(ang), np.sin(ang)
    cos_t = np.concatenate([cos, cos], axis=-1).astype(np.float32)
    sin_t = np.concatenate([-sin, sin], axis=-1).astype(np.float32)
    reps = max(rows // length, 1)
    return np.tile(cos_t, (reps, 1)), np.tile(sin_t, (reps, 1))


def _swap_halves(w):
    return jnp.concatenate([w[..., ROPE_DIM // 2:], w[..., :ROPE_DIM // 2]], axis=-1)


def kernel(x_prompt, x_sample, cache_ckv, cache_krope, state_pool, state_conv, norm1_g, w_in, pool_w, pool_scale, w_pool_out, q_norm_g, w_uq, kv_norm_g, w_uk, w_uv, w_mla_out, w_out, norm2_g, w_up, conv_w, conv_b, w_down, final_g):
    l = 0
    bp, sp, _ = x_prompt.shape
    bs, ss, _ = x_sample.shape
    past = cache_ckv.shape[2]

    w = w_in[l]
    w_all = _regroup_w_in(w.T)
    wq = w_uq[l].reshape(Q_LORA, N_HEADS, QK_DIM)
    wq = jnp.concatenate([wq, _swap_halves(wq[..., NOPE_DIM:])], axis=-1)
    wqt = wq.transpose(1, 2, 0).astype(BF16)
    wuk = w_uk[l].astype(BF16).reshape(KV_LORA, N_HEADS * NOPE_DIM)
    wuv = w_uv[l].astype(BF16).reshape(KV_LORA, N_HEADS * V_DIM)
    wuvt2 = wuv.reshape(KV_LORA, N_HEADS // 2, 2 * V_DIM).transpose(1, 2, 0)
    pw = pool_w[l].astype(BF16)
    wpo = w_pool_out[l].astype(BF16)
    wmo = w_mla_out[l].astype(BF16)
    wo = w_out[l].astype(BF16)
    wup = w_up[l].astype(BF16)
    wdn = w_down[l].astype(BF16)
    g1 = norm1_g[l][None]
    qg = q_norm_g[l][None]
    kvg = kv_norm_g[l][None]
    g2 = norm2_g[l][None]
    fg = final_g[None]
    ps = pool_scale[l][None]
    cb = conv_b[l][None]
    cw = conv_w[l]

    tm_in, tm_head, tm_merge, tm_ffn = 256, 512, 256, 512
    xp = x_prompt.reshape(bp * sp, D_MODEL)
    cos_p, sin_p = _rope_tables(0, sp, sp)
    z_p, cq_p, ckv_p, kr_p, krt_p, gates_p = _in_proj(
        xp, g1, w_all, qg, kvg, cos_p, sin_p, bp, sp, tm_in)
    qt_p, k_p, vt_p = _qkv_proj(cq_p, cos_p.T, sin_p.T, wqt, ckv_p, kr_p, wuk, wuvt2, bp, sp, tm_head)
    att_p = _flash(qt_p, k_p, vt_p)
    pool0 = jnp.zeros((bp, POOL_HALO, D_POOL), F32)
    h_p = _merge(z_p, pool0, att_p, gates_p, xp, pw, ps, wpo, wmo, wo, sp, 0, tm_merge)
    conv0 = jnp.zeros((bp, CONV_W - 1, D_FF), F32)
    y_p, tail_p = _ffn(h_p, g2, wup, cw, cb, conv0, wdn, fg, sp, tm_ffn)

    ns = bs * ss
    xs = x_sample.reshape(ns, D_MODEL)
    cos_s, sin_s = _rope_tables(past, ss, ns)
    z_s, cq_s, ckv_s, _, krt_s, gates_s = _in_proj(
        xs, g1, w_all, qg, kvg, cos_s, sin_s, bs, ss, 256)
    q_s = _q_proj(cq_s, cos_s, sin_s, wqt, bs, ss, ns)
    cache_krt = jnp.swapaxes(cache_krope[l], 1, 2)
    att_s = _decode_attn(q_s, wuk, wuv, cache_ckv[l], cache_krt, ckv_s, krt_s)
    pool_pre = jnp.pad(state_pool[l], ((0, 0), (POOL_HALO - POOL_STATE, 0), (0, 0)))
    h_s = _merge(z_s, pool_pre, att_s, gates_s, xs, pw, ps, wpo, wmo, wo, ss, past, 256)
    y_s, tail_s = _ffn(h_s, g2, wup, cw, cb, state_conv[l], wdn, fg, ss, ns)

    tiles_per_seq = sp // tm_ffn
    p_conv = tail_p.reshape(bp, tiles_per_seq, CONV_W - 1, D_FF)[:, -1]
    s_conv = tail_s.reshape(bs, CONV_W - 1, D_FF)
    return (
        y_p.reshape(bp, sp, D_MODEL),
        y_s.reshape(bs, ss, D_MODEL),
        ckv_p.reshape(1, bp, sp, KV_LORA),
        jnp.swapaxes(krt_p, 1, 2)[None],
        z_p.reshape(bp, sp, D_POOL)[:, sp - POOL_STATE:][None],
        p_conv[None],
        ckv_s.reshape(1, bs, ss, KV_LORA),
        jnp.swapaxes(krt_s, 1, 2)[None],
        z_s.reshape(bs, ss, D_POOL)[:, ss - POOL_STATE:][None],
        s_conv[None],
    )
```

```python
import functools

import jax
import jax.numpy as jnp
import numpy as np
from jax import lax
from jax.experimental import pallas as pl
from jax.experimental.pallas import tpu as pltpu

D_MODEL = 2048
CHUNK = 64
D_POOL = 1024
POOL_WINDOWS = (2, 4, 8, 16)
POOL_GROUP = D_POOL // len(POOL_WINDOWS)
POOL_STATE = max(POOL_WINDOWS) - 1
POOL_HALO = 8 * (max(POOL_WINDOWS).bit_length() - 1)
N_HEADS = 16
Q_LORA = 512
KV_LORA = 512
NOPE_DIM = 128
ROPE_DIM = 64
V_DIM = 128
V_AUG = V_DIM + 16
QK_DIM = NOPE_DIM + ROPE_DIM
ATTN_SCALE = QK_DIM ** -0.5
LOG2E = 1.4426950408889634
Q_SCALE = ATTN_SCALE * LOG2E
ROPE_BASE = 10000.0
D_FF = 6144
CONV_W = 3
CONV_HALO = 8
FFN_ROW_CHUNKS = 2
EPS = 1e-6
MASK_VALUE = -1e30

COL_Q = D_POOL
COL_KV = COL_Q + Q_LORA
COL_GATE = COL_KV + KV_LORA
COL_ROPE = COL_GATE + 2 * D_MODEL
W_IN_BLOCK = 256
W_IN_COLS = COL_ROPE + W_IN_BLOCK
W_IN_BLOCKS = W_IN_COLS // W_IN_BLOCK

V7X_VMEM_LIMIT = 60000 * 1024

F32 = jnp.float32
BF16 = jnp.bfloat16


def _rms(x, g):
    return x * lax.rsqrt(jnp.mean(x * x, axis=-1, keepdims=True) + EPS) * g


def _dot(a, b):
    return jnp.dot(a, b, preferred_element_type=F32)


def _dot_nt(a, b):
    return lax.dot_general(a, b, (((1,), (1,)), ((), ())), preferred_element_type=F32)


def _dot_tn(a, b):
    return lax.dot_general(a, b, (((0,), (0,)), ((), ())), preferred_element_type=F32)


def _resident(shape):
    zeros = (0,) * len(shape)
    return pl.BlockSpec(shape, lambda *_: zeros, pipeline_mode=pl.Buffered(1))


def _params(semantics):
    return pltpu.CompilerParams(dimension_semantics=semantics, vmem_limit_bytes=V7X_VMEM_LIMIT)


def _regroup_w_in_kernel(*refs):
    *in_refs, o_ref = refs
    rope = in_refs[0][...]
    half = ROPE_DIM // 2
    swapped = jnp.concatenate([rope[half:], rope[:half]], axis=0)
    is_rope_block = pl.program_id(0) == W_IN_BLOCKS - 1
    parts = [rope] + [jnp.where(is_rope_block, swapped if k % 2 else rope, in_refs[k][...])
                      for k in range(1, len(in_refs))]
    o_ref[...] = jnp.concatenate(parts, axis=0).T.astype(BF16)


def _regroup_w_in(wt):
    per_step = W_IN_BLOCK // ROPE_DIM
    main_steps, rope_block = COL_GATE // W_IN_BLOCK, COL_GATE // ROPE_DIM

    def source(k):
        def index(c):
            body = jnp.where(c < main_steps, per_step * c, rope_block + 1 + per_step * (c - main_steps)) + k
            return jnp.where(c < W_IN_BLOCKS - 1, body, rope_block), 0
        return index

    return pl.pallas_call(
        _regroup_w_in_kernel,
        grid=(W_IN_BLOCKS,),
        in_specs=[pl.BlockSpec((ROPE_DIM, D_MODEL), source(k)) for k in range(per_step)],
        out_specs=pl.BlockSpec((D_MODEL, W_IN_BLOCK), lambda c: (0, c)),
        out_shape=jax.ShapeDtypeStruct((D_MODEL, W_IN_COLS), BF16),
        compiler_params=_params(("arbitrary",)),
        name="regroup_w_in",
    )(*([wt] * per_step))


def _in_proj_kernel(x_ref, g1_ref, w_ref, qg_ref, kvg_ref, cos_ref, sin_ref,
                    cost_ref, sint_ref, z_ref, cq_ref, ckv_ref, kr_ref, krt_ref, gate_ref, *, n_seg, seg_len):
    u = _rms(x_ref[...], g1_ref[...]).astype(BF16)
    z_ref[...] = _dot(u, w_ref[:, 0:COL_Q])
    cq_ref[...] = _rms(_dot(u, w_ref[:, COL_Q:COL_KV]), qg_ref[...]).astype(BF16)
    ckv_ref[...] = _rms(_dot(u, w_ref[:, COL_KV:COL_GATE]), kvg_ref[...])
    r = _dot(u, w_ref[:, COL_ROPE:COL_ROPE + 2 * ROPE_DIM])
    kr_ref[...] = (r[:, :ROPE_DIM] * cos_ref[...] + r[:, ROPE_DIM:] * sin_ref[...]).astype(BF16)
    rt = r.T
    krt = rt[:ROPE_DIM] * cost_ref[...] + rt[ROPE_DIM:] * sint_ref[...]
    for s in range(n_seg):
        krt_ref[s] = krt[:, s * seg_len:(s + 1) * seg_len]
    step = 1024
    for lo in range(0, 2 * D_MODEL, step):
        logits = _dot(u, w_ref[:, COL_GATE + lo:COL_GATE + lo + step])
        gate_ref[:, lo:lo + step] = 1.0 / (1.0 + jnp.exp(-logits))


def _in_proj(x, g1, w_all, qg, kvg, cos_t, sin_t, batch, seq, tm):
    n = x.shape[0]
    table_blocks = cos_t.shape[0] // tm
    tiles_per_seq = max(seq // tm, 1)
    n_seg = max(tm // seq, 1)
    row = lambda i: (i, 0)
    tab = lambda i: (i % table_blocks, 0)
    tab_t = lambda i: (0, i % table_blocks)
    return pl.pallas_call(
        functools.partial(_in_proj_kernel, n_seg=n_seg, seg_len=tm // n_seg),
        grid=(n // tm,),
        in_specs=[
            pl.BlockSpec((tm, D_MODEL), row),
            _resident((1, D_MODEL)),
            _resident((D_MODEL, W_IN_COLS)),
            _resident((1, Q_LORA)),
            _resident((1, KV_LORA)),
            pl.BlockSpec((tm, ROPE_DIM), tab),
            pl.BlockSpec((tm, ROPE_DIM), tab),
            pl.BlockSpec((ROPE_DIM, tm), tab_t),
            pl.BlockSpec((ROPE_DIM, tm), tab_t),
        ],
        out_specs=[
            pl.BlockSpec((tm, D_POOL), row),
            pl.BlockSpec((tm, Q_LORA), row),
            pl.BlockSpec((tm, KV_LORA), row),
            pl.BlockSpec((tm, ROPE_DIM), row),
            pl.BlockSpec((n_seg, ROPE_DIM, tm // n_seg), lambda i: (i // tiles_per_seq, 0, i % tiles_per_seq)),
            pl.BlockSpec((tm, 2 * D_MODEL), row),
        ],
        out_shape=[
            jax.ShapeDtypeStruct((n, D_POOL), F32),
            jax.ShapeDtypeStruct((n, Q_LORA), BF16),
            jax.ShapeDtypeStruct((n, KV_LORA), F32),
            jax.ShapeDtypeStruct((n, ROPE_DIM), BF16),
            jax.ShapeDtypeStruct((batch, ROPE_DIM, seq), F32),
            jax.ShapeDtypeStruct((n, 2 * D_MODEL), F32),
        ],
        compiler_params=_params(("arbitrary",)),
        name="in_proj",
    )(x, g1, w_all, qg, kvg, cos_t, sin_t, cos_t.T, sin_t.T)


def _q_proj_kernel(cq_ref, cos_ref, sin_ref, wqt_ref, q_ref, *, n_seg, seg_len):
    cq = cq_ref[...]
    cos = cos_ref[...] * Q_SCALE
    sin = sin_ref[...] * Q_SCALE

    def head(h, carry):
        res = _dot_nt(cq, wqt_ref[h])
        qn = (res[:, :NOPE_DIM] * Q_SCALE).astype(BF16)
        qr = (res[:, NOPE_DIM:QK_DIM] * cos + res[:, QK_DIM:] * sin).astype(BF16)
        for s in range(n_seg):
            rows = slice(s * seg_len, (s + 1) * seg_len)
            q_ref[s, h, :, 0:NOPE_DIM] = qn[rows]
            q_ref[s, h, :, NOPE_DIM:QK_DIM] = qr[rows]
        return carry

    lax.fori_loop(0, N_HEADS, head, 0)


def _q_proj(cq, cos_t, sin_t, wqt, batch, seq, tm):
    n = cq.shape[0]
    n_seg = tm // seq
    table_blocks = cos_t.shape[0] // tm
    return pl.pallas_call(
        functools.partial(_q_proj_kernel, n_seg=n_seg, seg_len=seq),
        grid=(n // tm,),
        in_specs=[
            pl.BlockSpec((tm, Q_LORA), lambda i: (i, 0)),
            pl.BlockSpec((tm, ROPE_DIM), lambda i: (i % table_blocks, 0)),
            pl.BlockSpec((tm, ROPE_DIM), lambda i: (i % table_blocks, 0)),
            _resident((N_HEADS, 2 * NOPE_DIM, Q_LORA)),
        ],
        out_specs=pl.BlockSpec((n_seg, N_HEADS, seq, QK_DIM), lambda i: (i, 0, 0, 0)),
        out_shape=jax.ShapeDtypeStruct((batch, N_HEADS, seq, QK_DIM), BF16),
        compiler_params=_params(("arbitrary",)),
        name="q_proj",
    )(cq, cos_t, sin_t, wqt)


def _qkv_proj_kernel(cq_ref, cost_ref, sint_ref, wqt_ref, ckv_ref, kr_ref, wuk_ref, wuvt_ref,
                     qt_ref, k_ref, vt_ref):
    cq = cq_ref[...]
    cos_t = cost_ref[...] * Q_SCALE
    sin_t = sint_ref[...] * Q_SCALE
    ckv = ckv_ref[...].astype(BF16)
    krb = kr_ref[...].astype(BF16)
    ones = jnp.ones((V_AUG - V_DIM, ckv.shape[0]), BF16)

    for hp in range(N_HEADS // 2):
        for hh in range(2):
            h = 2 * hp + hh
            res = _dot_nt(wqt_ref[h], cq)
            qt_ref[0, h, 0:NOPE_DIM, :] = (res[0:NOPE_DIM] * Q_SCALE).astype(BF16)
            qt_ref[0, h, NOPE_DIM:QK_DIM, :] = (
                res[NOPE_DIM:QK_DIM] * cos_t + res[QK_DIM:] * sin_t).astype(BF16)
        pair_cols = slice(2 * hp * NOPE_DIM, 2 * (hp + 1) * NOPE_DIM)
        kn = _dot(ckv, wuk_ref[:, pair_cols]).astype(BF16)
        vt = _dot_nt(wuvt_ref[hp], ckv).astype(BF16)
        for hh in range(2):
            h = 2 * hp + hh
            k_ref[0, h, 0, :, 0:NOPE_DIM] = kn[:, hh * NOPE_DIM:(hh + 1) * NOPE_DIM]
            k_ref[0, h, 0, :, NOPE_DIM:QK_DIM] = krb
            vt_ref[0, h, 0, 0:V_DIM, :] = vt[hh * V_DIM:(hh + 1) * V_DIM]
            vt_ref[0, h, 0, V_DIM:V_AUG, :] = ones


def _qkv_proj(cq, cos_tt, sin_tt, wqt, ckv, kr, wuk, wuvt2, batch, seq, tm):
    n = cq.shape[0]
    tiles_per_seq = seq // tm
    row = lambda i: (i, 0)
    tab = lambda i: (0, i % tiles_per_seq)
    tile = lambda i: (i // tiles_per_seq, 0, i % tiles_per_seq, 0, 0)
    return pl.pallas_call(
        _qkv_proj_kernel,
        grid=(n // tm,),
        in_specs=[
            pl.BlockSpec((tm, Q_LORA), row),
            pl.BlockSpec((ROPE_DIM, tm), tab),
            pl.BlockSpec((ROPE_DIM, tm), tab),
            _resident((N_HEADS, 2 * NOPE_DIM, Q_LORA)),
            pl.BlockSpec((tm, KV_LORA), row),
            pl.BlockSpec((tm, ROPE_DIM), row),
            _resident((KV_LORA, N_HEADS * NOPE_DIM)),
            _resident((N_HEADS // 2, 2 * V_DIM, KV_LORA)),
        ],
        out_specs=[
            pl.BlockSpec((1, N_HEADS, QK_DIM, tm), lambda i: (i // tiles_per_seq, 0, 0, i % tiles_per_seq)),
            pl.BlockSpec((1, N_HEADS, 1, tm, QK_DIM), tile),
            pl.BlockSpec((1, N_HEADS, 1, V_AUG, tm), tile),
        ],
        out_shape=[
            jax.ShapeDtypeStruct((batch, N_HEADS, QK_DIM, seq), BF16),
            jax.ShapeDtypeStruct((batch, N_HEADS, tiles_per_seq, tm, QK_DIM), BF16),
            jax.ShapeDtypeStruct((batch, N_HEADS, tiles_per_seq, V_AUG, tm), BF16),
        ],
        compiler_params=_params(("arbitrary",)),
        name="qkv_proj",
    )(cq, cos_tt, sin_tt, wqt, ckv, kr, wuk, wuvt2)


def _flash_kernel(qt_ref, qtn_ref, knew_ref, vtnew_ref, o_ref,
                  k_ref, vt_ref, s_ref, sf_ref, acc_ref, m_ref, cmax_ref, cmaxf_ref, bias_ref, *, heads, tile):
    qi = pl.program_id(2)
    for g in range(heads):
        k_ref[g, qi] = knew_ref[0, g, 0]
        vt_ref[g, qi] = vtnew_ref[0, g, 0]

    @pl.when((pl.program_id(0) == 0) & (pl.program_id(1) == 0) & (qi == 0))
    def _():
        key_chunk = lax.broadcasted_iota(jnp.int32, (tile, tile), 0) // CHUNK
        query_chunk = lax.broadcasted_iota(jnp.int32, (tile, tile), 1) // CHUNK
        bias_ref[...] = jnp.where(key_chunk <= query_chunk, 0.0, MASK_VALUE)

    def scores(g, j, slot):
        st = _dot(k_ref[g, j], qt_ref[0, g])
        s_ref[slot, g] = st
        cmax_ref[slot, g] = jnp.max(st, axis=0, keepdims=True)

    def next_first_scores(g):
        st = _dot(k_ref[g, 0], qtn_ref[0, g])
        sf_ref[g] = st
        cmaxf_ref[g] = jnp.max(st, axis=0, keepdims=True)

    def softmax(g, st, tile_max):
        m_old = m_ref[g]
        m_new = jnp.maximum(m_old, tile_max)
        m_ref[g] = m_new
        return jnp.exp2(st - m_new).astype(BF16), jnp.exp2(m_old - m_new)

    def values(g, j, p, alpha):
        acc_ref[g] = alpha * acc_ref[g] + _dot(vt_ref[g, j], p)

    def steps(j0, count, slot0, from_first=False):
        pending = None
        for t in range(count):
            cur = (slot0 + t) % 2
            for g in range(heads):
                if from_first and t == 0:
                    p, alpha = softmax(g, sf_ref[g], cmaxf_ref[g])
                else:
                    p, alpha = softmax(g, s_ref[cur, g], cmax_ref[cur, g])
                scores(g, j0 + t + 1, 1 - cur)
                if pending is not None:
                    values(*pending)
                pending = (g, j0 + t, p, alpha)
        values(*pending)

    for g in range(heads):
        m_ref[g] = jnp.full((1, tile), MASK_VALUE, F32)
        acc_ref[g] = jnp.zeros((V_AUG, tile), F32)

    @pl.when(qi == 0)
    def _():
        for g in range(heads):
            scores(g, 0, 0)

    @pl.when(qi >= 1)
    def _():
        steps(0, 1, 0, from_first=True)

    def two_steps(jj, carry):
        steps(1 + 2 * jj, 2, 1)
        return carry

    lax.fori_loop(0, jnp.maximum(qi - 1, 0) // 2, two_steps, 0)

    @pl.when((qi >= 2) & (qi % 2 == 0))
    def _():
        steps(qi - 1, 1, 1)

    def finish(g, p, alpha):
        acc = alpha * acc_ref[g] + _dot(vt_ref[g, qi], p)
        o = acc[0:V_DIM] / acc[V_DIM:V_DIM + 1]
        o_ref[:, g * V_DIM:(g + 1) * V_DIM] = o.T.astype(BF16)

    diag_slot = qi % 2
    pending = None
    for g in range(heads):
        st = s_ref[diag_slot, g] + bias_ref[...]
        p, alpha = softmax(g, st, jnp.max(st, axis=0, keepdims=True))
        next_first_scores(g)
        if pending is not None:
            finish(*pending)
        pending = (g, p, alpha)
    finish(*pending)


def _flash(qt, k, vt, heads=4):
    batch, _, _, seq = qt.shape
    n_tiles, tile = k.shape[2], k.shape[3]
    return pl.pallas_call(
        functools.partial(_flash_kernel, heads=heads, tile=tile),
        grid=(batch, N_HEADS // heads, n_tiles),
        in_specs=[
            pl.BlockSpec((1, heads, QK_DIM, tile), lambda b, hg, qi: (b, hg, 0, qi)),
            pl.BlockSpec((1, heads, QK_DIM, tile),
                         lambda b, hg, qi: (b, hg, 0, jnp.minimum(qi + 1, n_tiles - 1))),
            pl.BlockSpec((1, heads, 1, tile, QK_DIM), lambda b, hg, qi: (b, hg, qi, 0, 0)),
            pl.BlockSpec((1, heads, 1, V_AUG, tile), lambda b, hg, qi: (b, hg, qi, 0, 0)),
        ],
        out_specs=pl.BlockSpec((tile, heads * V_DIM), lambda b, hg, qi: (b * n_tiles + qi, hg)),
        out_shape=jax.ShapeDtypeStruct((batch * seq, N_HEADS * V_DIM), BF16),
        scratch_shapes=[
            pltpu.VMEM((heads, n_tiles, tile, QK_DIM), BF16),
            pltpu.VMEM((heads, n_tiles, V_AUG, tile), BF16),
            pltpu.VMEM((2, heads, tile, tile), F32),
            pltpu.VMEM((heads, tile, tile), F32),
            pltpu.VMEM((heads, V_AUG, tile), F32),
            pltpu.VMEM((heads, 1, tile), F32),
            pltpu.VMEM((2, heads, 1, tile), F32),
            pltpu.VMEM((heads, 1, tile), F32),
            pltpu.VMEM((tile, tile), F32),
        ],
        compiler_params=_params(("arbitrary", "arbitrary", "arbitrary")),
        name="flash",
    )(qt, qt, k, vt)


def _decode_kernel(q_ref, wuk_ref, wuv_ref, cc_ref, ckr_ref, cn_ref, krn_ref, o_ref,
                   qt, s_ref, m_s, l_s, acc_s, *, seq, halves):
    kt = pl.program_id(1)
    half = cc_ref.shape[1] // halves

    @pl.when(kt == 0)
    def _():
        eye = (lax.broadcasted_iota(jnp.int32, (ROPE_DIM, ROPE_DIM), 0)
               == lax.broadcasted_iota(jnp.int32, (ROPE_DIM, ROPE_DIM), 1)).astype(BF16)
        for hp in range(N_HEADS // 2):
            lat, rope_t = [], []
            for h in (2 * hp, 2 * hp + 1):
                w_h = wuk_ref[:, h * NOPE_DIM:(h + 1) * NOPE_DIM]
                lat.append(_dot_nt(w_h, q_ref[0, h, :, 0:NOPE_DIM]))
                rope_t.append(_dot_nt(eye, q_ref[0, h, :, NOPE_DIM:QK_DIM]))
            cols = slice(2 * hp * seq, 2 * (hp + 1) * seq)
            qt[0:KV_LORA, cols] = jnp.concatenate(lat, axis=1).astype(BF16)
            qt[KV_LORA:, cols] = jnp.concatenate(rope_t, axis=1).astype(BF16)
        m_s[...] = jnp.full(m_s.shape, MASK_VALUE, F32)
        l_s[...] = jnp.zeros(l_s.shape, F32)
        acc_s[...] = jnp.zeros(acc_s.shape, F32)

    def scores(kc, krt, slot):
        n = kc.shape[0]
        s_ref[slot, 0:n, :] = _dot(kc, qt[0:KV_LORA, :]) + _dot_tn(krt, qt[KV_LORA:, :])

    def update(kc, slot):
        n = kc.shape[0]
        st = s_ref[slot, 0:n, :]
        m_old = m_s[...]
        m_new = jnp.maximum(m_old, jnp.max(st, axis=0, keepdims=True))
        alpha = jnp.exp2(m_old - m_new)
        p = jnp.exp2(st - m_new)
        l_s[...] = alpha * l_s[...] + jnp.sum(p, axis=0, keepdims=True)
        acc_s[...] = alpha * acc_s[...] + _dot_tn(kc, p.astype(BF16))
        m_s[...] = m_new

    parts = [cc_ref[0, i * half:(i + 1) * half, :].astype(BF16) for i in range(halves)]
    for i in range(halves):
        scores(parts[i], ckr_ref[0, :, i * half:(i + 1) * half].astype(BF16), i)
    for i in range(halves):
        update(parts[i], i)

    @pl.when(kt == pl.num_programs(1) - 1)
    def _():
        new = cn_ref[...].astype(BF16)
        scores(new, krn_ref[0].astype(BF16), 0)
        update(new, 0)
        o_lat = (acc_s[...] / l_s[...]).T.astype(BF16)
        for h in range(N_HEADS):
            o_h = _dot(o_lat[h * seq:(h + 1) * seq], wuv_ref[:, h * V_DIM:(h + 1) * V_DIM])
            o_ref[:, h * V_DIM:(h + 1) * V_DIM] = o_h.astype(BF16)


def _decode_attn(q, wuk, wuv, cache_ckv, cache_kr, ckv_new, kr_new, key_tile=1024, halves=2):
    batch, _, seq, _ = q.shape
    past = cache_ckv.shape[1]
    cols = N_HEADS * seq
    return pl.pallas_call(
        functools.partial(_decode_kernel, seq=seq, halves=halves),
        grid=(batch, past // key_tile),
        in_specs=[
            pl.BlockSpec((1, N_HEADS, seq, QK_DIM), lambda b, kt: (b, 0, 0, 0)),
            _resident((KV_LORA, N_HEADS * NOPE_DIM)),
            _resident((KV_LORA, N_HEADS * V_DIM)),
            pl.BlockSpec((1, key_tile, KV_LORA), lambda b, kt: (b, kt, 0)),
            pl.BlockSpec((1, ROPE_DIM, key_tile), lambda b, kt: (b, 0, kt)),
            pl.BlockSpec((seq, KV_LORA), lambda b, kt: (b, 0)),
            pl.BlockSpec((1, ROPE_DIM, seq), lambda b, kt: (b, 0, 0)),
        ],
        out_specs=pl.BlockSpec((seq, N_HEADS * V_DIM), lambda b, kt: (b, 0)),
        out_shape=jax.ShapeDtypeStruct((batch * seq, N_HEADS * V_DIM), BF16),
        scratch_shapes=[
            pltpu.VMEM((KV_LORA + ROPE_DIM, cols), BF16),
            pltpu.VMEM((halves, key_tile // halves, cols), F32),
            pltpu.VMEM((1, cols), F32),
            pltpu.VMEM((1, cols), F32),
            pltpu.VMEM((KV_LORA, cols), F32),
        ],
        compiler_params=_params(("arbitrary", "arbitrary")),
        name="decode_attn",
    )(q, wuk, wuv, cache_ckv, cache_kr, ckv_new, kr_new)


def _window_sum(ext, tmp, cols, w):
    end = ext.shape[0]
    levels = w.bit_length() - 1
    for k in range(levels):
        lo, shift = 8 * (k + 1), 2 ** k
        if k == 0:
            val = ext[lo:end, cols] + ext[lo - shift:end - shift, cols]
        else:
            below = tmp.at[(k - 1) % 2]
            val = below[lo:end, :] + below[lo - shift:end - shift, :]
        if k < levels - 1:
            tmp[k % 2, lo:end, :] = val
    return val[POOL_HALO - 8 * levels:]


def _merge_kernel(z_ref, zprev_ref, pre_ref, att_ref, gate_ref, x_ref, pw_ref, ps_ref,
                  wpo_ref, wmo_ref, wo_ref, h_ref, ext, tmp, ypool, *, n_seg, seg_len, tiles_per_seq, pos0):
    t = pl.program_id(0) % tiles_per_seq
    row = lax.broadcasted_iota(jnp.int32, (seg_len, 1), 0)
    pos = pos0 + t * seg_len + row
    br_b = _dot(att_ref[...], wmo_ref[...])
    for s in range(n_seg):
        if tiles_per_seq == 1:
            halo = pre_ref[s]
        else:
            halo = jnp.where(t == 0, pre_ref[0], zprev_ref[...])
        ext[0:POOL_HALO, :] = halo
        ext[POOL_HALO:, :] = z_ref[s * seg_len:(s + 1) * seg_len, :]
        for g, w in enumerate(POOL_WINDOWS):
            cols = slice(g * POOL_GROUP, (g + 1) * POOL_GROUP)
            cur = ext[POOL_HALO:, cols]
            total = _window_sum(ext, tmp, cols, w)
            count = jnp.minimum(pos + 1, w).astype(F32)
            diff = total / count - cur
            y = _dot(diff.astype(BF16), pw_ref[g]) * ps_ref[:, cols]
            ypool[s * seg_len:(s + 1) * seg_len, cols] = y.astype(BF16)
    br_a = _dot(ypool[...], wpo_ref[...])
    merged = gate_ref[:, 0:D_MODEL] * br_a + gate_ref[:, D_MODEL:] * br_b
    h_ref[...] = x_ref[...] + _dot(merged.astype(BF16), wo_ref[...])


def _merge(z, prefix, att, gates, x, pool_w, pool_scale, w_pool_out, w_mla_out, w_out, seq, pos0, tm):
    n = x.shape[0]
    tiles_per_seq = max(seq // tm, 1)
    n_seg = max(tm // seq, 1)
    seg_len = tm // n_seg
    halo_blocks = tm // POOL_HALO
    row = lambda i: (i, 0)
    return pl.pallas_call(
        functools.partial(_merge_kernel, n_seg=n_seg, seg_len=seg_len,
                          tiles_per_seq=tiles_per_seq, pos0=pos0),
        grid=(n // tm,),
        in_specs=[
            pl.BlockSpec((tm, D_POOL), row),
            pl.BlockSpec((POOL_HALO, D_POOL), lambda i: (jnp.maximum(i * halo_blocks - 1, 0), 0)),
            pl.BlockSpec((n_seg, POOL_HALO, D_POOL), lambda i: (i // tiles_per_seq, 0, 0)),
            pl.BlockSpec((tm, N_HEADS * V_DIM), row),
            pl.BlockSpec((tm, 2 * D_MODEL), row),
            pl.BlockSpec((tm, D_MODEL), row),
            _resident((len(POOL_WINDOWS), POOL_GROUP, POOL_GROUP)),
            _resident((1, D_POOL)),
            _resident((D_POOL, D_MODEL)),
            _resident((N_HEADS * V_DIM, D_MODEL)),
            _resident((D_MODEL, D_MODEL)),
        ],
        out_specs=pl.BlockSpec((tm, D_MODEL), row),
        out_shape=jax.ShapeDtypeStruct((n, D_MODEL), F32),
        scratch_shapes=[
            pltpu.VMEM((POOL_HALO + seg_len, D_POOL), F32),
            pltpu.VMEM((2, POOL_HALO + seg_len, POOL_GROUP), F32),
            pltpu.VMEM((tm, D_POOL), BF16),
        ],
        compiler_params=_params(("arbitrary",)),
        name="merge",
    )(z, z, prefix, att, gates, x, pool_w, pool_scale, w_pool_out, w_mla_out, w_out)


def _ffn_kernel(h_ref, g2_ref, wa_ref, wb_ref, cw_ref, cb_ref, pre_ref, wd_ref, fg_ref,
                y_ref, tail_ref, hn, ext, carry, *, n_seg, seg_len, tiles_per_seq):
    i = pl.program_id(0)
    j = pl.program_id(1)
    t = i % tiles_per_seq

    @pl.when(j == 0)
    def _():
        hn[...] = _rms(h_ref[...], g2_ref[...]).astype(BF16)
        y_ref[...] = jnp.zeros(y_ref.shape, F32)

    if tiles_per_seq > 1:
        @pl.when(t == 0)
        def _():
            carry[j] = pre_ref[0]

    tm = hn.shape[0]
    rc = tm // FFN_ROW_CHUNKS
    up = []
    for c in range(FFN_ROW_CHUNKS):
        hc = hn[c * rc:(c + 1) * rc, :]
        up.append((_dot(hc, wa_ref[...]), _dot(hc, wb_ref[...])))

    def conv_gelu(seg, lo, a_rows, b_rows):
        n = a_rows.shape[0]
        base = CONV_HALO + lo
        ext[seg, base:base + n, :] = a_rows
        c = cb_ref[...] + ext[seg, base - 2:base - 2 + n, :] * cw_ref[0:1, :]
        c = c + ext[seg, base - 1:base - 1 + n, :] * cw_ref[1:2, :]
        c = c + a_rows * cw_ref[2:3, :]
        gelu = 0.5 * c * (1.0 + lax.erf(c * (2.0 ** -0.5)))
        return (gelu * b_rows).astype(BF16)

    for c in range(FFN_ROW_CHUNKS):
        a_c, b_c = up[c]
        if n_seg == 1:
            if c == 0:
                ext[0, CONV_HALO - (CONV_W - 1):CONV_HALO, :] = carry[j] if tiles_per_seq > 1 else pre_ref[0]
            gated_c = conv_gelu(0, c * rc, a_c, b_c)
            if c == FFN_ROW_CHUNKS - 1:
                tail_rows = a_c[rc - (CONV_W - 1):]
                tail_ref[0, 0] = tail_rows
                if tiles_per_seq > 1:
                    carry[j] = tail_rows
        else:
            per_chunk = n_seg // FFN_ROW_CHUNKS
            parts = []
            for k in range(per_chunk):
                s = c * per_chunk + k
                rows = slice(k * seg_len, (k + 1) * seg_len)
                ext[s, CONV_HALO - (CONV_W - 1):CONV_HALO, :] = pre_ref[s]
                parts.append(conv_gelu(s, 0, a_c[rows], b_c[rows]))
                tail_ref[0, s] = a_c[rows][seg_len - (CONV_W - 1):]
            gated_c = jnp.concatenate(parts, axis=0)
        y_ref[c * rc:(c + 1) * rc, :] += _dot(gated_c, wd_ref[...])

    @pl.when(j == pl.num_programs(1) - 1)
    def _():
        y_ref[...] = _rms(h_ref[...] + y_ref[...], fg_ref[...])


def _ffn(h, norm2_g, w_up, conv_w, conv_b, prefix, w_down, final_g, seq, tm, tf=1024):
    n = h.shape[0]
    tiles_per_seq = max(seq // tm, 1)
    n_seg = max(tm // seq, 1)
    seg_len = tm // n_seg
    nff = D_FF // tf
    return pl.pallas_call(
        functools.partial(_ffn_kernel, n_seg=n_seg, seg_len=seg_len, tiles_per_seq=tiles_per_seq),
        grid=(n // tm, nff),
        in_specs=[
            pl.BlockSpec((tm, D_MODEL), lambda i, j: (i, 0)),
            _resident((1, D_MODEL)),
            pl.BlockSpec((D_MODEL, tf), lambda i, j: (0, j)),
            pl.BlockSpec((D_MODEL, tf), lambda i, j: (0, nff + j)),
            pl.BlockSpec((CONV_W, tf), lambda i, j: (0, j)),
            pl.BlockSpec((1, tf), lambda i, j: (0, j)),
            pl.BlockSpec((n_seg, CONV_W - 1, tf), lambda i, j: (i // tiles_per_seq, 0, j)),
            pl.BlockSpec((tf, D_MODEL), lambda i, j: (j, 0)),
            _resident((1, D_MODEL)),
        ],
        out_specs=[
            pl.BlockSpec((tm, D_MODEL), lambda i, j: (i, 0)),
            pl.BlockSpec((1, n_seg, CONV_W - 1, tf), lambda i, j: (i, 0, 0, j)),
        ],
        out_shape=[
            jax.ShapeDtypeStruct((n, D_MODEL), F32),
            jax.ShapeDtypeStruct((n // tm, n_seg, CONV_W - 1, D_FF), F32),
        ],
        scratch_shapes=[
            pltpu.VMEM((tm, D_MODEL), BF16),
            pltpu.VMEM((n_seg, CONV_HALO + seg_len, tf), F32),
            pltpu.VMEM((nff, CONV_W - 1, tf), F32),
        ],
        compiler_params=_params(("arbitrary", "arbitrary")),
        name="ffn",
    )(h, norm2_g, w_up, w_up, conv_w, conv_b, prefix, w_down, final_g)


def _rope_tables(pos0, length, rows):
    pos = (pos0 + np.arange(length)).astype(np.float64)
    inv = ROPE_BASE ** (-(np.arange(ROPE_DIM // 2, dtype=np.float64) * 2.0 / ROPE_DIM))
    ang = pos[:, None] * inv[None, :]
    cos, sin = np.cos(ang), np.sin(ang)
    cos_t = np.concatenate([cos, cos], axis=-1).astype(np.float32)
    sin_t = np.concatenate([-sin, sin], axis=-1).astype(np.float32)
    reps = max(rows // length, 1)
    return np.tile(cos_t, (reps, 1)), np.tile(sin_t, (reps, 1))


def _swap_halves(w):
    return jnp.concatenate([w[..., ROPE_DIM // 2:], w[..., :ROPE_DIM // 2]], axis=-1)


def kernel(x_prompt, x_sample, cache_ckv, cache_krope, state_pool, state_conv, norm1_g, w_in, pool_w, pool_scale, w_pool_out, q_norm_g, w_uq, kv_norm_g, w_uk, w_uv, w_mla_out, w_out, norm2_g, w_up, conv_w, conv_b, w_down, final_g):
    l = 0
    bp, sp, _ = x_prompt.shape
    bs, ss, _ = x_sample.shape
    past = cache_ckv.shape[2]

    w = w_in[l]
    w_all = _regroup_w_in(w.T)
    wq = w_uq[l].reshape(Q_LORA, N_HEADS, QK_DIM)
    wq = jnp.concatenate([wq, _swap_halves(wq[..., NOPE_DIM:])], axis=-1)
    wqt = wq.transpose(1, 2, 0).astype(BF16)
    wuk = w_uk[l].astype(BF16).reshape(KV_LORA, N_HEADS * NOPE_DIM)
    wuv = w_uv[l].astype(BF16).reshape(KV_LORA, N_HEADS * V_DIM)
    wuvt2 = wuv.reshape(KV_LORA, N_HEADS // 2, 2 * V_DIM).transpose(1, 2, 0)
    pw = pool_w[l].astype(BF16)
    wpo = w_pool_out[l].astype(BF16)
    wmo = w_mla_out[l].astype(BF16)
    wo = w_out[l].astype(BF16)
    wup = w_up[l].astype(BF16)
    wdn = w_down[l].astype(BF16)
    g1 = norm1_g[l][None]
    qg = q_norm_g[l][None]
    kvg = kv_norm_g[l][None]
    g2 = norm2_g[l][None]
    fg = final_g[None]
    ps = pool_scale[l][None]
    cb = conv_b[l][None]
    cw = conv_w[l]

    tm_in, tm_head, tm_merge, tm_ffn = 256, 512, 256, 512
    xp = x_prompt.reshape(bp * sp, D_MODEL)
    cos_p, sin_p = _rope_tables(0, sp, sp)
    z_p, cq_p, ckv_p, kr_p, krt_p, gates_p = _in_proj(
        xp, g1, w_all, qg, kvg, cos_p, sin_p, bp, sp, tm_in)
    qt_p, k_p, vt_p = _qkv_proj(cq_p, cos_p.T, sin_p.T, wqt, ckv_p, kr_p, wuk, wuvt2, bp, sp, tm_head)
    att_p = _flash(qt_p, k_p, vt_p)
    pool0 = jnp.zeros((bp, POOL_HALO, D_POOL), F32)
    h_p = _merge(z_p, pool0, att_p, gates_p, xp, pw, ps, wpo, wmo, wo, sp, 0, tm_merge)
    conv0 = jnp.zeros((bp, CONV_W - 1, D_FF), F32)
    y_p, tail_p = _ffn(h_p, g2, wup, cw, cb, conv0, wdn, fg, sp, tm_ffn)

    ns = bs * ss
    xs = x_sample.reshape(ns, D_MODEL)
    cos_s, sin_s = _rope_tables(past, ss, ns)
    z_s, cq_s, ckv_s, _, krt_s, gates_s = _in_proj(
        xs, g1, w_all, qg, kvg, cos_s, sin_s, bs, ss, 256)
    q_s = _q_proj(cq_s, cos_s, sin_s, wqt, bs, ss, ns)
    cache_krt = jnp.swapaxes(cache_krope[l], 1, 2)
    att_s = _decode_attn(q_s, wuk, wuv, cache_ckv[l], cache_krt, ckv_s, krt_s)
    pool_pre = jnp.pad(state_pool[l], ((0, 0), (POOL_HALO - POOL_STATE, 0), (0, 0)))
    h_s = _merge(z_s, pool_pre, att_s, gates_s, xs, pw, ps, wpo, wmo, wo, ss, past, 256)
    y_s, tail_s = _ffn(h_s, g2, wup, cw, cb, state_conv[l], wdn, fg, ss, ns)

    tiles_per_seq = sp // tm_ffn
    p_conv = tail_p.reshape(bp, tiles_per_seq, CONV_W - 1, D_FF)[:, -1]
    s_conv = tail_s.reshape(bs, CONV_W - 1, D_FF)
    return (
        y_p.reshape(bp, sp, D_MODEL),
        y_s.reshape(bs, ss, D_MODEL),
        ckv_p.reshape(1, bp, sp, KV_LORA),
        jnp.swapaxes(krt_p, 1, 2)[None],
        z_p.reshape(bp, sp, D_POOL)[:, sp - POOL_STATE:][None],
        p_conv[None],
        ckv_s.reshape(1, bs, ss, KV_LORA),
        jnp.swapaxes(krt_s, 1, 2)[None],
        z_s.reshape(bs, ss, D_POOL)[:, ss - POOL_STATE:][None],
        s_conv[None],
    )
```

```python
import functools

import jax
import jax.numpy as jnp
import numpy as np
from jax import lax
from jax.experimental import pallas as pl
from jax.experimental.pallas import tpu as pltpu

D_MODEL = 2048
CHUNK = 64
D_POOL = 1024
POOL_WINDOWS = (2, 4, 8, 16)
POOL_GROUP = D_POOL // len(POOL_WINDOWS)
POOL_STATE = max(POOL_WINDOWS) - 1
POOL_HALO = 8 * (max(POOL_WINDOWS).bit_length() - 1)
N_HEADS = 16
Q_LORA = 512
KV_LORA = 512
NOPE_DIM = 128
ROPE_DIM = 64
V_DIM = 128
V_AUG = V_DIM + 16
QK_DIM = NOPE_DIM + ROPE_DIM
ATTN_SCALE = QK_DIM ** -0.5
LOG2E = 1.4426950408889634
Q_SCALE = ATTN_SCALE * LOG2E
ROPE_BASE = 10000.0
D_FF = 6144
CONV_W = 3
CONV_HALO = 8
FFN_ROW_CHUNKS = 2
EPS = 1e-6
MASK_VALUE = -1e30

COL_Q = D_POOL
COL_KV = COL_Q + Q_LORA
COL_GATE = COL_KV + KV_LORA
COL_ROPE = COL_GATE + 2 * D_MODEL
W_IN_BLOCK = 256
W_IN_COLS = COL_ROPE + W_IN_BLOCK
W_IN_BLOCKS = W_IN_COLS // W_IN_BLOCK

V7X_VMEM_LIMIT = 60000 * 1024

TM_IN_PROJ = 256
TM_QKV = 512
TM_MERGE = 256
TM_FFN = 512
TF_FFN = 1024
FLASH_HEADS = 4
DECODE_KEY_TILE = 1024

F32 = jnp.float32
BF16 = jnp.bfloat16


def _rms(x, g):
    return x * lax.rsqrt(jnp.mean(x * x, axis=-1, keepdims=True) + EPS) * g


def _dot(a, b):
    return jnp.dot(a, b, preferred_element_type=F32)


def _dot_nt(a, b):
    return lax.dot_general(a, b, (((1,), (1,)), ((), ())), preferred_element_type=F32)


def _dot_tn(a, b):
    return lax.dot_general(a, b, (((0,), (0,)), ((), ())), preferred_element_type=F32)


def _resident(shape):
    zeros = (0,) * len(shape)
    return pl.BlockSpec(shape, lambda *_: zeros, pipeline_mode=pl.Buffered(1))


def _params(semantics):
    return pltpu.CompilerParams(dimension_semantics=semantics, vmem_limit_bytes=V7X_VMEM_LIMIT)


def _regroup_w_in_kernel(*refs):
    *in_refs, o_ref = refs
    rope = in_refs[0][...]
    half = ROPE_DIM // 2
    swapped = jnp.concatenate([rope[half:], rope[:half]], axis=0)
    is_rope_block = pl.program_id(0) == W_IN_BLOCKS - 1
    parts = [rope] + [jnp.where(is_rope_block, swapped if k % 2 else rope, in_refs[k][...])
                      for k in range(1, len(in_refs))]
    o_ref[...] = jnp.concatenate(parts, axis=0).T.astype(BF16)


def _regroup_w_in(wt):
    per_step = W_IN_BLOCK // ROPE_DIM
    main_steps, rope_block = COL_GATE // W_IN_BLOCK, COL_GATE // ROPE_DIM

    def source(k):
        def index(c):
            body = jnp.where(c < main_steps, per_step * c, rope_block + 1 + per_step * (c - main_steps)) + k
            return jnp.where(c < W_IN_BLOCKS - 1, body, rope_block), 0
        return index

    return pl.pallas_call(
        _regroup_w_in_kernel,
        grid=(W_IN_BLOCKS,),
        in_specs=[pl.BlockSpec((ROPE_DIM, D_MODEL), source(k)) for k in range(per_step)],
        out_specs=pl.BlockSpec((D_MODEL, W_IN_BLOCK), lambda c: (0, c)),
        out_shape=jax.ShapeDtypeStruct((D_MODEL, W_IN_COLS), BF16),
        compiler_params=_params(("arbitrary",)),
        name="regroup_w_in",
    )(*([wt] * per_step))


def _in_proj_kernel(x_ref, g1_ref, w_ref, qg_ref, kvg_ref, cos_ref, sin_ref,
                    cost_ref, sint_ref, z_ref, cq_ref, ckv_ref, kr_ref, krt_ref, gate_ref, *, n_seg, seg_len):
    u = _rms(x_ref[...], g1_ref[...]).astype(BF16)
    z_ref[...] = _dot(u, w_ref[:, 0:COL_Q])
    cq_ref[...] = _rms(_dot(u, w_ref[:, COL_Q:COL_KV]), qg_ref[...]).astype(BF16)
    ckv_ref[...] = _rms(_dot(u, w_ref[:, COL_KV:COL_GATE]), kvg_ref[...])
    r = _dot(u, w_ref[:, COL_ROPE:COL_ROPE + 2 * ROPE_DIM])
    kr_ref[...] = (r[:, :ROPE_DIM] * cos_ref[...] + r[:, ROPE_DIM:] * sin_ref[...]).astype(BF16)
    rt = r.T
    krt = rt[:ROPE_DIM] * cost_ref[...] + rt[ROPE_DIM:] * sint_ref[...]
    for s in range(n_seg):
        krt_ref[s] = krt[:, s * seg_len:(s + 1) * seg_len]
    step = 1024
    for lo in range(0, 2 * D_MODEL, step):
        logits = _dot(u, w_ref[:, COL_GATE + lo:COL_GATE + lo + step])
        gate_ref[:, lo:lo + step] = 1.0 / (1.0 + jnp.exp(-logits))


def _in_proj(x, g1, w_all, qg, kvg, cos_t, sin_t, batch, seq, tm):
    n = x.shape[0]
    table_blocks = cos_t.shape[0] // tm
    tiles_per_seq = max(seq // tm, 1)
    n_seg = max(tm // seq, 1)
    row = lambda i: (i, 0)
    tab = lambda i: (i % table_blocks, 0)
    tab_t = lambda i: (0, i % table_blocks)
    return pl.pallas_call(
        functools.partial(_in_proj_kernel, n_seg=n_seg, seg_len=tm // n_seg),
        grid=(n // tm,),
        in_specs=[
            pl.BlockSpec((tm, D_MODEL), row),
            _resident((1, D_MODEL)),
            _resident((D_MODEL, W_IN_COLS)),
            _resident((1, Q_LORA)),
            _resident((1, KV_LORA)),
            pl.BlockSpec((tm, ROPE_DIM), tab),
            pl.BlockSpec((tm, ROPE_DIM), tab),
            pl.BlockSpec((ROPE_DIM, tm), tab_t),
            pl.BlockSpec((ROPE_DIM, tm), tab_t),
        ],
        out_specs=[
            pl.BlockSpec((tm, D_POOL), row),
            pl.BlockSpec((tm, Q_LORA), row),
            pl.BlockSpec((tm, KV_LORA), row),
            pl.BlockSpec((tm, ROPE_DIM), row),
            pl.BlockSpec((n_seg, ROPE_DIM, tm // n_seg), lambda i: (i // tiles_per_seq, 0, i % tiles_per_seq)),
            pl.BlockSpec((tm, 2 * D_MODEL), row),
        ],
        out_shape=[
            jax.ShapeDtypeStruct((n, D_POOL), F32),
            jax.ShapeDtypeStruct((n, Q_LORA), BF16),
            jax.ShapeDtypeStruct((n, KV_LORA), F32),
            jax.ShapeDtypeStruct((n, ROPE_DIM), BF16),
            jax.ShapeDtypeStruct((batch, ROPE_DIM, seq), F32),
            jax.ShapeDtypeStruct((n, 2 * D_MODEL), F32),
        ],
        compiler_params=_params(("arbitrary",)),
        name="in_proj",
    )(x, g1, w_all, qg, kvg, cos_t, sin_t, cos_t.T, sin_t.T)


def _q_proj_kernel(cq_ref, cos_ref, sin_ref, wqt_ref, q_ref, *, n_seg, seg_len):
    cq = cq_ref[...]
    cos = cos_ref[...] * Q_SCALE
    sin = sin_ref[...] * Q_SCALE

    def head(h, carry):
        res = _dot_nt(cq, wqt_ref[h])
        qn = (res[:, :NOPE_DIM] * Q_SCALE).astype(BF16)
        qr = (res[:, NOPE_DIM:QK_DIM] * cos + res[:, QK_DIM:] * sin).astype(BF16)
        for s in range(n_seg):
            rows = slice(s * seg_len, (s + 1) * seg_len)
            q_ref[s, h, :, 0:NOPE_DIM] = qn[rows]
            q_ref[s, h, :, NOPE_DIM:QK_DIM] = qr[rows]
        return carry

    lax.fori_loop(0, N_HEADS, head, 0)


def _q_proj(cq, cos_t, sin_t, wqt, batch, seq, tm):
    n = cq.shape[0]
    n_seg = tm // seq
    table_blocks = cos_t.shape[0] // tm
    return pl.pallas_call(
        functools.partial(_q_proj_kernel, n_seg=n_seg, seg_len=seq),
        grid=(n // tm,),
        in_specs=[
            pl.BlockSpec((tm, Q_LORA), lambda i: (i, 0)),
            pl.BlockSpec((tm, ROPE_DIM), lambda i: (i % table_blocks, 0)),
            pl.BlockSpec((tm, ROPE_DIM), lambda i: (i % table_blocks, 0)),
            _resident((N_HEADS, 2 * NOPE_DIM, Q_LORA)),
        ],
        out_specs=pl.BlockSpec((n_seg, N_HEADS, seq, QK_DIM), lambda i: (i, 0, 0, 0)),
        out_shape=jax.ShapeDtypeStruct((batch, N_HEADS, seq, QK_DIM), BF16),
        compiler_params=_params(("arbitrary",)),
        name="q_proj",
    )(cq, cos_t, sin_t, wqt)


def _qkv_proj_kernel(cq_ref, cost_ref, sint_ref, wqt_ref, ckv_ref, kr_ref, wuk_ref, wuvt_ref,
                     qt_ref, k_ref, vt_ref):
    cq = cq_ref[...]
    cos_t = cost_ref[...] * Q_SCALE
    sin_t = sint_ref[...] * Q_SCALE
    ckv = ckv_ref[...].astype(BF16)
    krb = kr_ref[...].astype(BF16)
    ones = jnp.ones((V_AUG - V_DIM, ckv.shape[0]), BF16)

    for hp in range(N_HEADS // 2):
        for hh in range(2):
            h = 2 * hp + hh
            res = _dot_nt(wqt_ref[h], cq)
            qt_ref[0, h, 0:NOPE_DIM, :] = (res[0:NOPE_DIM] * Q_SCALE).astype(BF16)
            qt_ref[0, h, NOPE_DIM:QK_DIM, :] = (
                res[NOPE_DIM:QK_DIM] * cos_t + res[QK_DIM:] * sin_t).astype(BF16)
        pair_cols = slice(2 * hp * NOPE_DIM, 2 * (hp + 1) * NOPE_DIM)
        kn = _dot(ckv, wuk_ref[:, pair_cols]).astype(BF16)
        vt = _dot_nt(wuvt_ref[hp], ckv).astype(BF16)
        for hh in range(2):
            h = 2 * hp + hh
            k_ref[0, h, 0, :, 0:NOPE_DIM] = kn[:, hh * NOPE_DIM:(hh + 1) * NOPE_DIM]
            k_ref[0, h, 0, :, NOPE_DIM:QK_DIM] = krb
            vt_ref[0, h, 0, 0:V_DIM, :] = vt[hh * V_DIM:(hh + 1) * V_DIM]
            vt_ref[0, h, 0, V_DIM:V_AUG, :] = ones


def _qkv_proj(cq, cos_tt, sin_tt, wqt, ckv, kr, wuk, wuvt2, batch, seq, tm):
    n = cq.shape[0]
    tiles_per_seq = seq // tm
    row = lambda i: (i, 0)
    tab = lambda i: (0, i % tiles_per_seq)
    tile = lambda i: (i // tiles_per_seq, 0, i % tiles_per_seq, 0, 0)
    return pl.pallas_call(
        _qkv_proj_kernel,
        grid=(n // tm,),
        in_specs=[
            pl.BlockSpec((tm, Q_LORA), row),
            pl.BlockSpec((ROPE_DIM, tm), tab),
            pl.BlockSpec((ROPE_DIM, tm), tab),
            _resident((N_HEADS, 2 * NOPE_DIM, Q_LORA)),
            pl.BlockSpec((tm, KV_LORA), row),
            pl.BlockSpec((tm, ROPE_DIM), row),
            _resident((KV_LORA, N_HEADS * NOPE_DIM)),
            _resident((N_HEADS // 2, 2 * V_DIM, KV_LORA)),
        ],
        out_specs=[
            pl.BlockSpec((1, N_HEADS, QK_DIM, tm), lambda i: (i // tiles_per_seq, 0, 0, i % tiles_per_seq)),
            pl.BlockSpec((1, N_HEADS, 1, tm, QK_DIM), tile),
            pl.BlockSpec((1, N_HEADS, 1, V_AUG, tm), tile),
        ],
        out_shape=[
            jax.ShapeDtypeStruct((batch, N_HEADS, QK_DIM, seq), BF16),
            jax.ShapeDtypeStruct((batch, N_HEADS, tiles_per_seq, tm, QK_DIM), BF16),
            jax.ShapeDtypeStruct((batch, N_HEADS, tiles_per_seq, V_AUG, tm), BF16),
        ],
        compiler_params=_params(("arbitrary",)),
        name="qkv_proj",
    )(cq, cos_tt, sin_tt, wqt, ckv, kr, wuk, wuvt2)


def _flash_kernel(qt_ref, qtn_ref, knew_ref, vtnew_ref, o_ref,
                  k_ref, vt_ref, s_ref, sf_ref, acc_ref, m_ref, cmax_ref, cmaxf_ref, bias_ref, *, heads, tile):
    qi = pl.program_id(2)
    for g in range(heads):
        k_ref[g, qi] = knew_ref[0, g, 0]
        vt_ref[g, qi] = vtnew_ref[0, g, 0]

    @pl.when((pl.program_id(0) == 0) & (pl.program_id(1) == 0) & (qi == 0))
    def _():
        key_chunk = lax.broadcasted_iota(jnp.int32, (tile, tile), 0) // CHUNK
        query_chunk = lax.broadcasted_iota(jnp.int32, (tile, tile), 1) // CHUNK
        bias_ref[...] = jnp.where(key_chunk <= query_chunk, 0.0, MASK_VALUE)

    def scores(g, j, slot):
        st = _dot(k_ref[g, j], qt_ref[0, g])
        s_ref[slot, g] = st
        cmax_ref[slot, g] = jnp.max(st, axis=0, keepdims=True)

    def next_first_scores(g):
        st = _dot(k_ref[g, 0], qtn_ref[0, g])
        sf_ref[g] = st
        cmaxf_ref[g] = jnp.max(st, axis=0, keepdims=True)

    def softmax(g, st, tile_max):
        m_old = m_ref[g]
        m_new = jnp.maximum(m_old, tile_max)
        m_ref[g] = m_new
        return jnp.exp2(st - m_new).astype(BF16), jnp.exp2(m_old - m_new)

    def values(g, j, p, alpha):
        acc_ref[g] = alpha * acc_ref[g] + _dot(vt_ref[g, j], p)

    def steps(j0, count, slot0, from_first=False):
        pending = None
        for t in range(count):
            cur = (slot0 + t) % 2
            for g in range(heads):
                if from_first and t == 0:
                    p, alpha = softmax(g, sf_ref[g], cmaxf_ref[g])
                else:
                    p, alpha = softmax(g, s_ref[cur, g], cmax_ref[cur, g])
                scores(g, j0 + t + 1, 1 - cur)
                if pending is not None:
                    values(*pending)
                pending = (g, j0 + t, p, alpha)
        values(*pending)

    for g in range(heads):
        m_ref[g] = jnp.full((1, tile), MASK_VALUE, F32)
        acc_ref[g] = jnp.zeros((V_AUG, tile), F32)

    @pl.when(qi == 0)
    def _():
        for g in range(heads):
            scores(g, 0, 0)

    @pl.when(qi >= 1)
    def _():
        steps(0, 1, 0, from_first=True)

    rest = jnp.maximum(qi - 1, 0)

    def four_steps(jj, carry):
        steps(1 + 4 * jj, 4, 1)
        return carry

    lax.fori_loop(0, rest // 4, four_steps, 0)

    @pl.when(rest % 4 >= 2)
    def _():
        steps(1 + 4 * (rest // 4), 2, 1)

    @pl.when(rest % 2 == 1)
    def _():
        steps(qi - 1, 1, 1)

    def finish(g, p, alpha):
        acc = alpha * acc_ref[g] + _dot(vt_ref[g, qi], p)
        o = acc[0:V_DIM] / acc[V_DIM:V_DIM + 1]
        o_ref[:, g * V_DIM:(g + 1) * V_DIM] = o.T.astype(BF16)

    diag_slot = qi % 2
    pending = None
    for g in range(heads):
        st = s_ref[diag_slot, g] + bias_ref[...]
        p, alpha = softmax(g, st, jnp.max(st, axis=0, keepdims=True))
        next_first_scores(g)
        if pending is not None:
            finish(*pending)
        pending = (g, p, alpha)
    finish(*pending)


def _flash(qt, k, vt, heads=FLASH_HEADS):
    batch, _, _, seq = qt.shape
    n_tiles, tile = k.shape[2], k.shape[3]
    return pl.pallas_call(
        functools.partial(_flash_kernel, heads=heads, tile=tile),
        grid=(batch, N_HEADS // heads, n_tiles),
        in_specs=[
            pl.BlockSpec((1, heads, QK_DIM, tile), lambda b, hg, qi: (b, hg, 0, qi)),
            pl.BlockSpec((1, heads, QK_DIM, tile),
                         lambda b, hg, qi: (b, hg, 0, jnp.minimum(qi + 1, n_tiles - 1))),
            pl.BlockSpec((1, heads, 1, tile, QK_DIM), lambda b, hg, qi: (b, hg, qi, 0, 0)),
            pl.BlockSpec((1, heads, 1, V_AUG, tile), lambda b, hg, qi: (b, hg, qi, 0, 0)),
        ],
        out_specs=pl.BlockSpec((tile, heads * V_DIM), lambda b, hg, qi: (b * n_tiles + qi, hg)),
        out_shape=jax.ShapeDtypeStruct((batch * seq, N_HEADS * V_DIM), BF16),
        scratch_shapes=[
            pltpu.VMEM((heads, n_tiles, tile, QK_DIM), BF16),
            pltpu.VMEM((heads, n_tiles, V_AUG, tile), BF16),
            pltpu.VMEM((2, heads, tile, tile), F32),
            pltpu.VMEM((heads, tile, tile), F32),
            pltpu.VMEM((heads, V_AUG, tile), F32),
            pltpu.VMEM((heads, 1, tile), F32),
            pltpu.VMEM((2, heads, 1, tile), F32),
            pltpu.VMEM((heads, 1, tile), F32),
            pltpu.VMEM((tile, tile), F32),
        ],
        compiler_params=_params(("arbitrary", "arbitrary", "arbitrary")),
        name="flash",
    )(qt, qt, k, vt)


def _decode_kernel(q_ref, wuk_ref, wuv_ref, cc_ref, ckr_ref, cn_ref, krn_ref, o_ref,
                   qt, s_ref, m_s, l_s, acc_s, *, seq, halves):
    kt = pl.program_id(1)
    half = cc_ref.shape[1] // halves

    @pl.when(kt == 0)
    def _():
        eye = (lax.broadcasted_iota(jnp.int32, (ROPE_DIM, ROPE_DIM), 0)
               == lax.broadcasted_iota(jnp.int32, (ROPE_DIM, ROPE_DIM), 1)).astype(BF16)
        for hp in range(N_HEADS // 2):
            lat, rope_t = [], []
            for h in (2 * hp, 2 * hp + 1):
                w_h = wuk_ref[:, h * NOPE_DIM:(h + 1) * NOPE_DIM]
                lat.append(_dot_nt(w_h, q_ref[0, h, :, 0:NOPE_DIM]))
                rope_t.append(_dot_nt(eye, q_ref[0, h, :, NOPE_DIM:QK_DIM]))
            cols = slice(2 * hp * seq, 2 * (hp + 1) * seq)
            qt[0:KV_LORA, cols] = jnp.concatenate(lat, axis=1).astype(BF16)
            qt[KV_LORA:, cols] = jnp.concatenate(rope_t, axis=1).astype(BF16)
        m_s[...] = jnp.full(m_s.shape, MASK_VALUE, F32)
        l_s[...] = jnp.zeros(l_s.shape, F32)
        acc_s[...] = jnp.zeros(acc_s.shape, F32)

    def scores(kc, krt, slot):
        n = kc.shape[0]
        s_ref[slot, 0:n, :] = _dot(kc, qt[0:KV_LORA, :]) + _dot_tn(krt, qt[KV_LORA:, :])

    def update(kc, slot):
        n = kc.shape[0]
        st = s_ref[slot, 0:n, :]
        m_old = m_s[...]
        m_new = jnp.maximum(m_old, jnp.max(st, axis=0, keepdims=True))
        alpha = jnp.exp2(m_old - m_new)
        p = jnp.exp2(st - m_new)
        l_s[...] = alpha * l_s[...] + jnp.sum(p, axis=0, keepdims=True)
        acc_s[...] = alpha * acc_s[...] + _dot_tn(kc, p.astype(BF16))
        m_s[...] = m_new

    parts = [cc_ref[0, i * half:(i + 1) * half, :].astype(BF16) for i in range(halves)]
    for i in range(halves):
        scores(parts[i], ckr_ref[0, :, i * half:(i + 1) * half].astype(BF16), i)
    for i in range(halves):
        update(parts[i], i)

    @pl.when(kt == pl.num_programs(1) - 1)
    def _():
        new = cn_ref[...].astype(BF16)
        scores(new, krn_ref[0].astype(BF16), 0)
        update(new, 0)
        o_lat = (acc_s[...] / l_s[...]).T.astype(BF16)
        for h in range(N_HEADS):
            o_h = _dot(o_lat[h * seq:(h + 1) * seq], wuv_ref[:, h * V_DIM:(h + 1) * V_DIM])
            o_ref[:, h * V_DIM:(h + 1) * V_DIM] = o_h.astype(BF16)


def _decode_attn(q, wuk, wuv, cache_ckv, cache_kr, ckv_new, kr_new, key_tile=DECODE_KEY_TILE, halves=2):
    batch, _, seq, _ = q.shape
    past = cache_ckv.shape[1]
    cols = N_HEADS * seq
    return pl.pallas_call(
        functools.partial(_decode_kernel, seq=seq, halves=halves),
        grid=(batch, past // key_tile),
        in_specs=[
            pl.BlockSpec((1, N_HEADS, seq, QK_DIM), lambda b, kt: (b, 0, 0, 0)),
            _resident((KV_LORA, N_HEADS * NOPE_DIM)),
            _resident((KV_LORA, N_HEADS * V_DIM)),
            pl.BlockSpec((1, key_tile, KV_LORA), lambda b, kt: (b, kt, 0)),
            pl.BlockSpec((1, ROPE_DIM, key_tile), lambda b, kt: (b, 0, kt)),
            pl.BlockSpec((seq, KV_LORA), lambda b, kt: (b, 0)),
            pl.BlockSpec((1, ROPE_DIM, seq), lambda b, kt: (b, 0, 0)),
        ],
        out_specs=pl.BlockSpec((seq, N_HEADS * V_DIM), lambda b, kt: (b, 0)),
        out_shape=jax.ShapeDtypeStruct((batch * seq, N_HEADS * V_DIM), BF16),
        scratch_shapes=[
            pltpu.VMEM((KV_LORA + ROPE_DIM, cols), BF16),
            pltpu.VMEM((halves, key_tile // halves, cols), F32),
            pltpu.VMEM((1, cols), F32),
            pltpu.VMEM((1, cols), F32),
            pltpu.VMEM((KV_LORA, cols), F32),
        ],
        compiler_params=_params(("arbitrary", "arbitrary")),
        name="decode_attn",
    )(q, wuk, wuv, cache_ckv, cache_kr, ckv_new, kr_new)


def _window_sum(ext, tmp, cols, w):
    end = ext.shape[0]
    levels = w.bit_length() - 1
    for k in range(levels):
        lo, shift = 8 * (k + 1), 2 ** k
        if k == 0:
            val = ext[lo:end, cols] + ext[lo - shift:end - shift, cols]
        else:
            below = tmp.at[(k - 1) % 2]
            val = below[lo:end, :] + below[lo - shift:end - shift, :]
        if k < levels - 1:
            tmp[k % 2, lo:end, :] = val
    return val[POOL_HALO - 8 * levels:]


def _merge_kernel(z_ref, zprev_ref, pre_ref, att_ref, gate_ref, x_ref, pw_ref, ps_ref,
                  wpo_ref, wmo_ref, wo_ref, h_ref, ext, tmp, ypool, *, n_seg, seg_len, tiles_per_seq, pos0):
    t = pl.program_id(0) % tiles_per_seq
    row = lax.broadcasted_iota(jnp.int32, (seg_len, 1), 0)
    pos = pos0 + t * seg_len + row
    br_b = _dot(att_ref[...], wmo_ref[...])
    for s in range(n_seg):
        if tiles_per_seq == 1:
            halo = pre_ref[s]
        else:
            halo = jnp.where(t == 0, pre_ref[0], zprev_ref[...])
        ext[0:POOL_HALO, :] = halo
        ext[POOL_HALO:, :] = z_ref[s * seg_len:(s + 1) * seg_len, :]
        for g, w in enumerate(POOL_WINDOWS):
            cols = slice(g * POOL_GROUP, (g + 1) * POOL_GROUP)
            cur = ext[POOL_HALO:, cols]
            total = _window_sum(ext, tmp, cols, w)
            count = jnp.minimum(pos + 1, w).astype(F32)
            diff = total / count - cur
            y = _dot(diff.astype(BF16), pw_ref[g]) * ps_ref[:, cols]
            ypool[s * seg_len:(s + 1) * seg_len, cols] = y.astype(BF16)
    br_a = _dot(ypool[...], wpo_ref[...])
    merged = gate_ref[:, 0:D_MODEL] * br_a + gate_ref[:, D_MODEL:] * br_b
    h_ref[...] = x_ref[...] + _dot(merged.astype(BF16), wo_ref[...])


def _merge(z, prefix, att, gates, x, pool_w, pool_scale, w_pool_out, w_mla_out, w_out, seq, pos0, tm):
    n = x.shape[0]
    tiles_per_seq = max(seq // tm, 1)
    n_seg = max(tm // seq, 1)
    seg_len = tm // n_seg
    halo_blocks = tm // POOL_HALO
    row = lambda i: (i, 0)
    return pl.pallas_call(
        functools.partial(_merge_kernel, n_seg=n_seg, seg_len=seg_len,
                          tiles_per_seq=tiles_per_seq, pos0=pos0),
        grid=(n // tm,),
        in_specs=[
            pl.BlockSpec((tm, D_POOL), row),
            pl.BlockSpec((POOL_HALO, D_POOL), lambda i: (jnp.maximum(i * halo_blocks - 1, 0), 0)),
            pl.BlockSpec((n_seg, POOL_HALO, D_POOL), lambda i: (i // tiles_per_seq, 0, 0)),
            pl.BlockSpec((tm, N_HEADS * V_DIM), row),
            pl.BlockSpec((tm, 2 * D_MODEL), row),
            pl.BlockSpec((tm, D_MODEL), row),
            _resident((len(POOL_WINDOWS), POOL_GROUP, POOL_GROUP)),
            _resident((1, D_POOL)),
            _resident((D_POOL, D_MODEL)),
            _resident((N_HEADS * V_DIM, D_MODEL)),
            _resident((D_MODEL, D_MODEL)),
        ],
        out_specs=pl.BlockSpec((tm, D_MODEL), row),
        out_shape=jax.ShapeDtypeStruct((n, D_MODEL), F32),
        scratch_shapes=[
            pltpu.VMEM((POOL_HALO + seg_len, D_POOL), F32),
            pltpu.VMEM((2, POOL_HALO + seg_len, POOL_GROUP), F32),
            pltpu.VMEM((tm, D_POOL), BF16),
        ],
        compiler_params=_params(("arbitrary",)),
        name="merge",
    )(z, z, prefix, att, gates, x, pool_w, pool_scale, w_pool_out, w_mla_out, w_out)


def _ffn_kernel(h_ref, g2_ref, wa_ref, wb_ref, cw_ref, cb_ref, pre_ref, wd_ref, fg_ref,
                y_ref, tail_ref, hn, ext, carry, *, n_seg, seg_len, tiles_per_seq):
    i = pl.program_id(0)
    j = pl.program_id(1)
    t = i % tiles_per_seq

    @pl.when(j == 0)
    def _():
        hn[...] = _rms(h_ref[...], g2_ref[...]).astype(BF16)
        y_ref[...] = jnp.zeros(y_ref.shape, F32)

    if tiles_per_seq > 1:
        @pl.when(t == 0)
        def _():
            carry[j] = pre_ref[0]

    tm = hn.shape[0]
    rc = tm // FFN_ROW_CHUNKS
    up = []
    for c in range(FFN_ROW_CHUNKS):
        hc = hn[c * rc:(c + 1) * rc, :]
        up.append((_dot(hc, wa_ref[...]), _dot(hc, wb_ref[...])))

    def conv_gelu(seg, lo, a_rows, b_rows):
        n = a_rows.shape[0]
        base = CONV_HALO + lo
        ext[seg, base:base + n, :] = a_rows
        c = cb_ref[...] + ext[seg, base - 2:base - 2 + n, :] * cw_ref[0:1, :]
        c = c + ext[seg, base - 1:base - 1 + n, :] * cw_ref[1:2, :]
        c = c + a_rows * cw_ref[2:3, :]
        gelu = 0.5 * c * (1.0 + lax.erf(c * (2.0 ** -0.5)))
        return (gelu * b_rows).astype(BF16)

    for c in range(FFN_ROW_CHUNKS):
        a_c, b_c = up[c]
        if n_seg == 1:
            if c == 0:
                ext[0, CONV_HALO - (CONV_W - 1):CONV_HALO, :] = carry[j] if tiles_per_seq > 1 else pre_ref[0]
            gated_c = conv_gelu(0, c * rc, a_c, b_c)
            if c == FFN_ROW_CHUNKS - 1:
                tail_rows = a_c[rc - (CONV_W - 1):]
                tail_ref[0, 0] = tail_rows
                if tiles_per_seq > 1:
                    carry[j] = tail_rows
        else:
            per_chunk = n_seg // FFN_ROW_CHUNKS
            parts = []
            for k in range(per_chunk):
                s = c * per_chunk + k
                rows = slice(k * seg_len, (k + 1) * seg_len)
                ext[s, CONV_HALO - (CONV_W - 1):CONV_HALO, :] = pre_ref[s]
                parts.append(conv_gelu(s, 0, a_c[rows], b_c[rows]))
                tail_ref[0, s] = a_c[rows][seg_len - (CONV_W - 1):]
            gated_c = jnp.concatenate(parts, axis=0)
        y_ref[c * rc:(c + 1) * rc, :] += _dot(gated_c, wd_ref[...])

    @pl.when(j == pl.num_programs(1) - 1)
    def _():
        y_ref[...] = _rms(h_ref[...] + y_ref[...], fg_ref[...])


def _ffn(h, norm2_g, w_up, conv_w, conv_b, prefix, w_down, final_g, seq, tm, tf=TF_FFN):
    n = h.shape[0]
    tiles_per_seq = max(seq // tm, 1)
    n_seg = max(tm // seq, 1)
    seg_len = tm // n_seg
    nff = D_FF // tf
    return pl.pallas_call(
        functools.partial(_ffn_kernel, n_seg=n_seg, seg_len=seg_len, tiles_per_seq=tiles_per_seq),
        grid=(n // tm, nff),
        in_specs=[
            pl.BlockSpec((tm, D_MODEL), lambda i, j: (i, 0)),
            _resident((1, D_MODEL)),
            pl.BlockSpec((D_MODEL, tf), lambda i, j: (0, j)),
            pl.BlockSpec((D_MODEL, tf), lambda i, j: (0, nff + j)),
            pl.BlockSpec((CONV_W, tf), lambda i, j: (0, j)),
            pl.BlockSpec((1, tf), lambda i, j: (0, j)),
            pl.BlockSpec((n_seg, CONV_W - 1, tf), lambda i, j: (i // tiles_per_seq, 0, j)),
            pl.BlockSpec((tf, D_MODEL), lambda i, j: (j, 0)),
            _resident((1, D_MODEL)),
        ],
        out_specs=[
            pl.BlockSpec((tm, D_MODEL), lambda i, j: (i, 0)),
            pl.BlockSpec((1, n_seg, CONV_W - 1, tf), lambda i, j: (i, 0, 0, j)),
        ],
        out_shape=[
            jax.ShapeDtypeStruct((n, D_MODEL), F32),
            jax.ShapeDtypeStruct((n // tm, n_seg, CONV_W - 1, D_FF), F32),
        ],
        scratch_shapes=[
            pltpu.VMEM((tm, D_MODEL), BF16),
            pltpu.VMEM((n_seg, CONV_HALO + seg_len, tf), F32),
            pltpu.VMEM((nff, CONV_W - 1, tf), F32),
        ],
        compiler_params=_params(("arbitrary", "arbitrary")),
        name="ffn",
    )(h, norm2_g, w_up, w_up, conv_w, conv_b, prefix, w_down, final_g)


def _rope_tables(pos0, length, rows):
    pos = (pos0 + np.arange(length)).astype(np.float64)
    inv = ROPE_BASE ** (-(np.arange(ROPE_DIM // 2, dtype=np.float64) * 2.0 / ROPE_DIM))
    ang = pos[:, None] * inv[None, :]
    cos, sin = np.cos(ang), np.sin(ang)
    cos_t = np.concatenate([cos, cos], axis=-1).astype(np.float32)
    sin_t = np.concatenate([-sin, sin], axis=-1).astype(np.float32)
    reps = max(rows // length, 1)
    return np.tile(cos_t, (reps, 1)), np.tile(sin_t, (reps, 1))


def _swap_halves(w):
    return jnp.concatenate([w[..., ROPE_DIM // 2:], w[..., :ROPE_DIM // 2]], axis=-1)


def kernel(x_prompt, x_sample, cache_ckv, cache_krope, state_pool, state_conv, norm1_g, w_in, pool_w, pool_scale, w_pool_out, q_norm_g, w_uq, kv_norm_g, w_uk, w_uv, w_mla_out, w_out, norm2_g, w_up, conv_w, conv_b, w_down, final_g):
    l = 0
    bp, sp, _ = x_prompt.shape
    bs, ss, _ = x_sample.shape
    past = cache_ckv.shape[2]

    w = w_in[l]
    w_all = _regroup_w_in(w.T)
    wq = w_uq[l].reshape(Q_LORA, N_HEADS, QK_DIM)
    wq = jnp.concatenate([wq, _swap_halves(wq[..., NOPE_DIM:])], axis=-1)
    wqt = wq.transpose(1, 2, 0).astype(BF16)
    wuk = w_uk[l].astype(BF16).reshape(KV_LORA, N_HEADS * NOPE_DIM)
    wuv = w_uv[l].astype(BF16).reshape(KV_LORA, N_HEADS * V_DIM)
    wuvt2 = wuv.reshape(KV_LORA, N_HEADS // 2, 2 * V_DIM).transpose(1, 2, 0)
    pw = pool_w[l].astype(BF16)
    wpo = w_pool_out[l].astype(BF16)
    wmo = w_mla_out[l].astype(BF16)
    wo = w_out[l].astype(BF16)
    wup = w_up[l].astype(BF16)
    wdn = w_down[l].astype(BF16)
    g1 = norm1_g[l][None]
    qg = q_norm_g[l][None]
    kvg = kv_norm_g[l][None]
    g2 = norm2_g[l][None]
    fg = final_g[None]
    ps = pool_scale[l][None]
    cb = conv_b[l][None]
    cw = conv_w[l]

    tm_in, tm_head, tm_merge, tm_ffn = TM_IN_PROJ, TM_QKV, TM_MERGE, TM_FFN
    xp = x_prompt.reshape(bp * sp, D_MODEL)
    cos_p, sin_p = _rope_tables(0, sp, sp)
    z_p, cq_p, ckv_p, kr_p, krt_p, gates_p = _in_proj(
        xp, g1, w_all, qg, kvg, cos_p, sin_p, bp, sp, tm_in)
    qt_p, k_p, vt_p = _qkv_proj(cq_p, cos_p.T, sin_p.T, wqt, ckv_p, kr_p, wuk, wuvt2, bp, sp, tm_head)
    att_p = _flash(qt_p, k_p, vt_p)
    pool0 = jnp.zeros((bp, POOL_HALO, D_POOL), F32)
    h_p = _merge(z_p, pool0, att_p, gates_p, xp, pw, ps, wpo, wmo, wo, sp, 0, tm_merge)
    conv0 = jnp.zeros((bp, CONV_W - 1, D_FF), F32)
    y_p, tail_p = _ffn(h_p, g2, wup, cw, cb, conv0, wdn, fg, sp, tm_ffn)

    ns = bs * ss
    xs = x_sample.reshape(ns, D_MODEL)
    cos_s, sin_s = _rope_tables(past, ss, ns)
    z_s, cq_s, ckv_s, _, krt_s, gates_s = _in_proj(
        xs, g1, w_all, qg, kvg, cos_s, sin_s, bs, ss, tm_in)
    q_s = _q_proj(cq_s, cos_s, sin_s, wqt, bs, ss, ns)
    cache_krt = jnp.swapaxes(cache_krope[l], 1, 2)
    att_s = _decode_attn(q_s, wuk, wuv, cache_ckv[l], cache_krt, ckv_s, krt_s)
    pool_pre = jnp.pad(state_pool[l], ((0, 0), (POOL_HALO - POOL_STATE, 0), (0, 0)))
    h_s = _merge(z_s, pool_pre, att_s, gates_s, xs, pw, ps, wpo, wmo, wo, ss, past, tm_merge)
    y_s, tail_s = _ffn(h_s, g2, wup, cw, cb, state_conv[l], wdn, fg, ss, ns)

    tiles_per_seq = sp // tm_ffn
    p_conv = tail_p.reshape(bp, tiles_per_seq, CONV_W - 1, D_FF)[:, -1]
    s_conv = tail_s.reshape(bs, CONV_W - 1, D_FF)
    return (
        y_p.reshape(bp, sp, D_MODEL),
        y_s.reshape(bs, ss, D_MODEL),
        ckv_p.reshape(1, bp, sp, KV_LORA),
        jnp.swapaxes(krt_p, 1, 2)[None],
        z_p.reshape(bp, sp, D_POOL)[:, sp - POOL_STATE:][None],
        p_conv[None],
        ckv_s.reshape(1, bs, ss, KV_LORA),
        jnp.swapaxes(krt_s, 1, 2)[None],
        z_s.reshape(bs, ss, D_POOL)[:, ss - POOL_STATE:][None],
        s_conv[None],
    )
```

```python
import functools

import jax
import jax.numpy as jnp
import numpy as np
from jax import lax
from jax.experimental import pallas as pl
from jax.experimental.pallas import tpu as pltpu

D_MODEL = 2048
CHUNK = 64
D_POOL = 1024
POOL_WINDOWS = (2, 4, 8, 16)
POOL_GROUP = D_POOL // len(POOL_WINDOWS)
POOL_STATE = max(POOL_WINDOWS) - 1
POOL_HALO = 8 * (max(POOL_WINDOWS).bit_length() - 1)
N_HEADS = 16
Q_LORA = 512
KV_LORA = 512
NOPE_DIM = 128
ROPE_DIM = 64
V_DIM = 128
V_AUG = V_DIM + 16
QK_DIM = NOPE_DIM + ROPE_DIM
ATTN_SCALE = QK_DIM ** -0.5
LOG2E = 1.4426950408889634
Q_SCALE = ATTN_SCALE * LOG2E
ROPE_BASE = 10000.0
D_FF = 6144
CONV_W = 3
CONV_HALO = 8
FFN_ROW_CHUNKS = 2
EPS = 1e-6
MASK_VALUE = -1e30

COL_Q = D_POOL
COL_KV = COL_Q + Q_LORA
COL_GATE = COL_KV + KV_LORA
COL_ROPE = COL_GATE + 2 * D_MODEL
W_IN_BLOCK = 256
W_IN_COLS = COL_ROPE + W_IN_BLOCK
W_IN_BLOCKS = W_IN_COLS // W_IN_BLOCK

V7X_VMEM_LIMIT = 60000 * 1024

TM_IN_PROJ = 256
TM_QKV = 512
TM_MERGE = 256
TM_FFN = 512
TF_FFN = 1024
FLASH_HEADS = 4
DECODE_KEY_TILE = 1024

F32 = jnp.float32
BF16 = jnp.bfloat16


def _rms(x, g):
    return x * lax.rsqrt(jnp.mean(x * x, axis=-1, keepdims=True) + EPS) * g


def _dot(a, b):
    return jnp.dot(a, b, preferred_element_type=F32)


def _dot_nt(a, b):
    return lax.dot_general(a, b, (((1,), (1,)), ((), ())), preferred_element_type=F32)


def _dot_tn(a, b):
    return lax.dot_general(a, b, (((0,), (0,)), ((), ())), preferred_element_type=F32)


def _resident(shape):
    zeros = (0,) * len(shape)
    return pl.BlockSpec(shape, lambda *_: zeros, pipeline_mode=pl.Buffered(1))


def _params(semantics):
    return pltpu.CompilerParams(dimension_semantics=semantics, vmem_limit_bytes=V7X_VMEM_LIMIT)


def _regroup_w_in_kernel(*refs):
    *in_refs, o_ref = refs
    rope = in_refs[0][...]
    half = ROPE_DIM // 2
    swapped = jnp.concatenate([rope[half:], rope[:half]], axis=0)
    is_rope_block = pl.program_id(0) == W_IN_BLOCKS - 1
    parts = [rope] + [jnp.where(is_rope_block, swapped if k % 2 else rope, in_refs[k][...])
                      for k in range(1, len(in_refs))]
    o_ref[...] = jnp.concatenate(parts, axis=0).T.astype(BF16)


def _regroup_w_in(wt):
    per_step = W_IN_BLOCK // ROPE_DIM
    main_steps, rope_block = COL_GATE // W_IN_BLOCK, COL_GATE // ROPE_DIM

    def source(k):
        def index(c):
            body = jnp.where(c < main_steps, per_step * c, rope_block + 1 + per_step * (c - main_steps)) + k
            return jnp.where(c < W_IN_BLOCKS - 1, body, rope_block), 0
        return index

    return pl.pallas_call(
        _regroup_w_in_kernel,
        grid=(W_IN_BLOCKS,),
        in_specs=[pl.BlockSpec((ROPE_DIM, D_MODEL), source(k)) for k in range(per_step)],
        out_specs=pl.BlockSpec((D_MODEL, W_IN_BLOCK), lambda c: (0, c)),
        out_shape=jax.ShapeDtypeStruct((D_MODEL, W_IN_COLS), BF16),
        compiler_params=_params(("arbitrary",)),
        name="regroup_w_in",
    )(*([wt] * per_step))


def _in_proj_kernel(x_ref, g1_ref, w_ref, qg_ref, kvg_ref, cos_ref, sin_ref,
                    cost_ref, sint_ref, z_ref, cq_ref, ckv_ref, kr_ref, krt_ref, gate_ref, *, n_seg, seg_len):
    u = _rms(x_ref[...], g1_ref[...]).astype(BF16)
    z_ref[...] = _dot(u, w_ref[:, 0:COL_Q])
    cq_ref[...] = _rms(_dot(u, w_ref[:, COL_Q:COL_KV]), qg_ref[...]).astype(BF16)
    ckv_ref[...] = _rms(_dot(u, w_ref[:, COL_KV:COL_GATE]), kvg_ref[...])
    r = _dot(u, w_ref[:, COL_ROPE:COL_ROPE + 2 * ROPE_DIM])
    kr_ref[...] = (r[:, :ROPE_DIM] * cos_ref[...] + r[:, ROPE_DIM:] * sin_ref[...]).astype(BF16)
    rt = r.T
    krt = rt[:ROPE_DIM] * cost_ref[...] + rt[ROPE_DIM:] * sint_ref[...]
    for s in range(n_seg):
        krt_ref[s] = krt[:, s * seg_len:(s + 1) * seg_len]
    step = 1024
    for lo in range(0, 2 * D_MODEL, step):
        logits = _dot(u, w_ref[:, COL_GATE + lo:COL_GATE + lo + step])
        gate_ref[:, lo:lo + step] = 1.0 / (1.0 + jnp.exp(-logits))


def _in_proj(x, g1, w_all, qg, kvg, cos_t, sin_t, batch, seq, tm):
    n = x.shape[0]
    table_blocks = cos_t.shape[0] // tm
    tiles_per_seq = max(seq // tm, 1)
    n_seg = max(tm // seq, 1)
    row = lambda i: (i, 0)
    tab = lambda i: (i % table_blocks, 0)
    tab_t = lambda i: (0, i % table_blocks)
    return pl.pallas_call(
        functools.partial(_in_proj_kernel, n_seg=n_seg, seg_len=tm // n_seg),
        grid=(n // tm,),
        in_specs=[
            pl.BlockSpec((tm, D_MODEL), row),
            _resident((1, D_MODEL)),
            _resident((D_MODEL, W_IN_COLS)),
            _resident((1, Q_LORA)),
            _resident((1, KV_LORA)),
            pl.BlockSpec((tm, ROPE_DIM), tab),
            pl.BlockSpec((tm, ROPE_DIM), tab),
            pl.BlockSpec((ROPE_DIM, tm), tab_t),
            pl.BlockSpec((ROPE_DIM, tm), tab_t),
        ],
        out_specs=[
            pl.BlockSpec((tm, D_POOL), row),
            pl.BlockSpec((tm, Q_LORA), row),
            pl.BlockSpec((tm, KV_LORA), row),
            pl.BlockSpec((tm, ROPE_DIM), row),
            pl.BlockSpec((n_seg, ROPE_DIM, tm // n_seg), lambda i: (i // tiles_per_seq, 0, i % tiles_per_seq)),
            pl.BlockSpec((tm, 2 * D_MODEL), row),
        ],
        out_shape=[
            jax.ShapeDtypeStruct((n, D_POOL), F32),
            jax.ShapeDtypeStruct((n, Q_LORA), BF16),
            jax.ShapeDtypeStruct((n, KV_LORA), F32),
            jax.ShapeDtypeStruct((n, ROPE_DIM), BF16),
            jax.ShapeDtypeStruct((batch, ROPE_DIM, seq), F32),
            jax.ShapeDtypeStruct((n, 2 * D_MODEL), F32),
        ],
        compiler_params=_params(("arbitrary",)),
        name="in_proj",
    )(x, g1, w_all, qg, kvg, cos_t, sin_t, cos_t.T, sin_t.T)


def _q_proj_kernel(cq_ref, cos_ref, sin_ref, wqt_ref, q_ref, *, n_seg, seg_len):
    cq = cq_ref[...]
    cos = cos_ref[...] * Q_SCALE
    sin = sin_ref[...] * Q_SCALE

    def head(h, carry):
        res = _dot_nt(cq, wqt_ref[h])
        qn = (res[:, :NOPE_DIM] * Q_SCALE).astype(BF16)
        qr = (res[:, NOPE_DIM:QK_DIM] * cos + res[:, QK_DIM:] * sin).astype(BF16)
        for s in range(n_seg):
            rows = slice(s * seg_len, (s + 1) * seg_len)
            q_ref[s, h, :, 0:NOPE_DIM] = qn[rows]
            q_ref[s, h, :, NOPE_DIM:QK_DIM] = qr[rows]
        return carry

    lax.fori_loop(0, N_HEADS, head, 0)


def _q_proj(cq, cos_t, sin_t, wqt, batch, seq, tm):
    n = cq.shape[0]
    n_seg = tm // seq
    table_blocks = cos_t.shape[0] // tm
    return pl.pallas_call(
        functools.partial(_q_proj_kernel, n_seg=n_seg, seg_len=seq),
        grid=(n // tm,),
        in_specs=[
            pl.BlockSpec((tm, Q_LORA), lambda i: (i, 0)),
            pl.BlockSpec((tm, ROPE_DIM), lambda i: (i % table_blocks, 0)),
            pl.BlockSpec((tm, ROPE_DIM), lambda i: (i % table_blocks, 0)),
            _resident((N_HEADS, 2 * NOPE_DIM, Q_LORA)),
        ],
        out_specs=pl.BlockSpec((n_seg, N_HEADS, seq, QK_DIM), lambda i: (i, 0, 0, 0)),
        out_shape=jax.ShapeDtypeStruct((batch, N_HEADS, seq, QK_DIM), BF16),
        compiler_params=_params(("arbitrary",)),
        name="q_proj",
    )(cq, cos_t, sin_t, wqt)


def _qkv_proj_kernel(cq_ref, cost_ref, sint_ref, wqt_ref, ckv_ref, kr_ref, wuk_ref, wuvt_ref,
                     qt_ref, k_ref, vt_ref):
    cq = cq_ref[...]
    cos_t = cost_ref[...] * Q_SCALE
    sin_t = sint_ref[...] * Q_SCALE
    ckv = ckv_ref[...].astype(BF16)
    krb = kr_ref[...].astype(BF16)
    ones = jnp.ones((V_AUG - V_DIM, ckv.shape[0]), BF16)

    for hp in range(N_HEADS // 2):
        for hh in range(2):
            h = 2 * hp + hh
            res = _dot_nt(wqt_ref[h], cq)
            qt_ref[0, h, 0:NOPE_DIM, :] = (res[0:NOPE_DIM] * Q_SCALE).astype(BF16)
            qt_ref[0, h, NOPE_DIM:QK_DIM, :] = (
                res[NOPE_DIM:QK_DIM] * cos_t + res[QK_DIM:] * sin_t).astype(BF16)
        pair_cols = slice(2 * hp * NOPE_DIM, 2 * (hp + 1) * NOPE_DIM)
        kn = _dot(ckv, wuk_ref[:, pair_cols]).astype(BF16)
        vt = _dot_nt(wuvt_ref[hp], ckv).astype(BF16)
        for hh in range(2):
            h = 2 * hp + hh
            k_ref[0, h, 0, :, 0:NOPE_DIM] = kn[:, hh * NOPE_DIM:(hh + 1) * NOPE_DIM]
            k_ref[0, h, 0, :, NOPE_DIM:QK_DIM] = krb
            vt_ref[0, h, 0, 0:V_DIM, :] = vt[hh * V_DIM:(hh + 1) * V_DIM]
            vt_ref[0, h, 0, V_DIM:V_AUG, :] = ones


def _qkv_proj(cq, cos_tt, sin_tt, wqt, ckv, kr, wuk, wuvt2, batch, seq, tm):
    n = cq.shape[0]
    tiles_per_seq = seq // tm
    row = lambda i: (i, 0)
    tab = lambda i: (0, i % tiles_per_seq)
    tile = lambda i: (i // tiles_per_seq, 0, i % tiles_per_seq, 0, 0)
    return pl.pallas_call(
        _qkv_proj_kernel,
        grid=(n // tm,),
        in_specs=[
            pl.BlockSpec((tm, Q_LORA), row),
            pl.BlockSpec((ROPE_DIM, tm), tab),
            pl.BlockSpec((ROPE_DIM, tm), tab),
            _resident((N_HEADS, 2 * NOPE_DIM, Q_LORA)),
            pl.BlockSpec((tm, KV_LORA), row),
            pl.BlockSpec((tm, ROPE_DIM), row),
            _resident((KV_LORA, N_HEADS * NOPE_DIM)),
            _resident((N_HEADS // 2, 2 * V_DIM, KV_LORA)),
        ],
        out_specs=[
            pl.BlockSpec((1, N_HEADS, QK_DIM, tm), lambda i: (i // tiles_per_seq, 0, 0, i % tiles_per_seq)),
            pl.BlockSpec((1, N_HEADS, 1, tm, QK_DIM), tile),
            pl.BlockSpec((1, N_HEADS, 1, V_AUG, tm), tile),
        ],
        out_shape=[
            jax.ShapeDtypeStruct((batch, N_HEADS, QK_DIM, seq), BF16),
            jax.ShapeDtypeStruct((batch, N_HEADS, tiles_per_seq, tm, QK_DIM), BF16),
            jax.ShapeDtypeStruct((batch, N_HEADS, tiles_per_seq, V_AUG, tm), BF16),
        ],
        compiler_params=_params(("arbitrary",)),
        name="qkv_proj",
    )(cq, cos_tt, sin_tt, wqt, ckv, kr, wuk, wuvt2)


def _flash_kernel(*refs, heads, tile, n_cast):
    (qt_ref, qtn_ref, knew_ref, vtnew_ref), refs = refs[:4], refs[4:]
    cast_in, o_ref, cast_out = refs[:n_cast], refs[n_cast], refs[n_cast + 1:2 * n_cast + 1]
    for src, dst in zip(cast_in, cast_out):
        dst[...] = src[0].astype(BF16)
    _flash_body(qt_ref, qtn_ref, knew_ref, vtnew_ref, o_ref, *refs[2 * n_cast + 1:], heads=heads, tile=tile)


def _flash_body(qt_ref, qtn_ref, knew_ref, vtnew_ref, o_ref,
                k_ref, vt_ref, s_ref, sf_ref, acc_ref, m_ref, cmax_ref, cmaxf_ref, bias_ref, *, heads, tile):
    qi = pl.program_id(2)
    for g in range(heads):
        k_ref[g, qi] = knew_ref[0, g, 0]
        vt_ref[g, qi] = vtnew_ref[0, g, 0]

    @pl.when((pl.program_id(0) == 0) & (pl.program_id(1) == 0) & (qi == 0))
    def _():
        key_chunk = lax.broadcasted_iota(jnp.int32, (tile, tile), 0) // CHUNK
        query_chunk = lax.broadcasted_iota(jnp.int32, (tile, tile), 1) // CHUNK
        bias_ref[...] = jnp.where(key_chunk <= query_chunk, 0.0, MASK_VALUE)

    def scores(g, j, slot):
        st = _dot(k_ref[g, j], qt_ref[0, g])
        s_ref[slot, g] = st
        cmax_ref[slot, g] = jnp.max(st, axis=0, keepdims=True)

    def next_first_scores(g):
        st = _dot(k_ref[g, 0], qtn_ref[0, g])
        sf_ref[g] = st
        cmaxf_ref[g] = jnp.max(st, axis=0, keepdims=True)

    def softmax(g, st, tile_max):
        m_old = m_ref[g]
        m_new = jnp.maximum(m_old, tile_max)
        m_ref[g] = m_new
        return jnp.exp2(st - m_new).astype(BF16), jnp.exp2(m_old - m_new)

    def values(g, j, p, alpha):
        acc_ref[g] = alpha * acc_ref[g] + _dot(vt_ref[g, j], p)

    def steps(j0, count, slot0, from_first=False):
        pending = None
        for t in range(count):
            cur = (slot0 + t) % 2
            for g in range(heads):
                if from_first and t == 0:
                    p, alpha = softmax(g, sf_ref[g], cmaxf_ref[g])
                else:
                    p, alpha = softmax(g, s_ref[cur, g], cmax_ref[cur, g])
                scores(g, j0 + t + 1, 1 - cur)
                if pending is not None:
                    values(*pending)
                pending = (g, j0 + t, p, alpha)
        values(*pending)

    for g in range(heads):
        m_ref[g] = jnp.full((1, tile), MASK_VALUE, F32)
        acc_ref[g] = jnp.zeros((V_AUG, tile), F32)

    @pl.when(qi == 0)
    def _():
        for g in range(heads):
            scores(g, 0, 0)

    @pl.when(qi >= 1)
    def _():
        steps(0, 1, 0, from_first=True)

    rest = jnp.maximum(qi - 1, 0)

    def four_steps(jj, carry):
        steps(1 + 4 * jj, 4, 1)
        return carry

    lax.fori_loop(0, rest // 4, four_steps, 0)

    @pl.when(rest % 4 >= 2)
    def _():
        steps(1 + 4 * (rest // 4), 2, 1)

    @pl.when(rest % 2 == 1)
    def _():
        steps(qi - 1, 1, 1)

    def finish(g, p, alpha):
        acc = alpha * acc_ref[g] + _dot(vt_ref[g, qi], p)
        o = acc[0:V_DIM] / acc[V_DIM:V_DIM + 1]
        o_ref[:, g * V_DIM:(g + 1) * V_DIM] = o.T.astype(BF16)

    diag_slot = qi % 2
    pending = None
    for g in range(heads):
        st = s_ref[diag_slot, g] + bias_ref[...]
        p, alpha = softmax(g, st, jnp.max(st, axis=0, keepdims=True))
        next_first_scores(g)
        if pending is not None:
            finish(*pending)
        pending = (g, p, alpha)
    finish(*pending)


def _flash(qt, k, vt, cast=(), heads=FLASH_HEADS):
    batch, _, _, seq = qt.shape
    n_tiles, tile = k.shape[2], k.shape[3]
    groups = N_HEADS // heads
    n_steps = batch * groups * n_tiles
    slab = lambda b, hg, qi: (b * groups + hg) * n_tiles + qi
    cast_in = [pl.BlockSpec((1, w.shape[1] // n_steps, w.shape[2]), lambda b, hg, qi: (0, slab(b, hg, qi), 0))
               for w in cast]
    cast_out = [pl.BlockSpec((w.shape[1] // n_steps, w.shape[2]), lambda b, hg, qi: (slab(b, hg, qi), 0))
                for w in cast]
    cast_shape = [jax.ShapeDtypeStruct(w.shape[1:], BF16) for w in cast]
    return pl.pallas_call(
        functools.partial(_flash_kernel, heads=heads, tile=tile, n_cast=len(cast)),
        grid=(batch, groups, n_tiles),
        in_specs=[
            pl.BlockSpec((1, heads, QK_DIM, tile), lambda b, hg, qi: (b, hg, 0, qi)),
            pl.BlockSpec((1, heads, QK_DIM, tile),
                         lambda b, hg, qi: (b, hg, 0, jnp.minimum(qi + 1, n_tiles - 1))),
            pl.BlockSpec((1, heads, 1, tile, QK_DIM), lambda b, hg, qi: (b, hg, qi, 0, 0)),
            pl.BlockSpec((1, heads, 1, V_AUG, tile), lambda b, hg, qi: (b, hg, qi, 0, 0)),
        ] + cast_in,
        out_specs=[pl.BlockSpec((tile, heads * V_DIM), lambda b, hg, qi: (b * n_tiles + qi, hg))] + cast_out,
        out_shape=[jax.ShapeDtypeStruct((batch * seq, N_HEADS * V_DIM), BF16)] + cast_shape,
        scratch_shapes=[
            pltpu.VMEM((heads, n_tiles, tile, QK_DIM), BF16),
            pltpu.VMEM((heads, n_tiles, V_AUG, tile), BF16),
            pltpu.VMEM((2, heads, tile, tile), F32),
            pltpu.VMEM((heads, tile, tile), F32),
            pltpu.VMEM((heads, V_AUG, tile), F32),
            pltpu.VMEM((heads, 1, tile), F32),
            pltpu.VMEM((2, heads, 1, tile), F32),
            pltpu.VMEM((heads, 1, tile), F32),
            pltpu.VMEM((tile, tile), F32),
        ],
        compiler_params=_params(("arbitrary", "arbitrary", "arbitrary")),
        name="flash",
    )(qt, qt, k, vt, *cast)


def _decode_kernel(q_ref, wuk_ref, wuv_ref, cc_ref, ckr_ref, cn_ref, krn_ref, o_ref,
                   qt, s_ref, m_s, l_s, acc_s, *, seq, halves):
    kt = pl.program_id(1)
    half = cc_ref.shape[1] // halves

    @pl.when(kt == 0)
    def _():
        eye = (lax.broadcasted_iota(jnp.int32, (ROPE_DIM, ROPE_DIM), 0)
               == lax.broadcasted_iota(jnp.int32, (ROPE_DIM, ROPE_DIM), 1)).astype(BF16)
        for hp in range(N_HEADS // 2):
            lat, rope_t = [], []
            for h in (2 * hp, 2 * hp + 1):
                w_h = wuk_ref[:, h * NOPE_DIM:(h + 1) * NOPE_DIM]
                lat.append(_dot_nt(w_h, q_ref[0, h, :, 0:NOPE_DIM]))
                rope_t.append(_dot_nt(eye, q_ref[0, h, :, NOPE_DIM:QK_DIM]))
            cols = slice(2 * hp * seq, 2 * (hp + 1) * seq)
            qt[0:KV_LORA, cols] = jnp.concatenate(lat, axis=1).astype(BF16)
            qt[KV_LORA:, cols] = jnp.concatenate(rope_t, axis=1).astype(BF16)
        m_s[...] = jnp.full(m_s.shape, MASK_VALUE, F32)
        l_s[...] = jnp.zeros(l_s.shape, F32)
        acc_s[...] = jnp.zeros(acc_s.shape, F32)

    def scores(kc, krt, slot):
        n = kc.shape[0]
        s_ref[slot, 0:n, :] = _dot(kc, qt[0:KV_LORA, :]) + _dot_tn(krt, qt[KV_LORA:, :])

    def update(kc, slot):
        n = kc.shape[0]
        st = s_ref[slot, 0:n, :]
        m_old = m_s[...]
        m_new = jnp.maximum(m_old, jnp.max(st, axis=0, keepdims=True))
        alpha = jnp.exp2(m_old - m_new)
        p = jnp.exp2(st - m_new)
        l_s[...] = alpha * l_s[...] + jnp.sum(p, axis=0, keepdims=True)
        acc_s[...] = alpha * acc_s[...] + _dot_tn(kc, p.astype(BF16))
        m_s[...] = m_new

    parts = [cc_ref[0, i * half:(i + 1) * half, :].astype(BF16) for i in range(halves)]
    for i in range(halves):
        scores(parts[i], ckr_ref[0, :, i * half:(i + 1) * half].astype(BF16), i)
    for i in range(halves):
        update(parts[i], i)

    @pl.when(kt == pl.num_programs(1) - 1)
    def _():
        new = cn_ref[...].astype(BF16)
        scores(new, krn_ref[0].astype(BF16), 0)
        update(new, 0)
        o_lat = (acc_s[...] / l_s[...]).T.astype(BF16)
        for h in range(N_HEADS):
            o_h = _dot(o_lat[h * seq:(h + 1) * seq], wuv_ref[:, h * V_DIM:(h + 1) * V_DIM])
            o_ref[:, h * V_DIM:(h + 1) * V_DIM] = o_h.astype(BF16)


def _decode_attn(q, wuk, wuv, cache_ckv, cache_kr, ckv_new, kr_new, key_tile=DECODE_KEY_TILE, halves=2):
    batch, _, seq, _ = q.shape
    past = cache_ckv.shape[1]
    cols = N_HEADS * seq
    return pl.pallas_call(
        functools.partial(_decode_kernel, seq=seq, halves=halves),
        grid=(batch, past // key_tile),
        in_specs=[
            pl.BlockSpec((1, N_HEADS, seq, QK_DIM), lambda b, kt: (b, 0, 0, 0)),
            _resident((KV_LORA, N_HEADS * NOPE_DIM)),
            _resident((KV_LORA, N_HEADS * V_DIM)),
            pl.BlockSpec((1, key_tile, KV_LORA), lambda b, kt: (b, kt, 0)),
            pl.BlockSpec((1, ROPE_DIM, key_tile), lambda b, kt: (b, 0, kt)),
            pl.BlockSpec((seq, KV_LORA), lambda b, kt: (b, 0)),
            pl.BlockSpec((1, ROPE_DIM, seq), lambda b, kt: (b, 0, 0)),
        ],
        out_specs=pl.BlockSpec((seq, N_HEADS * V_DIM), lambda b, kt: (b, 0)),
        out_shape=jax.ShapeDtypeStruct((batch * seq, N_HEADS * V_DIM), BF16),
        scratch_shapes=[
            pltpu.VMEM((KV_LORA + ROPE_DIM, cols), BF16),
            pltpu.VMEM((halves, key_tile // halves, cols), F32),
            pltpu.VMEM((1, cols), F32),
            pltpu.VMEM((1, cols), F32),
            pltpu.VMEM((KV_LORA, cols), F32),
        ],
        compiler_params=_params(("arbitrary", "arbitrary")),
        name="decode_attn",
    )(q, wuk, wuv, cache_ckv, cache_kr, ckv_new, kr_new)


def _window_sum(ext, tmp, cols, w):
    end = ext.shape[0]
    levels = w.bit_length() - 1
    for k in range(levels):
        lo, shift = 8 * (k + 1), 2 ** k
        if k == 0:
            val = ext[lo:end, cols] + ext[lo - shift:end - shift, cols]
        else:
            below = tmp.at[(k - 1) % 2]
            val = below[lo:end, :] + below[lo - shift:end - shift, :]
        if k < levels - 1:
            tmp[k % 2, lo:end, :] = val
    return val[POOL_HALO - 8 * levels:]


def _merge_kernel(z_ref, zprev_ref, pre_ref, att_ref, gate_ref, x_ref, pw_ref, ps_ref,
                  wpo_ref, wmo_ref, wo_ref, h_ref, ext, tmp, ypool, *, n_seg, seg_len, tiles_per_seq, pos0):
    t = pl.program_id(0) % tiles_per_seq
    row = lax.broadcasted_iota(jnp.int32, (seg_len, 1), 0)
    pos = pos0 + t * seg_len + row
    br_b = _dot(att_ref[...], wmo_ref[...])
    for s in range(n_seg):
        if tiles_per_seq == 1:
            halo = pre_ref[s]
        else:
            halo = jnp.where(t == 0, pre_ref[0], zprev_ref[...])
        ext[0:POOL_HALO, :] = halo
        ext[POOL_HALO:, :] = z_ref[s * seg_len:(s + 1) * seg_len, :]
        for g, w in enumerate(POOL_WINDOWS):
            cols = slice(g * POOL_GROUP, (g + 1) * POOL_GROUP)
            cur = ext[POOL_HALO:, cols]
            total = _window_sum(ext, tmp, cols, w)
            count = jnp.minimum(pos + 1, w).astype(F32)
            diff = total / count - cur
            y = _dot(diff.astype(BF16), pw_ref[g]) * ps_ref[:, cols]
            ypool[s * seg_len:(s + 1) * seg_len, cols] = y.astype(BF16)
    br_a = _dot(ypool[...], wpo_ref[...])
    merged = gate_ref[:, 0:D_MODEL] * br_a + gate_ref[:, D_MODEL:] * br_b
    h_ref[...] = x_ref[...] + _dot(merged.astype(BF16), wo_ref[...])


def _merge(z, prefix, att, gates, x, pool_w, pool_scale, w_pool_out, w_mla_out, w_out, seq, pos0, tm):
    n = x.shape[0]
    tiles_per_seq = max(seq // tm, 1)
    n_seg = max(tm // seq, 1)
    seg_len = tm // n_seg
    halo_blocks = tm // POOL_HALO
    row = lambda i: (i, 0)
    return pl.pallas_call(
        functools.partial(_merge_kernel, n_seg=n_seg, seg_len=seg_len,
                          tiles_per_seq=tiles_per_seq, pos0=pos0),
        grid=(n // tm,),
        in_specs=[
            pl.BlockSpec((tm, D_POOL), row),
            pl.BlockSpec((POOL_HALO, D_POOL), lambda i: (jnp.maximum(i * halo_blocks - 1, 0), 0)),
            pl.BlockSpec((n_seg, POOL_HALO, D_POOL), lambda i: (i // tiles_per_seq, 0, 0)),
            pl.BlockSpec((tm, N_HEADS * V_DIM), row),
            pl.BlockSpec((tm, 2 * D_MODEL), row),
            pl.BlockSpec((tm, D_MODEL), row),
            _resident((len(POOL_WINDOWS), POOL_GROUP, POOL_GROUP)),
            _resident((1, D_POOL)),
            _resident((D_POOL, D_MODEL)),
            _resident((N_HEADS * V_DIM, D_MODEL)),
            _resident((D_MODEL, D_MODEL)),
        ],
        out_specs=pl.BlockSpec((tm, D_MODEL), row),
        out_shape=jax.ShapeDtypeStruct((n, D_MODEL), F32),
        scratch_shapes=[
            pltpu.VMEM((POOL_HALO + seg_len, D_POOL), F32),
            pltpu.VMEM((2, POOL_HALO + seg_len, POOL_GROUP), F32),
            pltpu.VMEM((tm, D_POOL), BF16),
        ],
        compiler_params=_params(("arbitrary",)),
        name="merge",
    )(z, z, prefix, att, gates, x, pool_w, pool_scale, w_pool_out, w_mla_out, w_out)


def _ffn_kernel(h_ref, g2_ref, wa_ref, wb_ref, cw_ref, cb_ref, pre_ref, wd_ref, fg_ref,
                y_ref, tail_ref, hn, ext, carry, *, n_seg, seg_len, tiles_per_seq):
    i = pl.program_id(0)
    j = pl.program_id(1)
    t = i % tiles_per_seq

    @pl.when(j == 0)
    def _():
        hn[...] = _rms(h_ref[...], g2_ref[...]).astype(BF16)
        y_ref[...] = jnp.zeros(y_ref.shape, F32)

    if tiles_per_seq > 1:
        @pl.when(t == 0)
        def _():
            carry[j] = pre_ref[0]

    tm = hn.shape[0]
    rc = tm // FFN_ROW_CHUNKS
    up = []
    for c in range(FFN_ROW_CHUNKS):
        hc = hn[c * rc:(c + 1) * rc, :]
        up.append((_dot(hc, wa_ref[...]), _dot(hc, wb_ref[...])))

    def conv_gelu(seg, lo, a_rows, b_rows):
        n = a_rows.shape[0]
        base = CONV_HALO + lo
        ext[seg, base:base + n, :] = a_rows
        c = cb_ref[...] + ext[seg, base - 2:base - 2 + n, :] * cw_ref[0:1, :]
        c = c + ext[seg, base - 1:base - 1 + n, :] * cw_ref[1:2, :]
        c = c + a_rows * cw_ref[2:3, :]
        gelu = 0.5 * c * (1.0 + lax.erf(c * (2.0 ** -0.5)))
        return (gelu * b_rows).astype(BF16)

    for c in range(FFN_ROW_CHUNKS):
        a_c, b_c = up[c]
        if n_seg == 1:
            if c == 0:
                ext[0, CONV_HALO - (CONV_W - 1):CONV_HALO, :] = carry[j] if tiles_per_seq > 1 else pre_ref[0]
            gated_c = conv_gelu(0, c * rc, a_c, b_c)
            if c == FFN_ROW_CHUNKS - 1:
                tail_rows = a_c[rc - (CONV_W - 1):]
                tail_ref[0, 0] = tail_rows
                if tiles_per_seq > 1:
                    carry[j] = tail_rows
        else:
            per_chunk = n_seg // FFN_ROW_CHUNKS
            parts = []
            for k in range(per_chunk):
                s = c * per_chunk + k
                rows = slice(k * seg_len, (k + 1) * seg_len)
                ext[s, CONV_HALO - (CONV_W - 1):CONV_HALO, :] = pre_ref[s]
                parts.append(conv_gelu(s, 0, a_c[rows], b_c[rows]))
                tail_ref[0, s] = a_c[rows][seg_len - (CONV_W - 1):]
            gated_c = jnp.concatenate(parts, axis=0)
        y_ref[c * rc:(c + 1) * rc, :] += _dot(gated_c, wd_ref[...])

    @pl.when(j == pl.num_programs(1) - 1)
    def _():
        y_ref[...] = _rms(h_ref[...] + y_ref[...], fg_ref[...])


def _ffn(h, norm2_g, w_up, conv_w, conv_b, prefix, w_down, final_g, seq, tm, tf=TF_FFN):
    n = h.shape[0]
    tiles_per_seq = max(seq // tm, 1)
    n_seg = max(tm // seq, 1)
    seg_len = tm // n_seg
    nff = D_FF // tf
    return pl.pallas_call(
        functools.partial(_ffn_kernel, n_seg=n_seg, seg_len=seg_len, tiles_per_seq=tiles_per_seq),
        grid=(n // tm, nff),
        in_specs=[
            pl.BlockSpec((tm, D_MODEL), lambda i, j: (i, 0)),
            _resident((1, D_MODEL)),
            pl.BlockSpec((D_MODEL, tf), lambda i, j: (0, j)),
            pl.BlockSpec((D_MODEL, tf), lambda i, j: (0, nff + j)),
            pl.BlockSpec((CONV_W, tf), lambda i, j: (0, j)),
            pl.BlockSpec((1, tf), lambda i, j: (0, j)),
            pl.BlockSpec((n_seg, CONV_W - 1, tf), lambda i, j: (i // tiles_per_seq, 0, j)),
            pl.BlockSpec((tf, D_MODEL), lambda i, j: (j, 0)),
            _resident((1, D_MODEL)),
        ],
        out_specs=[
            pl.BlockSpec((tm, D_MODEL), lambda i, j: (i, 0)),
            pl.BlockSpec((1, n_seg, CONV_W - 1, tf), lambda i, j: (i, 0, 0, j)),
        ],
        out_shape=[
            jax.ShapeDtypeStruct((n, D_MODEL), F32),
            jax.ShapeDtypeStruct((n // tm, n_seg, CONV_W - 1, D_FF), F32),
        ],
        scratch_shapes=[
            pltpu.VMEM((tm, D_MODEL), BF16),
            pltpu.VMEM((n_seg, CONV_HALO + seg_len, tf), F32),
            pltpu.VMEM((nff, CONV_W - 1, tf), F32),
        ],
        compiler_params=_params(("arbitrary", "arbitrary")),
        name="ffn",
    )(h, norm2_g, w_up, w_up, conv_w, conv_b, prefix, w_down, final_g)


def _rope_tables(pos0, length, rows):
    pos = (pos0 + np.arange(length)).astype(np.float64)
    inv = ROPE_BASE ** (-(np.arange(ROPE_DIM // 2, dtype=np.float64) * 2.0 / ROPE_DIM))
    ang = pos[:, None] * inv[None, :]
    cos, sin = np.cos(ang), np.sin(ang)
    cos_t = np.concatenate([cos, cos], axis=-1).astype(np.float32)
    sin_t = np.concatenate([-sin, sin], axis=-1).astype(np.float32)
    reps = max(rows // length, 1)
    return np.tile(cos_t, (reps, 1)), np.tile(sin_t, (reps, 1))


def _swap_halves(w):
    return jnp.concatenate([w[..., ROPE_DIM // 2:], w[..., :ROPE_DIM // 2]], axis=-1)


def kernel(x_prompt, x_sample, cache_ckv, cache_krope, state_pool, state_conv, norm1_g, w_in, pool_w, pool_scale, w_pool_out, q_norm_g, w_uq, kv_norm_g, w_uk, w_uv, w_mla_out, w_out, norm2_g, w_up, conv_w, conv_b, w_down, final_g):
    l = 0
    bp, sp, _ = x_prompt.shape
    bs, ss, _ = x_sample.shape
    past = cache_ckv.shape[2]

    w = w_in[l]
    w_all = _regroup_w_in(w.T)
    wq = w_uq[l].reshape(Q_LORA, N_HEADS, QK_DIM)
    wq = jnp.concatenate([wq, _swap_halves(wq[..., NOPE_DIM:])], axis=-1)
    wqt = wq.transpose(1, 2, 0).astype(BF16)
    wuk = w_uk[l].astype(BF16).reshape(KV_LORA, N_HEADS * NOPE_DIM)
    wuv = w_uv[l].astype(BF16).reshape(KV_LORA, N_HEADS * V_DIM)
    wuvt2 = wuv.reshape(KV_LORA, N_HEADS // 2, 2 * V_DIM).transpose(1, 2, 0)
    pw = pool_w[l].astype(BF16)
    wpo = w_pool_out[l].astype(BF16)
    g1 = norm1_g[l][None]
    qg = q_norm_g[l][None]
    kvg = kv_norm_g[l][None]
    g2 = norm2_g[l][None]
    fg = final_g[None]
    ps = pool_scale[l][None]
    cb = conv_b[l][None]
    cw = conv_w[l]

    tm_in, tm_head, tm_merge, tm_ffn = TM_IN_PROJ, TM_QKV, TM_MERGE, TM_FFN
    xp = x_prompt.reshape(bp * sp, D_MODEL)
    cos_p, sin_p = _rope_tables(0, sp, sp)
    z_p, cq_p, ckv_p, kr_p, krt_p, gates_p = _in_proj(
        xp, g1, w_all, qg, kvg, cos_p, sin_p, bp, sp, tm_in)
    qt_p, k_p, vt_p = _qkv_proj(cq_p, cos_p.T, sin_p.T, wqt, ckv_p, kr_p, wuk, wuvt2, bp, sp, tm_head)
    att_p, wmo, wo, wup, wdn = _flash(
        qt_p, k_p, vt_p, cast=(w_mla_out[l][None], w_out[l][None], w_up[l][None], w_down[l][None]))
    pool0 = jnp.zeros((bp, POOL_HALO, D_POOL), F32)
    h_p = _merge(z_p, pool0, att_p, gates_p, xp, pw, ps, wpo, wmo, wo, sp, 0, tm_merge)
    conv0 = jnp.zeros((bp, CONV_W - 1, D_FF), F32)
    y_p, tail_p = _ffn(h_p, g2, wup, cw, cb, conv0, wdn, fg, sp, tm_ffn)

    ns = bs * ss
    xs = x_sample.reshape(ns, D_MODEL)
    cos_s, sin_s = _rope_tables(past, ss, ns)
    z_s, cq_s, ckv_s, _, krt_s, gates_s = _in_proj(
        xs, g1, w_all, qg, kvg, cos_s, sin_s, bs, ss, tm_in)
    q_s = _q_proj(cq_s, cos_s, sin_s, wqt, bs, ss, ns)
    cache_krt = jnp.swapaxes(cache_krope[l], 1, 2)
    att_s = _decode_attn(q_s, wuk, wuv, cache_ckv[l], cache_krt, ckv_s, krt_s)
    pool_pre = jnp.pad(state_pool[l], ((0, 0), (POOL_HALO - POOL_STATE, 0), (0, 0)))
    h_s = _merge(z_s, pool_pre, att_s, gates_s, xs, pw, ps, wpo, wmo, wo, ss, past, tm_merge)
    y_s, tail_s = _ffn(h_s, g2, wup, cw, cb, state_conv[l], wdn, fg, ss, ns)

    tiles_per_seq = sp // tm_ffn
    p_conv = tail_p.reshape(bp, tiles_per_seq, CONV_W - 1, D_FF)[:, -1]
    s_conv = tail_s.reshape(bs, CONV_W - 1, D_FF)
    return (
        y_p.reshape(bp, sp, D_MODEL),
        y_s.reshape(bs, ss, D_MODEL),
        ckv_p.reshape(1, bp, sp, KV_LORA),
        jnp.swapaxes(krt_p, 1, 2)[None],
        z_p.reshape(bp, sp, D_POOL)[:, sp - POOL_STATE:][None],
        p_conv[None],
        ckv_s.reshape(1, bs, ss, KV_LORA),
        jnp.swapaxes(krt_s, 1, 2)[None],
        z_s.reshape(bs, ss, D_POOL)[:, ss - POOL_STATE:][None],
        s_conv[None],
    )
```

```python
import functools

import jax
import jax.numpy as jnp
import numpy as np
from jax import lax
from jax.experimental import pallas as pl
from jax.experimental.pallas import tpu as pltpu

D_MODEL = 2048
CHUNK = 64
D_POOL = 1024
POOL_WINDOWS = (2, 4, 8, 16)
POOL_GROUP = D_POOL // len(POOL_WINDOWS)
POOL_STATE = max(POOL_WINDOWS) - 1
POOL_HALO = 8 * (max(POOL_WINDOWS).bit_length() - 1)
N_HEADS = 16
Q_LORA = 512
KV_LORA = 512
NOPE_DIM = 128
ROPE_DIM = 64
V_DIM = 128
V_AUG = V_DIM + 16
QK_DIM = NOPE_DIM + ROPE_DIM
ATTN_SCALE = QK_DIM ** -0.5
LOG2E = 1.4426950408889634
Q_SCALE = ATTN_SCALE * LOG2E
ROPE_BASE = 10000.0
D_FF = 6144
CONV_W = 3
CONV_HALO = 8
FFN_ROW_CHUNKS = 2
EPS = 1e-6
MASK_VALUE = -1e30

COL_Q = D_POOL
COL_KV = COL_Q + Q_LORA
COL_GATE = COL_KV + KV_LORA
COL_ROPE = COL_GATE + 2 * D_MODEL
W_IN_BLOCK = 256
W_IN_COLS = COL_ROPE + W_IN_BLOCK
W_IN_BLOCKS = W_IN_COLS // W_IN_BLOCK

V7X_VMEM_LIMIT = 60000 * 1024

TM_IN_PROJ = 256
TM_QKV = 512
TM_MERGE = 256
TM_FFN = 512
TF_FFN = 1024
FLASH_HEADS = 4
DECODE_KEY_TILE = 1024

F32 = jnp.float32
BF16 = jnp.bfloat16


def _rms(x, g):
    return x * lax.rsqrt(jnp.mean(x * x, axis=-1, keepdims=True) + EPS) * g


def _dot(a, b):
    return jnp.dot(a, b, preferred_element_type=F32)


def _dot_nt(a, b):
    return lax.dot_general(a, b, (((1,), (1,)), ((), ())), preferred_element_type=F32)


def _dot_tn(a, b):
    return lax.dot_general(a, b, (((0,), (0,)), ((), ())), preferred_element_type=F32)


def _resident(shape):
    zeros = (0,) * len(shape)
    return pl.BlockSpec(shape, lambda *_: zeros, pipeline_mode=pl.Buffered(1))


def _params(semantics):
    return pltpu.CompilerParams(dimension_semantics=semantics, vmem_limit_bytes=V7X_VMEM_LIMIT)


def _regroup_w_in_kernel(*refs):
    *in_refs, o_ref = refs
    rope = in_refs[0][...]
    half = ROPE_DIM // 2
    swapped = jnp.concatenate([rope[half:], rope[:half]], axis=0)
    is_rope_block = pl.program_id(0) == W_IN_BLOCKS - 1
    parts = [rope] + [jnp.where(is_rope_block, swapped if k % 2 else rope, in_refs[k][...])
                      for k in range(1, len(in_refs))]
    o_ref[...] = jnp.concatenate(parts, axis=0).T.astype(BF16)


def _regroup_w_in(wt):
    per_step = W_IN_BLOCK // ROPE_DIM
    main_steps, rope_block = COL_GATE // W_IN_BLOCK, COL_GATE // ROPE_DIM

    def source(k):
        def index(c):
            body = jnp.where(c < main_steps, per_step * c, rope_block + 1 + per_step * (c - main_steps)) + k
            return jnp.where(c < W_IN_BLOCKS - 1, body, rope_block), 0
        return index

    return pl.pallas_call(
        _regroup_w_in_kernel,
        grid=(W_IN_BLOCKS,),
        in_specs=[pl.BlockSpec((ROPE_DIM, D_MODEL), source(k)) for k in range(per_step)],
        out_specs=pl.BlockSpec((D_MODEL, W_IN_BLOCK), lambda c: (0, c)),
        out_shape=jax.ShapeDtypeStruct((D_MODEL, W_IN_COLS), BF16),
        compiler_params=_params(("arbitrary",)),
        name="regroup_w_in",
    )(*([wt] * per_step))


def _in_proj_kernel(x_ref, g1_ref, w_ref, qg_ref, kvg_ref, cos_ref, sin_ref,
                    cost_ref, sint_ref, z_ref, cq_ref, ckv_ref, kr_ref, krt_ref, gate_ref, *, n_seg, seg_len):
    u = _rms(x_ref[...], g1_ref[...]).astype(BF16)
    z_ref[...] = _dot(u, w_ref[:, 0:COL_Q])
    cq_ref[...] = _rms(_dot(u, w_ref[:, COL_Q:COL_KV]), qg_ref[...]).astype(BF16)
    ckv_ref[...] = _rms(_dot(u, w_ref[:, COL_KV:COL_GATE]), kvg_ref[...])
    r = _dot(u, w_ref[:, COL_ROPE:COL_ROPE + 2 * ROPE_DIM])
    kr_ref[...] = (r[:, :ROPE_DIM] * cos_ref[...] + r[:, ROPE_DIM:] * sin_ref[...]).astype(BF16)
    rt = r.T
    krt = rt[:ROPE_DIM] * cost_ref[...] + rt[ROPE_DIM:] * sint_ref[...]
    for s in range(n_seg):
        krt_ref[s] = krt[:, s * seg_len:(s + 1) * seg_len]
    step = 1024
    for lo in range(0, 2 * D_MODEL, step):
        logits = _dot(u, w_ref[:, COL_GATE + lo:COL_GATE + lo + step])
        gate_ref[:, lo:lo + step] = (1.0 / (1.0 + jnp.exp(-logits))).astype(gate_ref.dtype)


def _in_proj(x, g1, w_all, qg, kvg, cos_t, sin_t, batch, seq, tm):
    n = x.shape[0]
    table_blocks = cos_t.shape[0] // tm
    tiles_per_seq = max(seq // tm, 1)
    n_seg = max(tm // seq, 1)
    row = lambda i: (i, 0)
    tab = lambda i: (i % table_blocks, 0)
    tab_t = lambda i: (0, i % table_blocks)
    return pl.pallas_call(
        functools.partial(_in_proj_kernel, n_seg=n_seg, seg_len=tm // n_seg),
        grid=(n // tm,),
        in_specs=[
            pl.BlockSpec((tm, D_MODEL), row),
            _resident((1, D_MODEL)),
            _resident((D_MODEL, W_IN_COLS)),
            _resident((1, Q_LORA)),
            _resident((1, KV_LORA)),
            pl.BlockSpec((tm, ROPE_DIM), tab),
            pl.BlockSpec((tm, ROPE_DIM), tab),
            pl.BlockSpec((ROPE_DIM, tm), tab_t),
            pl.BlockSpec((ROPE_DIM, tm), tab_t),
        ],
        out_specs=[
            pl.BlockSpec((tm, D_POOL), row),
            pl.BlockSpec((tm, Q_LORA), row),
            pl.BlockSpec((tm, KV_LORA), row),
            pl.BlockSpec((tm, ROPE_DIM), row),
            pl.BlockSpec((n_seg, ROPE_DIM, tm // n_seg), lambda i: (i // tiles_per_seq, 0, i % tiles_per_seq)),
            pl.BlockSpec((tm, 2 * D_MODEL), row),
        ],
        out_shape=[
            jax.ShapeDtypeStruct((n, D_POOL), F32),
            jax.ShapeDtypeStruct((n, Q_LORA), BF16),
            jax.ShapeDtypeStruct((n, KV_LORA), F32),
            jax.ShapeDtypeStruct((n, ROPE_DIM), BF16),
            jax.ShapeDtypeStruct((batch, ROPE_DIM, seq), F32),
            jax.ShapeDtypeStruct((n, 2 * D_MODEL), BF16),
        ],
        compiler_params=_params(("arbitrary",)),
        name="in_proj",
    )(x, g1, w_all, qg, kvg, cos_t, sin_t, cos_t.T, sin_t.T)


def _q_proj_kernel(cq_ref, cos_ref, sin_ref, wqt_ref, q_ref, *, n_seg, seg_len):
    cq = cq_ref[...]
    cos = cos_ref[...] * Q_SCALE
    sin = sin_ref[...] * Q_SCALE

    def head(h, carry):
        res = _dot_nt(cq, wqt_ref[h])
        qn = (res[:, :NOPE_DIM] * Q_SCALE).astype(BF16)
        qr = (res[:, NOPE_DIM:QK_DIM] * cos + res[:, QK_DIM:] * sin).astype(BF16)
        for s in range(n_seg):
            rows = slice(s * seg_len, (s + 1) * seg_len)
            q_ref[s, h, :, 0:NOPE_DIM] = qn[rows]
            q_ref[s, h, :, NOPE_DIM:QK_DIM] = qr[rows]
        return carry

    lax.fori_loop(0, N_HEADS, head, 0)


def _q_proj(cq, cos_t, sin_t, wqt, batch, seq, tm):
    n = cq.shape[0]
    n_seg = tm // seq
    table_blocks = cos_t.shape[0] // tm
    return pl.pallas_call(
        functools.partial(_q_proj_kernel, n_seg=n_seg, seg_len=seq),
        grid=(n // tm,),
        in_specs=[
            pl.BlockSpec((tm, Q_LORA), lambda i: (i, 0)),
            pl.BlockSpec((tm, ROPE_DIM), lambda i: (i % table_blocks, 0)),
            pl.BlockSpec((tm, ROPE_DIM), lambda i: (i % table_blocks, 0)),
            _resident((N_HEADS, 2 * NOPE_DIM, Q_LORA)),
        ],
        out_specs=pl.BlockSpec((n_seg, N_HEADS, seq, QK_DIM), lambda i: (i, 0, 0, 0)),
        out_shape=jax.ShapeDtypeStruct((batch, N_HEADS, seq, QK_DIM), BF16),
        compiler_params=_params(("arbitrary",)),
        name="q_proj",
    )(cq, cos_t, sin_t, wqt)


def _qkv_proj_kernel(cq_ref, cost_ref, sint_ref, wqt_ref, ckv_ref, kr_ref, wuk_ref, wuvt_ref,
                     qt_ref, k_ref, vt_ref):
    cq = cq_ref[...]
    cos_t = cost_ref[...] * Q_SCALE
    sin_t = sint_ref[...] * Q_SCALE
    ckv = ckv_ref[...].astype(BF16)
    krb = kr_ref[...].astype(BF16)
    ones = jnp.ones((V_AUG - V_DIM, ckv.shape[0]), BF16)

    for hp in range(N_HEADS // 2):
        for hh in range(2):
            h = 2 * hp + hh
            res = _dot_nt(wqt_ref[h], cq)
            qt_ref[0, h, 0:NOPE_DIM, :] = (res[0:NOPE_DIM] * Q_SCALE).astype(BF16)
            qt_ref[0, h, NOPE_DIM:QK_DIM, :] = (
                res[NOPE_DIM:QK_DIM] * cos_t + res[QK_DIM:] * sin_t).astype(BF16)
        pair_cols = slice(2 * hp * NOPE_DIM, 2 * (hp + 1) * NOPE_DIM)
        kn = _dot(ckv, wuk_ref[:, pair_cols]).astype(BF16)
        vt = _dot_nt(wuvt_ref[hp], ckv).astype(BF16)
        for hh in range(2):
            h = 2 * hp + hh
            k_ref[0, h, 0, :, 0:NOPE_DIM] = kn[:, hh * NOPE_DIM:(hh + 1) * NOPE_DIM]
            k_ref[0, h, 0, :, NOPE_DIM:QK_DIM] = krb
            vt_ref[0, h, 0, 0:V_DIM, :] = vt[hh * V_DIM:(hh + 1) * V_DIM]
            vt_ref[0, h, 0, V_DIM:V_AUG, :] = ones


def _qkv_proj(cq, cos_tt, sin_tt, wqt, ckv, kr, wuk, wuvt2, batch, seq, tm):
    n = cq.shape[0]
    tiles_per_seq = seq // tm
    row = lambda i: (i, 0)
    tab = lambda i: (0, i % tiles_per_seq)
    tile = lambda i: (i // tiles_per_seq, 0, i % tiles_per_seq, 0, 0)
    return pl.pallas_call(
        _qkv_proj_kernel,
        grid=(n // tm,),
        in_specs=[
            pl.BlockSpec((tm, Q_LORA), row),
            pl.BlockSpec((ROPE_DIM, tm), tab),
            pl.BlockSpec((ROPE_DIM, tm), tab),
            _resident((N_HEADS, 2 * NOPE_DIM, Q_LORA)),
            pl.BlockSpec((tm, KV_LORA), row),
            pl.BlockSpec((tm, ROPE_DIM), row),
            _resident((KV_LORA, N_HEADS * NOPE_DIM)),
            _resident((N_HEADS // 2, 2 * V_DIM, KV_LORA)),
        ],
        out_specs=[
            pl.BlockSpec((1, N_HEADS, QK_DIM, tm), lambda i: (i // tiles_per_seq, 0, 0, i % tiles_per_seq)),
            pl.BlockSpec((1, N_HEADS, 1, tm, QK_DIM), tile),
            pl.BlockSpec((1, N_HEADS, 1, V_AUG, tm), tile),
        ],
        out_shape=[
            jax.ShapeDtypeStruct((batch, N_HEADS, QK_DIM, seq), BF16),
            jax.ShapeDtypeStruct((batch, N_HEADS, tiles_per_seq, tm, QK_DIM), BF16),
            jax.ShapeDtypeStruct((batch, N_HEADS, tiles_per_seq, V_AUG, tm), BF16),
        ],
        compiler_params=_params(("arbitrary",)),
        name="qkv_proj",
    )(cq, cos_tt, sin_tt, wqt, ckv, kr, wuk, wuvt2)


def _flash_kernel(*refs, heads, tile, n_cast):
    (qt_ref, qtn_ref, knew_ref, vtnew_ref), refs = refs[:4], refs[4:]
    cast_in, o_ref, cast_out = refs[:n_cast], refs[n_cast], refs[n_cast + 1:2 * n_cast + 1]
    for src, dst in zip(cast_in, cast_out):
        dst[...] = src[0].astype(BF16)
    _flash_body(qt_ref, qtn_ref, knew_ref, vtnew_ref, o_ref, *refs[2 * n_cast + 1:], heads=heads, tile=tile)


def _flash_body(qt_ref, qtn_ref, knew_ref, vtnew_ref, o_ref,
                k_ref, vt_ref, s_ref, sf_ref, acc_ref, m_ref, cmax_ref, cmaxf_ref, bias_ref, *, heads, tile):
    qi = pl.program_id(2)
    for g in range(heads):
        k_ref[g, qi] = knew_ref[0, g, 0]
        vt_ref[g, qi] = vtnew_ref[0, g, 0]

    @pl.when((pl.program_id(0) == 0) & (pl.program_id(1) == 0) & (qi == 0))
    def _():
        key_chunk = lax.broadcasted_iota(jnp.int32, (tile, tile), 0) // CHUNK
        query_chunk = lax.broadcasted_iota(jnp.int32, (tile, tile), 1) // CHUNK
        bias_ref[...] = jnp.where(key_chunk <= query_chunk, 0.0, MASK_VALUE)

    def scores(g, j, slot):
        st = _dot(k_ref[g, j], qt_ref[0, g])
        s_ref[slot, g] = st
        cmax_ref[slot, g] = jnp.max(st, axis=0, keepdims=True)

    def next_first_scores(g):
        st = _dot(k_ref[g, 0], qtn_ref[0, g])
        sf_ref[g] = st
        cmaxf_ref[g] = jnp.max(st, axis=0, keepdims=True)

    def softmax(g, st, tile_max):
        m_old = m_ref[g]
        m_new = jnp.maximum(m_old, tile_max)
        m_ref[g] = m_new
        return jnp.exp2(st - m_new).astype(BF16), jnp.exp2(m_old - m_new)

    def values(g, j, p, alpha):
        acc_ref[g] = alpha * acc_ref[g] + _dot(vt_ref[g, j], p)

    def steps(j0, count, slot0, from_first=False):
        pending = None
        for t in range(count):
            cur = (slot0 + t) % 2
            for g in range(heads):
                if from_first and t == 0:
                    p, alpha = softmax(g, sf_ref[g], cmaxf_ref[g])
                else:
                    p, alpha = softmax(g, s_ref[cur, g], cmax_ref[cur, g])
                scores(g, j0 + t + 1, 1 - cur)
                if pending is not None:
                    values(*pending)
                pending = (g, j0 + t, p, alpha)
        values(*pending)

    for g in range(heads):
        m_ref[g] = jnp.full((1, tile), MASK_VALUE, F32)
        acc_ref[g] = jnp.zeros((V_AUG, tile), F32)

    @pl.when(qi == 0)
    def _():
        for g in range(heads):
            scores(g, 0, 0)

    @pl.when(qi >= 1)
    def _():
        steps(0, 1, 0, from_first=True)

    rest = jnp.maximum(qi - 1, 0)

    def four_steps(jj, carry):
        steps(1 + 4 * jj, 4, 1)
        return carry

    lax.fori_loop(0, rest // 4, four_steps, 0)

    @pl.when(rest % 4 >= 2)
    def _():
        steps(1 + 4 * (rest // 4), 2, 1)

    @pl.when(rest % 2 == 1)
    def _():
        steps(qi - 1, 1, 1)

    def finish(g, p, alpha):
        acc = alpha * acc_ref[g] + _dot(vt_ref[g, qi], p)
        o = acc[0:V_DIM] / acc[V_DIM:V_DIM + 1]
        o_ref[:, g * V_DIM:(g + 1) * V_DIM] = o.T.astype(BF16)

    diag_slot = qi % 2
    pending = None
    for g in range(heads):
        st = s_ref[diag_slot, g] + bias_ref[...]
        p, alpha = softmax(g, st, jnp.max(st, axis=0, keepdims=True))
        next_first_scores(g)
        if pending is not None:
            finish(*pending)
        pending = (g, p, alpha)
    finish(*pending)


def _flash(qt, k, vt, cast=(), heads=FLASH_HEADS):
    batch, _, _, seq = qt.shape
    n_tiles, tile = k.shape[2], k.shape[3]
    groups = N_HEADS // heads
    n_steps = batch * groups * n_tiles
    slab = lambda b, hg, qi: (b * groups + hg) * n_tiles + qi
    cast_in = [pl.BlockSpec((1, w.shape[1] // n_steps, w.shape[2]), lambda b, hg, qi: (0, slab(b, hg, qi), 0))
               for w in cast]
    cast_out = [pl.BlockSpec((w.shape[1] // n_steps, w.shape[2]), lambda b, hg, qi: (slab(b, hg, qi), 0))
                for w in cast]
    cast_shape = [jax.ShapeDtypeStruct(w.shape[1:], BF16) for w in cast]
    return pl.pallas_call(
        functools.partial(_flash_kernel, heads=heads, tile=tile, n_cast=len(cast)),
        grid=(batch, groups, n_tiles),
        in_specs=[
            pl.BlockSpec((1, heads, QK_DIM, tile), lambda b, hg, qi: (b, hg, 0, qi)),
            pl.BlockSpec((1, heads, QK_DIM, tile),
                         lambda b, hg, qi: (b, hg, 0, jnp.minimum(qi + 1, n_tiles - 1))),
            pl.BlockSpec((1, heads, 1, tile, QK_DIM), lambda b, hg, qi: (b, hg, qi, 0, 0)),
            pl.BlockSpec((1, heads, 1, V_AUG, tile), lambda b, hg, qi: (b, hg, qi, 0, 0)),
        ] + cast_in,
        out_specs=[pl.BlockSpec((tile, heads * V_DIM), lambda b, hg, qi: (b * n_tiles + qi, hg))] + cast_out,
        out_shape=[jax.ShapeDtypeStruct((batch * seq, N_HEADS * V_DIM), BF16)] + cast_shape,
        scratch_shapes=[
            pltpu.VMEM((heads, n_tiles, tile, QK_DIM), BF16),
            pltpu.VMEM((heads, n_tiles, V_AUG, tile), BF16),
            pltpu.VMEM((2, heads, tile, tile), F32),
            pltpu.VMEM((heads, tile, tile), F32),
            pltpu.VMEM((heads, V_AUG, tile), F32),
            pltpu.VMEM((heads, 1, tile), F32),
            pltpu.VMEM((2, heads, 1, tile), F32),
            pltpu.VMEM((heads, 1, tile), F32),
            pltpu.VMEM((tile, tile), F32),
        ],
        compiler_params=_params(("arbitrary", "arbitrary", "arbitrary")),
        name="flash",
    )(qt, qt, k, vt, *cast)


def _decode_kernel(q_ref, wuk_ref, wuv_ref, cc_ref, ckr_ref, cn_ref, krn_ref, o_ref,
                   qt, s_ref, m_s, l_s, acc_s, *, seq, halves):
    kt = pl.program_id(1)
    half = cc_ref.shape[1] // halves

    @pl.when(kt == 0)
    def _():
        eye = (lax.broadcasted_iota(jnp.int32, (ROPE_DIM, ROPE_DIM), 0)
               == lax.broadcasted_iota(jnp.int32, (ROPE_DIM, ROPE_DIM), 1)).astype(BF16)
        for hp in range(N_HEADS // 2):
            lat, rope_t = [], []
            for h in (2 * hp, 2 * hp + 1):
                w_h = wuk_ref[:, h * NOPE_DIM:(h + 1) * NOPE_DIM]
                lat.append(_dot_nt(w_h, q_ref[0, h, :, 0:NOPE_DIM]))
                rope_t.append(_dot_nt(eye, q_ref[0, h, :, NOPE_DIM:QK_DIM]))
            cols = slice(2 * hp * seq, 2 * (hp + 1) * seq)
            qt[0:KV_LORA, cols] = jnp.concatenate(lat, axis=1).astype(BF16)
            qt[KV_LORA:, cols] = jnp.concatenate(rope_t, axis=1).astype(BF16)
        m_s[...] = jnp.full(m_s.shape, MASK_VALUE, F32)
        l_s[...] = jnp.zeros(l_s.shape, F32)
        acc_s[...] = jnp.zeros(acc_s.shape, F32)

    def scores(kc, krt, slot):
        n = kc.shape[0]
        s_ref[slot, 0:n, :] = _dot(kc, qt[0:KV_LORA, :]) + _dot_tn(krt, qt[KV_LORA:, :])

    def update(kc, slot):
        n = kc.shape[0]
        st = s_ref[slot, 0:n, :]
        m_old = m_s[...]
        m_new = jnp.maximum(m_old, jnp.max(st, axis=0, keepdims=True))
        alpha = jnp.exp2(m_old - m_new)
        p = jnp.exp2(st - m_new)
        l_s[...] = alpha * l_s[...] + jnp.sum(p, axis=0, keepdims=True)
        acc_s[...] = alpha * acc_s[...] + _dot_tn(kc, p.astype(BF16))
        m_s[...] = m_new

    parts = [cc_ref[0, i * half:(i + 1) * half, :].astype(BF16) for i in range(halves)]
    for i in range(halves):
        scores(parts[i], ckr_ref[0, :, i * half:(i + 1) * half].astype(BF16), i)
    for i in range(halves):
        update(parts[i], i)

    @pl.when(kt == pl.num_programs(1) - 1)
    def _():
        new = cn_ref[...].astype(BF16)
        scores(new, krn_ref[0].astype(BF16), 0)
        update(new, 0)
        o_lat = (acc_s[...] / l_s[...]).T.astype(BF16)
        for h in range(N_HEADS):
            o_h = _dot(o_lat[h * seq:(h + 1) * seq], wuv_ref[:, h * V_DIM:(h + 1) * V_DIM])
            o_ref[:, h * V_DIM:(h + 1) * V_DIM] = o_h.astype(BF16)


def _decode_attn(q, wuk, wuv, cache_ckv, cache_kr, ckv_new, kr_new, key_tile=DECODE_KEY_TILE, halves=2):
    batch, _, seq, _ = q.shape
    past = cache_ckv.shape[1]
    cols = N_HEADS * seq
    return pl.pallas_call(
        functools.partial(_decode_kernel, seq=seq, halves=halves),
        grid=(batch, past // key_tile),
        in_specs=[
            pl.BlockSpec((1, N_HEADS, seq, QK_DIM), lambda b, kt: (b, 0, 0, 0)),
            _resident((KV_LORA, N_HEADS * NOPE_DIM)),
            _resident((KV_LORA, N_HEADS * V_DIM)),
            pl.BlockSpec((1, key_tile, KV_LORA), lambda b, kt: (b, kt, 0)),
            pl.BlockSpec((1, ROPE_DIM, key_tile), lambda b, kt: (b, 0, kt)),
            pl.BlockSpec((seq, KV_LORA), lambda b, kt: (b, 0)),
            pl.BlockSpec((1, ROPE_DIM, seq), lambda b, kt: (b, 0, 0)),
        ],
        out_specs=pl.BlockSpec((seq, N_HEADS * V_DIM), lambda b, kt: (b, 0)),
        out_shape=jax.ShapeDtypeStruct((batch * seq, N_HEADS * V_DIM), BF16),
        scratch_shapes=[
            pltpu.VMEM((KV_LORA + ROPE_DIM, cols), BF16),
            pltpu.VMEM((halves, key_tile // halves, cols), F32),
            pltpu.VMEM((1, cols), F32),
            pltpu.VMEM((1, cols), F32),
            pltpu.VMEM((KV_LORA, cols), F32),
        ],
        compiler_params=_params(("arbitrary", "arbitrary")),
        name="decode_attn",
    )(q, wuk, wuv, cache_ckv, cache_kr, ckv_new, kr_new)


def _window_sum(ext, tmp, cols, w):
    end = ext.shape[0]
    levels = w.bit_length() - 1
    for k in range(levels):
        lo, shift = 8 * (k + 1), 2 ** k
        if k == 0:
            val = ext[lo:end, cols] + ext[lo - shift:end - shift, cols]
        else:
            below = tmp.at[(k - 1) % 2]
            val = below[lo:end, :] + below[lo - shift:end - shift, :]
        if k < levels - 1:
            tmp[k % 2, lo:end, :] = val
    return val[POOL_HALO - 8 * levels:]


def _merge_kernel(z_ref, zprev_ref, pre_ref, att_ref, gate_ref, x_ref, pw_ref, ps_ref,
                  wpo_ref, wmo_ref, wo_ref, h_ref, ext, tmp, ypool, *, n_seg, seg_len, tiles_per_seq, pos0):
    t = pl.program_id(0) % tiles_per_seq
    row = lax.broadcasted_iota(jnp.int32, (seg_len, 1), 0)
    pos = pos0 + t * seg_len + row
    br_b = _dot(att_ref[...], wmo_ref[...])
    for s in range(n_seg):
        if tiles_per_seq == 1:
            halo = pre_ref[s]
        else:
            halo = jnp.where(t == 0, pre_ref[0], zprev_ref[...])
        ext[0:POOL_HALO, :] = halo
        ext[POOL_HALO:, :] = z_ref[s * seg_len:(s + 1) * seg_len, :]
        for g, w in enumerate(POOL_WINDOWS):
            cols = slice(g * POOL_GROUP, (g + 1) * POOL_GROUP)
            cur = ext[POOL_HALO:, cols]
            total = _window_sum(ext, tmp, cols, w)
            count = jnp.minimum(pos + 1, w).astype(F32)
            diff = total / count - cur
            y = _dot(diff.astype(BF16), pw_ref[g]) * ps_ref[:, cols]
            ypool[s * seg_len:(s + 1) * seg_len, cols] = y.astype(BF16)
    br_a = _dot(ypool[...], wpo_ref[...])
    merged = gate_ref[:, 0:D_MODEL] * br_a + gate_ref[:, D_MODEL:] * br_b
    h_ref[...] = x_ref[...] + _dot(merged.astype(BF16), wo_ref[...])


def _merge(z, prefix, att, gates, x, pool_w, pool_scale, w_pool_out, w_mla_out, w_out, seq, pos0, tm):
    n = x.shape[0]
    tiles_per_seq = max(seq // tm, 1)
    n_seg = max(tm // seq, 1)
    seg_len = tm // n_seg
    halo_blocks = tm // POOL_HALO
    row = lambda i: (i, 0)
    return pl.pallas_call(
        functools.partial(_merge_kernel, n_seg=n_seg, seg_len=seg_len,
                          tiles_per_seq=tiles_per_seq, pos0=pos0),
        grid=(n // tm,),
        in_specs=[
            pl.BlockSpec((tm, D_POOL), row),
            pl.BlockSpec((POOL_HALO, D_POOL), lambda i: (jnp.maximum(i * halo_blocks - 1, 0), 0)),
            pl.BlockSpec((n_seg, POOL_HALO, D_POOL), lambda i: (i // tiles_per_seq, 0, 0)),
            pl.BlockSpec((tm, N_HEADS * V_DIM), row),
            pl.BlockSpec((tm, 2 * D_MODEL), row),
            pl.BlockSpec((tm, D_MODEL), row),
            _resident((len(POOL_WINDOWS), POOL_GROUP, POOL_GROUP)),
            _resident((1, D_POOL)),
            _resident((D_POOL, D_MODEL)),
            _resident((N_HEADS * V_DIM, D_MODEL)),
            _resident((D_MODEL, D_MODEL)),
        ],
        out_specs=pl.BlockSpec((tm, D_MODEL), row),
        out_shape=jax.ShapeDtypeStruct((n, D_MODEL), F32),
        scratch_shapes=[
            pltpu.VMEM((POOL_HALO + seg_len, D_POOL), F32),
            pltpu.VMEM((2, POOL_HALO + seg_len, POOL_GROUP), F32),
            pltpu.VMEM((tm, D_POOL), BF16),
        ],
        compiler_params=_params(("arbitrary",)),
        name="merge",
    )(z, z, prefix, att, gates, x, pool_w, pool_scale, w_pool_out, w_mla_out, w_out)


def _ffn_kernel(h_ref, g2_ref, wa_ref, wb_ref, cw_ref, cb_ref, pre_ref, wd_ref, fg_ref,
                y_ref, tail_ref, hn, ext, carry, *, n_seg, seg_len, tiles_per_seq):
    i = pl.program_id(0)
    j = pl.program_id(1)
    t = i % tiles_per_seq

    @pl.when(j == 0)
    def _():
        hn[...] = _rms(h_ref[...], g2_ref[...]).astype(BF16)
        y_ref[...] = jnp.zeros(y_ref.shape, F32)

    if tiles_per_seq > 1:
        @pl.when(t == 0)
        def _():
            carry[j] = pre_ref[0]

    tm = hn.shape[0]
    rc = tm // FFN_ROW_CHUNKS
    up = []
    for c in range(FFN_ROW_CHUNKS):
        hc = hn[c * rc:(c + 1) * rc, :]
        up.append((_dot(hc, wa_ref[...]), _dot(hc, wb_ref[...])))

    def conv_gelu(seg, lo, a_rows, b_rows):
        n = a_rows.shape[0]
        base = CONV_HALO + lo
        ext[seg, base:base + n, :] = a_rows
        c = cb_ref[...] + ext[seg, base - 2:base - 2 + n, :] * cw_ref[0:1, :]
        c = c + ext[seg, base - 1:base - 1 + n, :] * cw_ref[1:2, :]
        c = c + a_rows * cw_ref[2:3, :]
        gelu = 0.5 * c * (1.0 + lax.erf(c * (2.0 ** -0.5)))
        return (gelu * b_rows).astype(BF16)

    for c in range(FFN_ROW_CHUNKS):
        a_c, b_c = up[c]
        if n_seg == 1:
            if c == 0:
                ext[0, CONV_HALO - (CONV_W - 1):CONV_HALO, :] = carry[j] if tiles_per_seq > 1 else pre_ref[0]
            gated_c = conv_gelu(0, c * rc, a_c, b_c)
            if c == FFN_ROW_CHUNKS - 1:
                tail_rows = a_c[rc - (CONV_W - 1):]
                tail_ref[0, 0] = tail_rows
                if tiles_per_seq > 1:
                    carry[j] = tail_rows
        else:
            per_chunk = n_seg // FFN_ROW_CHUNKS
            parts = []
            for k in range(per_chunk):
                s = c * per_chunk + k
                rows = slice(k * seg_len, (k + 1) * seg_len)
                ext[s, CONV_HALO - (CONV_W - 1):CONV_HALO, :] = pre_ref[s]
                parts.append(conv_gelu(s, 0, a_c[rows], b_c[rows]))
                tail_ref[0, s] = a_c[rows][seg_len - (CONV_W - 1):]
            gated_c = jnp.concatenate(parts, axis=0)
        y_ref[c * rc:(c + 1) * rc, :] += _dot(gated_c, wd_ref[...])

    @pl.when(j == pl.num_programs(1) - 1)
    def _():
        y_ref[...] = _rms(h_ref[...] + y_ref[...], fg_ref[...])


def _ffn(h, norm2_g, w_up, conv_w, conv_b, prefix, w_down, final_g, seq, tm, tf=TF_FFN):
    n = h.shape[0]
    tiles_per_seq = max(seq // tm, 1)
    n_seg = max(tm // seq, 1)
    seg_len = tm // n_seg
    nff = D_FF // tf
    return pl.pallas_call(
        functools.partial(_ffn_kernel, n_seg=n_seg, seg_len=seg_len, tiles_per_seq=tiles_per_seq),
        grid=(n // tm, nff),
        in_specs=[
            pl.BlockSpec((tm, D_MODEL), lambda i, j: (i, 0)),
            _resident((1, D_MODEL)),
            pl.BlockSpec((D_MODEL, tf), lambda i, j: (0, j)),
            pl.BlockSpec((D_MODEL, tf), lambda i, j: (0, nff + j)),
            pl.BlockSpec((CONV_W, tf), lambda i, j: (0, j)),
            pl.BlockSpec((1, tf), lambda i, j: (0, j)),
            pl.BlockSpec((n_seg, CONV_W - 1, tf), lambda i, j: (i // tiles_per_seq, 0, j)),
            pl.BlockSpec((tf, D_MODEL), lambda i, j: (j, 0)),
            _resident((1, D_MODEL)),
        ],
        out_specs=[
            pl.BlockSpec((tm, D_MODEL), lambda i, j: (i, 0)),
            pl.BlockSpec((1, n_seg, CONV_W - 1, tf), lambda i, j: (i, 0, 0, j)),
        ],
        out_shape=[
            jax.ShapeDtypeStruct((n, D_MODEL), F32),
            jax.ShapeDtypeStruct((n // tm, n_seg, CONV_W - 1, D_FF), F32),
        ],
        scratch_shapes=[
            pltpu.VMEM((tm, D_MODEL), BF16),
            pltpu.VMEM((n_seg, CONV_HALO + seg_len, tf), F32),
            pltpu.VMEM((nff, CONV_W - 1, tf), F32),
        ],
        compiler_params=_params(("arbitrary", "arbitrary")),
        name="ffn",
    )(h, norm2_g, w_up, w_up, conv_w, conv_b, prefix, w_down, final_g)


def _rope_tables(pos0, length, rows):
    pos = (pos0 + np.arange(length)).astype(np.float64)
    inv = ROPE_BASE ** (-(np.arange(ROPE_DIM // 2, dtype=np.float64) * 2.0 / ROPE_DIM))
    ang = pos[:, None] * inv[None, :]
    cos, sin = np.cos(ang), np.sin(ang)
    cos_t = np.concatenate([cos, cos], axis=-1).astype(np.float32)
    sin_t = np.concatenate([-sin, sin], axis=-1).astype(np.float32)
    reps = max(rows // length, 1)
    return np.tile(cos_t, (reps, 1)), np.tile(sin_t, (reps, 1))


def _swap_halves(w):
    return jnp.concatenate([w[..., ROPE_DIM // 2:], w[..., :ROPE_DIM // 2]], axis=-1)


def kernel(x_prompt, x_sample, cache_ckv, cache_krope, state_pool, state_conv, norm1_g, w_in, pool_w, pool_scale, w_pool_out, q_norm_g, w_uq, kv_norm_g, w_uk, w_uv, w_mla_out, w_out, norm2_g, w_up, conv_w, conv_b, w_down, final_g):
    l = 0
    bp, sp, _ = x_prompt.shape
    bs, ss, _ = x_sample.shape
    past = cache_ckv.shape[2]

    w = w_in[l]
    w_all = _regroup_w_in(w.T)
    wq = w_uq[l].reshape(Q_LORA, N_HEADS, QK_DIM)
    wq = jnp.concatenate([wq, _swap_halves(wq[..., NOPE_DIM:])], axis=-1)
    wqt = wq.transpose(1, 2, 0).astype(BF16)
    wuk = w_uk[l].astype(BF16).reshape(KV_LORA, N_HEADS * NOPE_DIM)
    wuv = w_uv[l].astype(BF16).reshape(KV_LORA, N_HEADS * V_DIM)
    wuvt2 = wuv.reshape(KV_LORA, N_HEADS // 2, 2 * V_DIM).transpose(1, 2, 0)
    pw = pool_w[l].astype(BF16)
    wpo = w_pool_out[l].astype(BF16)
    g1 = norm1_g[l][None]
    qg = q_norm_g[l][None]
    kvg = kv_norm_g[l][None]
    g2 = norm2_g[l][None]
    fg = final_g[None]
    ps = pool_scale[l][None]
    cb = conv_b[l][None]
    cw = conv_w[l]

    tm_in, tm_head, tm_merge, tm_ffn = TM_IN_PROJ, TM_QKV, TM_MERGE, TM_FFN
    xp = x_prompt.reshape(bp * sp, D_MODEL)
    cos_p, sin_p = _rope_tables(0, sp, sp)
    z_p, cq_p, ckv_p, kr_p, krt_p, gates_p = _in_proj(
        xp, g1, w_all, qg, kvg, cos_p, sin_p, bp, sp, tm_in)
    qt_p, k_p, vt_p = _qkv_proj(cq_p, cos_p.T, sin_p.T, wqt, ckv_p, kr_p, wuk, wuvt2, bp, sp, tm_head)
    att_p, wmo, wo, wup, wdn = _flash(
        qt_p, k_p, vt_p, cast=(w_mla_out[l][None], w_out[l][None], w_up[l][None], w_down[l][None]))
    pool0 = jnp.zeros((bp, POOL_HALO, D_POOL), F32)
    h_p = _merge(z_p, pool0, att_p, gates_p, xp, pw, ps, wpo, wmo, wo, sp, 0, tm_merge)
    conv0 = jnp.zeros((bp, CONV_W - 1, D_FF), F32)
    y_p, tail_p = _ffn(h_p, g2, wup, cw, cb, conv0, wdn, fg, sp, tm_ffn)

    ns = bs * ss
    xs = x_sample.reshape(ns, D_MODEL)
    cos_s, sin_s = _rope_tables(past, ss, ns)
    z_s, cq_s, ckv_s, _, krt_s, gates_s = _in_proj(
        xs, g1, w_all, qg, kvg, cos_s, sin_s, bs, ss, tm_in)
    q_s = _q_proj(cq_s, cos_s, sin_s, wqt, bs, ss, ns)
    cache_krt = jnp.swapaxes(cache_krope[l], 1, 2)
    att_s = _decode_attn(q_s, wuk, wuv, cache_ckv[l], cache_krt, ckv_s, krt_s)
    pool_pre = jnp.pad(state_pool[l], ((0, 0), (POOL_HALO - POOL_STATE, 0), (0, 0)))
    h_s = _merge(z_s, pool_pre, att_s, gates_s, xs, pw, ps, wpo, wmo, wo, ss, past, tm_merge)
    y_s, tail_s = _ffn(h_s, g2, wup, cw, cb, state_conv[l], wdn, fg, ss, ns)

    tiles_per_seq = sp // tm_ffn
    p_conv = tail_p.reshape(bp, tiles_per_seq, CONV_W - 1, D_FF)[:, -1]
    s_conv = tail_s.reshape(bs, CONV_W - 1, D_FF)
    return (
        y_p.reshape(bp, sp, D_MODEL),
        y_s.reshape(bs, ss, D_MODEL),
        ckv_p.reshape(1, bp, sp, KV_LORA),
        jnp.swapaxes(krt_p, 1, 2)[None],
        z_p.reshape(bp, sp, D_POOL)[:, sp - POOL_STATE:][None],
        p_conv[None],
        ckv_s.reshape(1, bs, ss, KV_LORA),
        jnp.swapaxes(krt_s, 1, 2)[None],
        z_s.reshape(bs, ss, D_POOL)[:, ss - POOL_STATE:][None],
        s_conv[None],
    )
```
